```python
import math
import jax, jax.numpy as jnp
from jax import lax
import numpy as np

D_MODEL = 1024
BATCH = 16
SEQ = 2048
DEPTH = 2

CHUNK = 64
D_MIX = D_MODEL
D_CONV = D_MIX // 2
CONV_WIDTH = 31
D_GLA_V = D_MIX - D_CONV
GLA_HEADS = 4
GLA_DV = D_GLA_V // GLA_HEADS
GLA_DK = GLA_DV // 2
D_GLA_K = GLA_HEADS * GLA_DK
GATE_RANK = 16
GATE_TAU = 16.0
SPLIT_SIZES = (D_CONV, D_CONV, D_GLA_K, D_GLA_K, D_GLA_V, D_GLA_V, GATE_RANK)
IN_COLS = sum(SPLIT_SIZES)
N_EXPERTS = 16
N_GROUPS = 4
EXPERTS_PER_GROUP = N_EXPERTS // N_GROUPS
TOP_K = 2
D_EXPERT = 704
EPS = 1e-6

kernel_name = 'hybrid_conv_gla_groupmoe_adaln_block'


def rmsnorm(x, g):
    xf = x.astype(jnp.float32)
    y = xf * lax.rsqrt(jnp.mean(xf * xf, axis=-1, keepdims=True) + EPS)
    return (y * g.astype(jnp.float32)).astype(x.dtype)


def layernorm(x, g, b):
    xf = x.astype(jnp.float32)
    mu = jnp.mean(xf, axis=-1, keepdims=True)
    var = jnp.mean(jnp.square(xf - mu), axis=-1, keepdims=True)
    y = (xf - mu) * lax.rsqrt(var + EPS)
    return (y * g.astype(jnp.float32) + b.astype(jnp.float32)).astype(x.dtype)


def conv_group(val, gate, w_dw, b_dw, ln_g, ln_b):
    u = val * jax.nn.sigmoid(gate)
    u = lax.conv_general_dilated(
        u, w_dw[:, None, :].astype(u.dtype), window_strides=(1,),
        padding=[(CONV_WIDTH - 1, 0)],
        dimension_numbers=('NWC', 'WIO', 'NWC'),
        feature_group_count=D_CONV) + b_dw
    u = layernorm(u, ln_g, ln_b)
    return jax.nn.silu(u)


def gla_group(q, k, v, g, lr, w_up, b_up, norm_g):
    B, S, _ = q.shape
    nc = S // CHUNK
    q = q.reshape(B, nc, CHUNK, GLA_HEADS, GLA_DK) * (GLA_DK ** -0.5)
    k = k.reshape(B, nc, CHUNK, GLA_HEADS, GLA_DK)
    v = v.reshape(B, nc, CHUNK, GLA_HEADS, GLA_DV)
    log_a = jax.nn.log_sigmoid((lr @ w_up + b_up).astype(jnp.float32)) / GATE_TAU
    log_a = log_a.reshape(B, nc, CHUNK, GLA_HEADS, GLA_DK)
    b_cum = jnp.cumsum(log_a, axis=2)
    b_end = b_cum[:, :, -1]
    k_dec = k * jnp.exp(b_end[:, :, None] - b_cum).astype(k.dtype)
    upd = jnp.einsum('bnchk,bnchv->nbhkv', k_dec, v)
    dec = jnp.exp(b_end).astype(upd.dtype).transpose(1, 0, 2, 3)[..., None]

    def step(state, inp):
        d, u = inp
        state = d * state + u
        return state, state

    _, states = lax.scan(step, jnp.zeros_like(upd[0]), (dec, upd))
    o = jnp.einsum('bnchk,nbhkv->bnchv', q, states).reshape(B, S, GLA_HEADS, GLA_DV)
    o = rmsnorm(o, norm_g)
    return o.reshape(B, S, D_GLA_V) * jax.nn.silu(g)


def grouped_moe(h, w_router, b_router, w1, w3, w2):
    B, S, D = h.shape
    hf = h.reshape(B * S, D)
    probs = jax.nn.softmax((hf @ w_router).astype(jnp.float32), axis=-1)
    sel = probs + b_router.astype(jnp.float32)
    grp = sel.reshape(-1, N_GROUPS, EXPERTS_PER_GROUP)
    grp_score = lax.top_k(grp, TOP_K)[0].sum(-1)
    best = jnp.argmax(grp_score, axis=-1)
    in_grp = jax.nn.one_hot(best, N_GROUPS, dtype=jnp.bool_)[..., None]
    masked = jnp.where(in_grp, grp, -jnp.inf).reshape(-1, N_EXPERTS)
    _, idx = lax.top_k(masked, TOP_K)
    w = jnp.take_along_axis(probs, idx, axis=-1)
    w = w / jnp.sum(w, axis=-1, keepdims=True)
    comb = jnp.einsum('tk,tke->te', w, jax.nn.one_hot(idx, N_EXPERTS, dtype=jnp.float32))
    comb = comb.astype(h.dtype)
    y = jnp.zeros_like(hf)
    for e in range(N_EXPERTS):
        he = jax.nn.silu(hf @ w1[e]) * (hf @ w3[e])
        y = y + comb[:, e:e + 1] * (he @ w2[e])
    return y.reshape(B, S, D)


def setup_inputs(seed: int = 0) -> dict:
    key = jax.random.key(seed)
    ks = jax.random.split(key, 24)
    f32 = jnp.float32
    nrm = lambda k, shape, s: jax.random.normal(k, shape, f32) * s
    return {
        'x': nrm(ks[0], (BATCH, SEQ, D_MODEL), 1.0),
        'c': nrm(ks[1], (BATCH, D_MODEL), 1.0),
        'w_ada': nrm(ks[2], (DEPTH, D_MODEL, 6 * D_MODEL), 0.3 * D_MODEL ** -0.5),
        'b_ada': nrm(ks[3], (DEPTH, 6 * D_MODEL), 0.05),
        'g_pre_mix': 1.0 + nrm(ks[4], (DEPTH, D_MODEL), 0.05),
        'g_post_mix': 1.0 + nrm(ks[5], (DEPTH, D_MODEL), 0.05),
        'g_pre_ffn': 1.0 + nrm(ks[6], (DEPTH, D_MODEL), 0.05),
        'g_post_ffn': 1.0 + nrm(ks[7], (DEPTH, D_MODEL), 0.05),
        'w_in': nrm(ks[8], (DEPTH, D_MODEL, IN_COLS), D_MODEL ** -0.5),
        'w_dw': nrm(ks[9], (DEPTH, CONV_WIDTH, D_CONV), CONV_WIDTH ** -0.5),
        'b_dw': nrm(ks[10], (DEPTH, D_CONV), 0.02),
        'conv_ln_g': 1.0 + nrm(ks[11], (DEPTH, D_CONV), 0.05),
        'conv_ln_b': nrm(ks[12], (DEPTH, D_CONV), 0.02),
        'w_gate_up': nrm(ks[13], (DEPTH, GATE_RANK, D_GLA_K), GATE_RANK ** -0.5),
        'b_gate': nrm(ks[14], (DEPTH, D_GLA_K), 0.1),
        'gla_norm_g': 1.0 + nrm(ks[15], (DEPTH, GLA_DV), 0.05),
        'w_out': nrm(ks[16], (DEPTH, D_MIX, D_MODEL), D_MIX ** -0.5),
        'w_router': nrm(ks[17], (D_MODEL, N_EXPERTS), D_MODEL ** -0.5),
        'b_router': nrm(ks[18], (N_EXPERTS,), 0.01),
        'w1': nrm(ks[19], (DEPTH, N_EXPERTS, D_MODEL, D_EXPERT), D_MODEL ** -0.5),
        'w3': nrm(ks[20], (DEPTH, N_EXPERTS, D_MODEL, D_EXPERT), D_MODEL ** -0.5),
        'w2': nrm(ks[21], (DEPTH, N_EXPERTS, D_EXPERT, D_MODEL), D_EXPERT ** -0.5),
    }


def reference(x, c, w_ada, b_ada, g_pre_mix, g_post_mix, g_pre_ffn, g_post_ffn,
              w_in, w_dw, b_dw, conv_ln_g, conv_ln_b, w_gate_up, b_gate, gla_norm_g,
              w_out, w_router, b_router, w1, w3, w2):
    split_at = [int(s) for s in np.cumsum(SPLIT_SIZES)[:-1]]
    c_act = jax.nn.silu(c)
    for l in range(DEPTH):
        mod = c_act @ w_ada[l] + b_ada[l]
        sh1, sc1, g1, sh2, sc2, g2 = [m[:, None, :] for m in jnp.split(mod, 6, axis=-1)]
        h = rmsnorm(x, g_pre_mix[l]) * (1.0 + sc1) + sh1
        proj = h @ w_in[l]
        cv, cg, q, k, v, og, lr = jnp.split(proj, split_at, axis=-1)
        y_conv = conv_group(cv, cg, w_dw[l], b_dw[l], conv_ln_g[l], conv_ln_b[l])
        y_gla = gla_group(q, k, v, og, lr, w_gate_up[l], b_gate[l], gla_norm_g[l])
        y = jnp.concatenate([y_conv, y_gla], axis=-1) @ w_out[l]
        x = x + g1 * rmsnorm(y, g_post_mix[l])
        h = rmsnorm(x, g_pre_ffn[l]) * (1.0 + sc2) + sh2
        y = grouped_moe(h, w_router, b_router, w1[l], w3[l], w2[l])
        x = x + g2 * rmsnorm(y, g_post_ffn[l])
    return x
```

```python
import functools

import jax
import jax.numpy as jnp
from jax import lax
from jax.experimental import pallas as pl
from jax.experimental.pallas import tpu as pltpu

CHUNK = 64
CONV_WIDTH = 31
D_CONV = 512
GLA_HEADS = 4
GLA_DV = 128
GLA_DK = 64
D_GLA_K = GLA_HEADS * GLA_DK
D_GLA_V = GLA_HEADS * GLA_DV
GATE_RANK = 16
GATE_TAU = 16.0
N_EXPERTS = 16
N_GROUPS = 4
EXPERTS_PER_GROUP = 4
PAIRS_PER_GROUP = 6
N_CLASSES = N_GROUPS * PAIRS_PER_GROUP
D_EXPERT = 704
D_EXPERT_PAD = 768
EPS = 1e-6

SUBLANES = 8
CONV_PAD = 32
CONV_ROWS = 32
SEQ_TILE = 512
MOE_TILE = 256
COMBINE_TILE = 256
GATHER_UNROLL = 8

F32 = jnp.float32
BF16 = jnp.bfloat16
NT_DIMS = (((1,), (1,)), ((), ()))


def _sigmoid(x):
    return 1.0 / (1.0 + jnp.exp(-x))


def _rms(x, g):
    ms = jnp.mean(x * x, axis=-1, keepdims=True)
    return x * lax.rsqrt(ms + EPS) * g


def _split_bf16(x):
    hi = x.astype(BF16)
    lo = (x - hi.astype(F32)).astype(BF16)
    return hi, lo


def _mod_kernel(c_ref, w_ref, b_ref, o_ref):
    c = c_ref[...]
    ca = c * _sigmoid(c)
    a_hi, a_lo = _split_bf16(ca)
    w_hi, w_lo = _split_bf16(w_ref[0])
    acc = jnp.dot(a_hi, w_hi, preferred_element_type=F32)
    acc += jnp.dot(a_lo, w_hi, preferred_element_type=F32)
    acc += jnp.dot(a_hi, w_lo, preferred_element_type=F32)
    o_ref[0] = acc + b_ref[0]


def _modulation(c, w_ada, b_ada):
    depth, d, d6 = w_ada.shape
    b = c.shape[0]
    nblk = d6 // d
    return pl.pallas_call(
        _mod_kernel,
        grid=(depth, nblk),
        in_specs=[
            pl.BlockSpec((b, d), lambda l, n: (0, 0)),
            pl.BlockSpec((1, d, d), lambda l, n: (l, 0, n)),
            pl.BlockSpec((1, 1, d), lambda l, n: (l, 0, n)),
        ],
        out_specs=pl.BlockSpec((1, b, d), lambda l, n: (l, 0, n)),
        out_shape=jax.ShapeDtypeStruct((depth, b, d6), F32),
        compiler_params=pltpu.CompilerParams(
            dimension_semantics=("arbitrary", "arbitrary"),
            vmem_limit_bytes=32 * 1024 * 1024),
        name="adaln_mod",
    )(c, w_ada, b_ada.reshape(depth, 1, d6))


def _route(logits_t, br):
    m = jnp.max(logits_t, axis=0, keepdims=True)
    e = jnp.exp(logits_t - m)
    probs = e / jnp.sum(e, axis=0, keepdims=True)
    sel = probs + br
    one = jnp.ones_like(m)
    zero = jnp.zeros_like(m)
    picked, gscore = [], []
    for g in range(N_GROUPS):
        rows = [sel[g * EXPERTS_PER_GROUP + k:g * EXPERTS_PER_GROUP + k + 1, :]
                for k in range(EXPERTS_PER_GROUP)]
        score = zero
        for k in range(EXPERTS_PER_GROUP):
            rank = zero
            for k2 in range(EXPERTS_PER_GROUP):
                if k2 == k:
                    continue
                ahead = (rows[k2] >= rows[k]) if k2 < k else (rows[k2] > rows[k])
                rank = rank + jnp.where(ahead, one, zero)
            pk = jnp.where(rank < 2.0, one, zero)
            picked.append(pk)
            score = score + pk * rows[k]
        gscore.append(score)
    best = gscore[0]
    for g in range(1, N_GROUPS):
        best = jnp.maximum(best, gscore[g])
    taken = zero
    flag = [zero] * EXPERTS_PER_GROUP
    prob = [zero] * EXPERTS_PER_GROUP
    gbase = zero
    for g in range(N_GROUPS):
        isb = jnp.where(gscore[g] == best, one, zero) * (one - taken)
        taken = taken + isb
        gbase = gbase + isb * float(g * PAIRS_PER_GROUP)
        for k in range(EXPERTS_PER_GROUP):
            ei = g * EXPERTS_PER_GROUP + k
            flag[k] = flag[k] + isb * picked[ei]
            prob[k] = prob[k] + isb * probs[ei:ei + 1, :]
    f0, f1, f2, f3 = flag
    pair = f0 * (f2 * 1.0 + f3 * 2.0) + (one - f0) * (f1 * (f2 * 3.0 + f3 * 4.0) + (one - f1) * 5.0)
    cls = gbase + pair
    wa_raw = f0 * prob[0] + (one - f0) * (f1 * prob[1] + (one - f1) * prob[2])
    tot = f0 * prob[0] + f1 * prob[1] + f2 * prob[2] + f3 * prob[3]
    wa = wa_raw / tot
    wb = (tot - wa_raw) / tot
    return cls, wa, wb


def _mix_kernel(x_ref, mod_ref, gpre_ref, gpost_ref, gffn_ref, wnat_ref, wt_ref, wdw_ref, bdw_ref,
                lng_ref, lnb_ref, wup_ref, bup_ref, gnorm_ref, wout_ref, wr_ref, br_ref,
                ubd_ref, ebd_ref,
                x1_ref, h2p_ref, route_ref,
                ubuf, ush, st_ref, sbd_ref, ycat_ref, *, ts):
    j = pl.program_id(1)
    nch = ts // CHUNK

    @pl.when(j == 0)
    def _():
        ubuf[0:CONV_PAD, :] = jnp.zeros((CONV_PAD, D_CONV), F32)
        st_ref[...] = jnp.zeros_like(st_ref)
        sbd_ref[...] = jnp.zeros_like(sbd_ref)

    @pl.when(j > 0)
    def _():
        ubuf[0:CONV_PAD, :] = ubuf[ts:ts + CONV_PAD, :]

    x = x_ref[0]
    sh1 = mod_ref[0, 0:1, :]
    sc1 = mod_ref[0, 1:2, :]
    g1 = mod_ref[0, 2:3, :]
    sh2 = mod_ref[0, 3:4, :]
    sc2 = mod_ref[0, 4:5, :]

    h = _rms(x, gpre_ref[...]) * (1.0 + sc1) + sh1
    hb = h.astype(BF16)

    cvg = jnp.dot(hb, wnat_ref[:, 0:2 * D_CONV], preferred_element_type=F32)
    ubuf[CONV_PAD:CONV_PAD + ts, :] = cvg[:, 0:D_CONV] * _sigmoid(cvg[:, D_CONV:2 * D_CONV])

    for r in range(1, SUBLANES):
        ush[r - 1] = ubuf[r:r + ts + CONV_PAD - SUBLANES, :]

    def conv_block(rb, carry):
        r0 = pl.multiple_of(rb * CONV_ROWS, CONV_ROWS)
        acc = jnp.broadcast_to(bdw_ref[...], (CONV_ROWS, D_CONV))
        for tap in range(CONV_WIDTH):
            off = CONV_PAD - (CONV_WIDTH - 1) + tap
            r, a8 = off % SUBLANES, off - off % SUBLANES
            if r == 0:
                win = ubuf[pl.ds(r0 + a8, CONV_ROWS), :]
            else:
                win = ush[r - 1, pl.ds(r0 + a8, CONV_ROWS), :]
            acc = acc + wdw_ref[tap:tap + 1, :] * win
        mu = jnp.mean(acc, axis=-1, keepdims=True)
        cen = acc - mu
        var = jnp.mean(cen * cen, axis=-1, keepdims=True)
        yn = cen * lax.rsqrt(var + EPS) * lng_ref[...] + lnb_ref[...]
        ycat_ref[pl.ds(r0, CONV_ROWS), 0:D_CONV] = (yn * _sigmoid(yn)).astype(BF16)
        return carry

    lax.fori_loop(0, ts // CONV_ROWS, conv_block, 0)

    qvo = jnp.dot(hb, wnat_ref[:, 2 * D_CONV:], preferred_element_type=F32)
    q = (qvo[:, 0:D_GLA_K] * (GLA_DK ** -0.5)).astype(BF16)
    v = qvo[:, D_GLA_K:D_GLA_K + D_GLA_V].astype(BF16)
    og = qvo[:, D_GLA_K + D_GLA_V:]
    tt = lax.dot_general(wt_ref[...], hb, NT_DIMS, preferred_element_type=F32)
    k_t = tt[0:D_GLA_K, :]
    lr_t = tt[D_GLA_K:, :].astype(BF16)
    z_t = jnp.dot(wup_ref[...], lr_t, preferred_element_type=F32) + bup_ref[...]
    la_t = (jnp.minimum(z_t, 0.0) - jnp.log(1.0 + jnp.exp(-jnp.abs(z_t)))) * (1.0 / GATE_TAU)
    la_hi, la_lo = _split_bf16(la_t)
    bcum = (jnp.dot(la_hi, ubd_ref[...], preferred_element_type=F32)
            + jnp.dot(la_lo, ubd_ref[...], preferred_element_type=F32))
    bend = (jnp.dot(la_hi, ebd_ref[...], preferred_element_type=F32)
            + jnp.dot(la_lo, ebd_ref[...], preferred_element_type=F32))
    kd_t = (k_t * jnp.exp(bend - bcum)).astype(BF16)
    dec_t = jnp.exp(bend)

    st = [st_ref[hh] for hh in range(GLA_HEADS)]
    gn = gnorm_ref[...]
    for c in range(nch):
        lo, hi_ = c * CHUNK, (c + 1) * CHUNK
        kd_c = kd_t[:, lo:hi_]
        v_c = v[lo:hi_, :]
        for hh in range(GLA_HEADS):
            ks, ke = hh * GLA_DK, (hh + 1) * GLA_DK
            vs, ve = hh * GLA_DV, (hh + 1) * GLA_DV
            upd = jnp.dot(kd_c[ks:ke, :], v_c[:, vs:ve], preferred_element_type=F32)
            dec = jnp.broadcast_to(dec_t[ks:ke, lo:lo + 1], (GLA_DK, GLA_DV))
            st[hh] = dec * st[hh] + upd
            sbd_ref[ks:ke, vs:ve] = st[hh].astype(BF16)
        o_c = jnp.dot(q[lo:hi_, :], sbd_ref[...], preferred_element_type=F32)
        og_c = og[lo:hi_, :]
        for hh in range(GLA_HEADS):
            vs, ve = hh * GLA_DV, (hh + 1) * GLA_DV
            on = _rms(o_c[:, vs:ve], gn)
            gate = og_c[:, vs:ve]
            ycat_ref[lo:hi_, D_CONV + vs:D_CONV + ve] = (on * (gate * _sigmoid(gate))).astype(BF16)
    for hh in range(GLA_HEADS):
        st_ref[hh] = st[hh]

    y = jnp.dot(ycat_ref[...], wout_ref[...], preferred_element_type=F32)
    x1 = x + g1 * _rms(y, gpost_ref[...])
    x1_ref[0] = x1
    h2 = _rms(x1, gffn_ref[...]) * (1.0 + sc2) + sh2
    h2p_ref[0] = h2
    h2_hi, h2_lo = _split_bf16(h2)
    lg2 = lax.dot_general(wr_ref[...], h2_hi, NT_DIMS, preferred_element_type=F32)
    lg1 = lax.dot_general(wr_ref[0:N_EXPERTS, :], h2_lo, NT_DIMS, preferred_element_type=F32)
    logits_t = lg2[0:N_EXPERTS, :] + lg2[N_EXPERTS:, :] + lg1
    cls, wa, wb = _route(logits_t, br_ref[...])
    route_ref[0] = jnp.concatenate([cls, wa, wb, jnp.zeros((5, ts), F32)], axis=0)


def _mix(x, mod, lw, consts, ts):
    b, s, d = x.shape
    nst = s // ts
    full = lambda shape: pl.BlockSpec(shape, lambda bi, ji: (0,) * len(shape))
    in_specs = [
        pl.BlockSpec((1, ts, d), lambda bi, ji: (bi, ji, 0)),
        pl.BlockSpec((1, 6, d), lambda bi, ji: (bi, 0, 0)),
        full((1, d)), full((1, d)), full((1, d)),
        full(lw["wnat"].shape), full(lw["wt"].shape),
        full((CONV_WIDTH, D_CONV)), full((1, D_CONV)), full((1, D_CONV)), full((1, D_CONV)),
        full((D_GLA_K, GATE_RANK)), full((D_GLA_K, 1)), full((1, GLA_DV)),
        full((d, d)), full((2 * N_EXPERTS, d)), full((N_EXPERTS, 1)),
        full((ts, ts)), full((ts, ts)),
    ]
    out_specs = [
        pl.BlockSpec((1, ts, d), lambda bi, ji: (bi, ji, 0)),
        pl.BlockSpec((1, ts, d), lambda bi, ji: (bi, ji, 0)),
        pl.BlockSpec((1, 8, ts), lambda bi, ji: (bi * nst + ji, 0, 0)),
    ]
    out_shape = [
        jax.ShapeDtypeStruct((b, s, d), F32),
        jax.ShapeDtypeStruct((b, s, d), F32),
        jax.ShapeDtypeStruct((b * nst, 8, ts), F32),
    ]
    scratch = [
        pltpu.VMEM((CONV_PAD + ts, D_CONV), F32),
        pltpu.VMEM((SUBLANES - 1, CONV_PAD + ts - SUBLANES, D_CONV), F32),
        pltpu.VMEM((GLA_HEADS, GLA_DK, GLA_DV), F32),
        pltpu.VMEM((D_GLA_K, D_GLA_V), BF16),
        pltpu.VMEM((ts, d), BF16),
    ]
    return pl.pallas_call(
        functools.partial(_mix_kernel, ts=ts),
        grid=(b, nst),
        in_specs=in_specs,
        out_specs=out_specs,
        out_shape=out_shape,
        scratch_shapes=scratch,
        compiler_params=pltpu.CompilerParams(
            dimension_semantics=("arbitrary", "arbitrary"),
            vmem_limit_bytes=56 * 1024 * 1024),
        name="mix",
    )(x, mod, lw["gpre"], lw["gpost"], lw["gffn"], lw["wnat"], lw["wt"], lw["wdw"], lw["bdw"],
      lw["lng"], lw["lnb"], lw["wup"], lw["bup"], lw["gnorm"], lw["wout"], consts["wr"],
      consts["br"], consts["ubd"], consts["ebd"])


def _row_gather_start(idx_ref, base, src_hbm, dst_vmem, sem, n_rows):
    def body(it, carry):
        r0 = it * GATHER_UNROLL
        for u in range(GATHER_UNROLL):
            r = r0 + u
            tok = idx_ref[base + r]
            pltpu.make_async_copy(src_hbm.at[pl.ds(tok, 1), :], dst_vmem.at[pl.ds(r, 1), :], sem).start()
        return carry
    lax.fori_loop(0, n_rows // GATHER_UNROLL, body, 0)


def _row_gather_wait(src_hbm, dst_vmem, sem, n_rows):
    pltpu.make_async_copy(src_hbm.at[pl.ds(0, n_rows), :], dst_vmem, sem).wait()


def _moe_kernel(te1_ref, te2_ref, tvalid_ref, rowtok_ref,
                h2p_hbm, wab_ref, w1a_ref, w3a_ref, w2a_ref, w1b_ref, w3b_ref, w2b_ref, gpost_ref,
                z_ref, xg, sems, *, tm):
    i = pl.program_id(0)
    n = pl.num_programs(0)
    slot = lax.rem(i, 2)

    @pl.when(i == 0)
    def _():
        _row_gather_start(rowtok_ref, 0, h2p_hbm, xg.at[0], sems.at[0], tm)

    @pl.when(i + 1 < n)
    def _():
        _row_gather_start(rowtok_ref, (i + 1) * tm, h2p_hbm, xg.at[1 - slot], sems.at[1 - slot], tm)

    _row_gather_wait(h2p_hbm, xg.at[slot], sems.at[slot], tm)

    @pl.when(tvalid_ref[i] == 0)
    def _():
        z_ref[...] = jnp.zeros_like(z_ref)

    @pl.when(tvalid_ref[i] != 0)
    def _():
        h = xg[slot].astype(BF16)
        wcols = [jnp.transpose(jnp.broadcast_to(wab_ref[0, e:e + 1, :], (128, tm))) for e in range(2)]
        y = None
        for e, (w1, w3, w2) in enumerate(((w1a_ref, w3a_ref, w2a_ref), (w1b_ref, w3b_ref, w2b_ref))):
            g = jnp.dot(h, w1[0], preferred_element_type=F32)
            u = jnp.dot(h, w3[0], preferred_element_type=F32)
            he = ((g * _sigmoid(g)) * u).astype(BF16)
            ye = jnp.dot(he, w2[0], preferred_element_type=F32)
            ye = ye * jnp.concatenate([wcols[e]] * (ye.shape[1] // 128), axis=1)
            y = ye if y is None else y + ye
        z_ref[...] = _rms(y, gpost_ref[...])


def _moe(h2p, rowtok, wab, te1, te2, tvalid, lw, tm):
    t, d = h2p.shape
    nt = te1.shape[0]
    wspec_a = lambda shape: pl.BlockSpec((1,) + shape, lambda i, e1, e2, tv, rt: (e1[i], 0, 0))
    wspec_b = lambda shape: pl.BlockSpec((1,) + shape, lambda i, e1, e2, tv, rt: (e2[i], 0, 0))
    grid_spec = pltpu.PrefetchScalarGridSpec(
        num_scalar_prefetch=4,
        grid=(nt,),
        in_specs=[
            pl.BlockSpec(memory_space=pl.ANY),
            pl.BlockSpec((1, 2, tm), lambda i, e1, e2, tv, rt: (i, 0, 0)),
            wspec_a((d, D_EXPERT_PAD)), wspec_a((d, D_EXPERT_PAD)), wspec_a((D_EXPERT_PAD, d)),
            wspec_b((d, D_EXPERT_PAD)), wspec_b((d, D_EXPERT_PAD)), wspec_b((D_EXPERT_PAD, d)),
            pl.BlockSpec((1, d), lambda i, e1, e2, tv, rt: (0, 0)),
        ],
        out_specs=pl.BlockSpec((tm, d), lambda i, e1, e2, tv, rt: (i, 0)),
        scratch_shapes=[
            pltpu.VMEM((2, tm, d), F32),
            pltpu.SemaphoreType.DMA((2,)),
        ],
    )
    return pl.pallas_call(
        functools.partial(_moe_kernel, tm=tm),
        grid_spec=grid_spec,
        out_shape=jax.ShapeDtypeStruct((nt * tm, d), F32),
        compiler_params=pltpu.CompilerParams(
            dimension_semantics=("arbitrary",),
            vmem_limit_bytes=48 * 1024 * 1024),
        name="moe",
    )(te1, te2, tvalid, rowtok, h2p, wab, lw["w1"], lw["w3"], lw["w2"], lw["w1"], lw["w3"], lw["w2"],
      lw["gpostffn"])


def _combine_kernel(pos_ref, x1_ref, g2_ref, z_hbm, o_ref, zg, sems, *, tc):
    i = pl.program_id(0)
    n = pl.num_programs(0)
    slot = lax.rem(i, 2)

    @pl.when(i == 0)
    def _():
        _row_gather_start(pos_ref, 0, z_hbm, zg.at[0], sems.at[0], tc)

    @pl.when(i + 1 < n)
    def _():
        _row_gather_start(pos_ref, (i + 1) * tc, z_hbm, zg.at[1 - slot], sems.at[1 - slot], tc)

    _row_gather_wait(z_hbm, zg.at[slot], sems.at[slot], tc)
    o_ref[...] = x1_ref[...] + g2_ref[0] * zg[slot]


def _combine(x1, g2, z, pos, seq_len, tc):
    t, d = x1.shape
    per_seq = seq_len // tc
    grid_spec = pltpu.PrefetchScalarGridSpec(
        num_scalar_prefetch=1,
        grid=(t // tc,),
        in_specs=[
            pl.BlockSpec((tc, d), lambda i, pos: (i, 0)),
            pl.BlockSpec((1, 1, d), lambda i, pos: (i // per_seq, 0, 0)),
            pl.BlockSpec(memory_space=pl.ANY),
        ],
        out_specs=pl.BlockSpec((tc, d), lambda i, pos: (i, 0)),
        scratch_shapes=[
            pltpu.VMEM((2, tc, d), F32),
            pltpu.SemaphoreType.DMA((2,)),
        ],
    )
    return pl.pallas_call(
        functools.partial(_combine_kernel, tc=tc),
        grid_spec=grid_spec,
        out_shape=jax.ShapeDtypeStruct((t, d), F32),
        compiler_params=pltpu.CompilerParams(
            dimension_semantics=("arbitrary",),
            vmem_limit_bytes=32 * 1024 * 1024),
        name="combine",
    )(pos, x1, g2, z)


def _class_tables():
    e1, e2 = [], []
    for g in range(N_GROUPS):
        for a in range(EXPERTS_PER_GROUP):
            for b in range(a + 1, EXPERTS_PER_GROUP):
                e1.append(g * EXPERTS_PER_GROUP + a)
                e2.append(g * EXPERTS_PER_GROUP + b)
    return jnp.asarray(e1, jnp.int32), jnp.asarray(e2, jnp.int32)


def _plan(cls, wa, wb, tm):
    t = cls.shape[0]
    nt = t // tm + N_CLASSES
    onehot = (cls[:, None] == jnp.arange(N_CLASSES, dtype=jnp.int32)[None, :]).astype(jnp.int32)
    csum = jnp.cumsum(onehot, axis=0)
    counts = csum[-1]
    rank = jnp.take_along_axis(csum, cls[:, None], axis=1)[:, 0] - 1
    tiles = (counts + tm - 1) // tm
    tile_end = jnp.cumsum(tiles)
    tile_start = tile_end - tiles
    pos = tile_start[cls] * tm + rank
    tile_ids = jnp.arange(nt, dtype=jnp.int32)
    tile_cls = jnp.sum((tile_end[None, :] <= tile_ids[:, None]).astype(jnp.int32), axis=1)
    tvalid = (tile_ids < tile_end[-1]).astype(jnp.int32)
    last_cls = jnp.max(jnp.where(counts > 0, jnp.arange(N_CLASSES, dtype=jnp.int32), 0))
    tile_cls = jnp.where(tvalid == 1, jnp.minimum(tile_cls, N_CLASSES - 1), last_cls)
    ce1, ce2 = _class_tables()
    rowtok = jnp.zeros((nt * tm,), jnp.int32).at[pos].set(jnp.arange(t, dtype=jnp.int32))
    wab = jnp.zeros((2, nt * tm), F32).at[:, pos].set(jnp.stack([wa, wb]))
    wab = wab.reshape(2, nt, tm).transpose(1, 0, 2)
    return pos.astype(jnp.int32), rowtok, wab, ce1[tile_cls], ce2[tile_cls], tvalid


def _layer_weights(l, g_pre_mix, g_post_mix, g_pre_ffn, g_post_ffn, w_in, w_dw, b_dw, conv_ln_g,
                   conv_ln_b, w_gate_up, b_gate, gla_norm_g, w_out, w1, w3, w2):
    d = w_in.shape[1]
    wi = w_in[l]
    o_q = 2 * D_CONV
    o_k = o_q + D_GLA_K
    o_v = o_k + D_GLA_K
    o_g = o_v + D_GLA_V
    o_lr = o_g + D_GLA_V
    wnat = jnp.concatenate([wi[:, 0:o_q], wi[:, o_q:o_k], wi[:, o_v:o_g], wi[:, o_g:o_lr]], axis=1)
    wt = jnp.concatenate([wi[:, o_k:o_v], wi[:, o_lr:]], axis=1).T
    pad_n = ((0, 0), (0, 0), (0, D_EXPERT_PAD - D_EXPERT))
    pad_k = ((0, 0), (0, D_EXPERT_PAD - D_EXPERT), (0, 0))
    return dict(
        gpre=g_pre_mix[l].reshape(1, d), gpost=g_post_mix[l].reshape(1, d),
        gffn=g_pre_ffn[l].reshape(1, d), gpostffn=g_post_ffn[l].reshape(1, d),
        wnat=wnat.astype(BF16), wt=wt.astype(BF16),
        wdw=w_dw[l], bdw=b_dw[l].reshape(1, D_CONV),
        lng=conv_ln_g[l].reshape(1, D_CONV), lnb=conv_ln_b[l].reshape(1, D_CONV),
        wup=w_gate_up[l].T.astype(BF16), bup=b_gate[l].reshape(D_GLA_K, 1),
        gnorm=gla_norm_g[l].reshape(1, GLA_DV),
        wout=w_out[l].astype(BF16),
        w1=jnp.pad(w1[l].astype(BF16), pad_n), w3=jnp.pad(w3[l].astype(BF16), pad_n),
        w2=jnp.pad(w2[l].astype(BF16), pad_k),
    )


def kernel(x, c, w_ada, b_ada, g_pre_mix, g_post_mix, g_pre_ffn, g_post_ffn, w_in, w_dw, b_dw,
           conv_ln_g, conv_ln_b, w_gate_up, b_gate, gla_norm_g, w_out, w_router, b_router, w1, w3, w2):
    b, s, d = x.shape
    depth = w_ada.shape[0]
    t = b * s
    ts = min(SEQ_TILE, s)
    tm = min(MOE_TILE, t)
    tc = min(COMBINE_TILE, s)
    assert s % ts == 0 and ts % CHUNK == 0 and t % tm == 0 and s % tc == 0

    mod = _modulation(c, w_ada, b_ada).reshape(depth, b, 6, d)

    wr_hi, wr_lo = _split_bf16(w_router.T)
    tok = jnp.arange(ts, dtype=jnp.int32)
    same = (tok[:, None] // CHUNK) == (tok[None, :] // CHUNK)
    consts = dict(
        wr=jnp.concatenate([wr_hi, wr_lo], axis=0),
        br=b_router.reshape(N_EXPERTS, 1),
        ubd=(same & (tok[:, None] <= tok[None, :])).astype(BF16),
        ebd=same.astype(BF16),
    )

    for l in range(depth):
        lw = _layer_weights(l, g_pre_mix, g_post_mix, g_pre_ffn, g_post_ffn, w_in, w_dw, b_dw,
                            conv_ln_g, conv_ln_b, w_gate_up, b_gate, gla_norm_g, w_out, w1, w3, w2)
        x1, h2p, route = _mix(x, mod[l], lw, consts, ts)
        route = route.reshape(b, s // ts, 8, ts)
        cls = route[:, :, 0, :].reshape(t).astype(jnp.int32)
        wa = route[:, :, 1, :].reshape(t)
        wb = route[:, :, 2, :].reshape(t)
        pos, rowtok, wab, te1, te2, tvalid = _plan(cls, wa, wb, tm)
        z = _moe(h2p.reshape(t, d), rowtok, wab, te1, te2, tvalid, lw, tm)
        g2 = mod[l][:, 5:6, :]
        x = _combine(x1.reshape(t, d), g2, z, pos, s, tc).reshape(b, s, d)
    return x
```

```python
import functools

import jax
import jax.numpy as jnp
from jax import lax
from jax.experimental import pallas as pl
from jax.experimental.pallas import tpu as pltpu

CHUNK = 64
CONV_WIDTH = 31
D_CONV = 512
GLA_HEADS = 4
GLA_DV = 128
GLA_DK = 64
D_GLA_K = GLA_HEADS * GLA_DK
D_GLA_V = GLA_HEADS * GLA_DV
GATE_RANK = 16
GATE_TAU = 16.0
N_EXPERTS = 16
N_GROUPS = 4
EXPERTS_PER_GROUP = 4
PAIRS_PER_GROUP = 6
N_CLASSES = N_GROUPS * PAIRS_PER_GROUP
D_EXPERT = 704
D_EXPERT_PAD = 768
EPS = 1e-6

SUBLANES = 8
LANES = 128
MXU_TILE = 256
CONV_PAD = 32
CONV_ROWS = 32
CONV_LANES = 256
SEQ_TILE = 512
MOE_TILE = 256
COMBINE_TILE = 256

F32 = jnp.float32
BF16 = jnp.bfloat16
NT_DIMS = (((1,), (1,)), ((), ()))


def _sigmoid(x):
    return 1.0 / (1.0 + jnp.exp(-x))


def _rms(x, g):
    ms = jnp.mean(x * x, axis=-1, keepdims=True)
    return x * lax.rsqrt(ms + EPS) * g


def _split_bf16(x):
    hi = x.astype(BF16)
    lo = (x - hi.astype(F32)).astype(BF16)
    return hi, lo


def _mod_kernel(c_ref, w_ref, b_ref, o_ref):
    c = c_ref[...]
    ca = c * _sigmoid(c)
    a_hi, a_lo = _split_bf16(ca)
    w_hi, w_lo = _split_bf16(w_ref[0])
    acc = jnp.dot(a_hi, w_hi, preferred_element_type=F32)
    acc += jnp.dot(a_lo, w_hi, preferred_element_type=F32)
    acc += jnp.dot(a_hi, w_lo, preferred_element_type=F32)
    o_ref[0] = acc + b_ref[0]


def _modulation(c, w_ada, b_ada):
    depth, d, d6 = w_ada.shape
    b = c.shape[0]
    nblk = d6 // d
    return pl.pallas_call(
        _mod_kernel,
        grid=(depth, nblk),
        in_specs=[
            pl.BlockSpec((b, d), lambda l, n: (0, 0)),
            pl.BlockSpec((1, d, d), lambda l, n: (l, 0, n)),
            pl.BlockSpec((1, 1, d), lambda l, n: (l, 0, n)),
        ],
        out_specs=pl.BlockSpec((1, b, d), lambda l, n: (l, 0, n)),
        out_shape=jax.ShapeDtypeStruct((depth, b, d6), F32),
        compiler_params=pltpu.CompilerParams(
            dimension_semantics=("arbitrary", "arbitrary"),
            vmem_limit_bytes=32 * 1024 * 1024),
        name="adaln_mod",
    )(c, w_ada, b_ada.reshape(depth, 1, d6))


def _route(logits_t, br):
    m = jnp.max(logits_t, axis=0, keepdims=True)
    e = jnp.exp(logits_t - m)
    probs = e / jnp.sum(e, axis=0, keepdims=True)
    sel = probs + br
    one = jnp.ones_like(m)
    zero = jnp.zeros_like(m)
    picked, gscore = [], []
    for g in range(N_GROUPS):
        rows = [sel[g * EXPERTS_PER_GROUP + k:g * EXPERTS_PER_GROUP + k + 1, :]
                for k in range(EXPERTS_PER_GROUP)]
        score = zero
        for k in range(EXPERTS_PER_GROUP):
            rank = zero
            for k2 in range(EXPERTS_PER_GROUP):
                if k2 == k:
                    continue
                ahead = (rows[k2] >= rows[k]) if k2 < k else (rows[k2] > rows[k])
                rank = rank + jnp.where(ahead, one, zero)
            pk = jnp.where(rank < 2.0, one, zero)
            picked.append(pk)
            score = score + pk * rows[k]
        gscore.append(score)
    best = gscore[0]
    for g in range(1, N_GROUPS):
        best = jnp.maximum(best, gscore[g])
    taken = zero
    flag = [zero] * EXPERTS_PER_GROUP
    prob = [zero] * EXPERTS_PER_GROUP
    gbase = zero
    for g in range(N_GROUPS):
        isb = jnp.where(gscore[g] == best, one, zero) * (one - taken)
        taken = taken + isb
        gbase = gbase + isb * float(g * PAIRS_PER_GROUP)
        for k in range(EXPERTS_PER_GROUP):
            ei = g * EXPERTS_PER_GROUP + k
            flag[k] = flag[k] + isb * picked[ei]
            prob[k] = prob[k] + isb * probs[ei:ei + 1, :]
    f0, f1, f2, f3 = flag
    pair = f0 * (f2 * 1.0 + f3 * 2.0) + (one - f0) * (f1 * (f2 * 3.0 + f3 * 4.0) + (one - f1) * 5.0)
    cls = gbase + pair
    wa_raw = f0 * prob[0] + (one - f0) * (f1 * prob[1] + (one - f1) * prob[2])
    tot = f0 * prob[0] + f1 * prob[1] + f2 * prob[2] + f3 * prob[3]
    wa = wa_raw / tot
    wb = (tot - wa_raw) / tot
    return cls, wa, wb


def _mix_kernel(x_ref, mod_ref, gpre_ref, gpost_ref, gffn_ref, wnat_ref, wt_ref, wdw_ref, bdw_ref,
                lng_ref, lnb_ref, wup_ref, bup_ref, gnorm_ref, wout_ref, wr_ref, br_ref,
                ubd_ref, ebd_ref,
                x1_ref, h2p_ref, route_ref,
                ubuf, ush, cbuf, st_ref, sbd_ref, ycat_ref, *, ts):
    j = pl.program_id(1)
    nch = ts // CHUNK

    @pl.when(j == 0)
    def _():
        ubuf[0:CONV_PAD, :] = jnp.zeros((CONV_PAD, D_CONV), F32)
        st_ref[...] = jnp.zeros_like(st_ref)
        sbd_ref[...] = jnp.zeros_like(sbd_ref)

    @pl.when(j > 0)
    def _():
        ubuf[0:CONV_PAD, :] = ubuf[ts:ts + CONV_PAD, :]

    x = x_ref[0]
    sh1 = mod_ref[0, 0:1, :]
    sc1 = mod_ref[0, 1:2, :]
    g1 = mod_ref[0, 2:3, :]
    sh2 = mod_ref[0, 3:4, :]
    sc2 = mod_ref[0, 4:5, :]

    h = _rms(x, gpre_ref[...]) * (1.0 + sc1) + sh1
    hb = h.astype(BF16)

    cvg = jnp.dot(hb, wnat_ref[:, 0:2 * D_CONV], preferred_element_type=F32)
    ubuf[CONV_PAD:CONV_PAD + ts, :] = cvg[:, 0:D_CONV] * _sigmoid(cvg[:, D_CONV:2 * D_CONV])

    for r in range(1, SUBLANES):
        ush[r - 1] = ubuf[r:r + ts + CONV_PAD - SUBLANES, :]

    groups = CONV_ROWS // SUBLANES
    first_off = CONV_PAD - (CONV_WIDTH - 1)

    def conv_block(rb, carry, c0):
        r0 = pl.multiple_of(rb * CONV_ROWS, CONV_ROWS)
        cs = slice(c0, c0 + CONV_LANES)
        acc = [jnp.broadcast_to(bdw_ref[:, cs], (SUBLANES, CONV_LANES)) for _ in range(groups)]
        for r in range(SUBLANES):
            taps = [tap for tap in range(CONV_WIDTH) if (first_off + tap) % SUBLANES == r]
            shifts = [(first_off + tap) // SUBLANES for tap in taps]
            src = ubuf if r == 0 else ush.at[r - 1]
            blk = {m: src[pl.ds(r0 + m * SUBLANES, SUBLANES), cs]
                   for m in range(min(shifts), max(shifts) + groups)}
            for tap, a in zip(taps, shifts):
                w8 = wdw_ref[tap * SUBLANES:(tap + 1) * SUBLANES, cs]
                for g in range(groups):
                    acc[g] = acc[g] + w8 * blk[a + g]
        cbuf[pl.ds(r0, CONV_ROWS), cs] = jnp.concatenate(acc, axis=0)
        return carry

    for c0 in range(0, D_CONV, CONV_LANES):
        lax.fori_loop(0, ts // CONV_ROWS, functools.partial(conv_block, c0=c0), 0)
    cv = cbuf[...]
    mu = jnp.mean(cv, axis=-1, keepdims=True)
    cen = cv - mu
    var = jnp.mean(cen * cen, axis=-1, keepdims=True)
    yn = cen * lax.rsqrt(var + EPS) * lng_ref[...] + lnb_ref[...]
    ycat_ref[:, 0:D_CONV] = (yn * _sigmoid(yn)).astype(BF16)

    qvo = jnp.dot(hb, wnat_ref[:, 2 * D_CONV:], preferred_element_type=F32)
    q = (qvo[:, 0:D_GLA_K] * (GLA_DK ** -0.5)).astype(BF16)
    v = qvo[:, D_GLA_K:D_GLA_K + D_GLA_V].astype(BF16)
    og = qvo[:, D_GLA_K + D_GLA_V:]
    tt = lax.dot_general(wt_ref[...], hb, NT_DIMS, preferred_element_type=F32)
    k_t = tt[0:D_GLA_K, :]
    lr_t = tt[D_GLA_K:, :].astype(BF16)
    z_t = jnp.dot(wup_ref[...], lr_t, preferred_element_type=F32) + bup_ref[...]
    la_t = (jnp.minimum(z_t, 0.0) - jnp.log(1.0 + jnp.exp(-jnp.abs(z_t)))) * (1.0 / GATE_TAU)
    la_b = la_t.astype(BF16)
    slab = ubd_ref.shape[0]
    bcum = jnp.concatenate(
        [jnp.dot(la_b[:, s0:s0 + slab], ubd_ref[...], preferred_element_type=F32)
         for s0 in range(0, ts, slab)], axis=1)
    bend = jnp.concatenate(
        [jnp.dot(la_b[:, s0:s0 + slab], ebd_ref[...], preferred_element_type=F32)
         for s0 in range(0, ts, slab)], axis=1)
    kd_t = (k_t * jnp.exp(bend - bcum)).astype(BF16)
    dec_t = jnp.exp(bend)

    st = [st_ref[hh] for hh in range(GLA_HEADS)]
    gn = gnorm_ref[...]
    for c in range(nch):
        lo, hi_ = c * CHUNK, (c + 1) * CHUNK
        kd_c = kd_t[:, lo:hi_]
        v_c = v[lo:hi_, :]
        for hh in range(GLA_HEADS):
            ks, ke = hh * GLA_DK, (hh + 1) * GLA_DK
            vs, ve = hh * GLA_DV, (hh + 1) * GLA_DV
            upd = jnp.dot(kd_c[ks:ke, :], v_c[:, vs:ve], preferred_element_type=F32)
            dec = jnp.broadcast_to(dec_t[ks:ke, lo:lo + 1], (GLA_DK, GLA_DV))
            st[hh] = dec * st[hh] + upd
            sbd_ref[ks:ke, vs:ve] = st[hh].astype(BF16)
        o_c = jnp.dot(q[lo:hi_, :], sbd_ref[...], preferred_element_type=F32)
        og_c = og[lo:hi_, :]
        for hh in range(GLA_HEADS):
            vs, ve = hh * GLA_DV, (hh + 1) * GLA_DV
            on = _rms(o_c[:, vs:ve], gn)
            gate = og_c[:, vs:ve]
            ycat_ref[lo:hi_, D_CONV + vs:D_CONV + ve] = (on * (gate * _sigmoid(gate))).astype(BF16)
    for hh in range(GLA_HEADS):
        st_ref[hh] = st[hh]

    y = jnp.dot(ycat_ref[...], wout_ref[...], preferred_element_type=F32)
    x1 = x + g1 * _rms(y, gpost_ref[...])
    x1_ref[0] = x1
    h2 = _rms(x1, gffn_ref[...]) * (1.0 + sc2) + sh2
    d = h2.shape[1]
    h2p_ref[0, :, 0:d] = h2
    h2_hi, h2_lo = _split_bf16(h2)
    lg2 = lax.dot_general(wr_ref[...], h2_hi, NT_DIMS, preferred_element_type=F32)
    lg1 = lax.dot_general(wr_ref[0:N_EXPERTS, :], h2_lo, NT_DIMS, preferred_element_type=F32)
    logits_t = lg2[0:N_EXPERTS, :] + lg2[N_EXPERTS:, :] + lg1
    cls, wa, wb = _route(logits_t, br_ref[...])
    route_ref[0] = jnp.concatenate([cls, jnp.zeros((SUBLANES - 1, ts), F32)], axis=0)
    h2p_ref[0, :, d:d + LANES] = jnp.transpose(jnp.broadcast_to(wa, (LANES, ts)))
    h2p_ref[0, :, d + LANES:] = jnp.transpose(jnp.broadcast_to(wb, (LANES, ts)))


def _mix(x, mod, lw, consts, ts):
    b, s, d = x.shape
    nst = s // ts
    full = lambda shape: pl.BlockSpec(shape, lambda bi, ji: (0,) * len(shape))
    in_specs = [
        pl.BlockSpec((1, ts, d), lambda bi, ji: (bi, ji, 0)),
        pl.BlockSpec((1, 6, d), lambda bi, ji: (bi, 0, 0)),
        full((1, d)), full((1, d)), full((1, d)),
        full(lw["wnat"].shape), full(lw["wt"].shape),
        full((CONV_WIDTH * SUBLANES, D_CONV)), full((1, D_CONV)), full((1, D_CONV)), full((1, D_CONV)),
        full((D_GLA_K, GATE_RANK)), full((D_GLA_K, 1)), full((1, GLA_DV)),
        full((d, d)), full((2 * N_EXPERTS, d)), full((N_EXPERTS, 1)),
        full(consts["ubd"].shape), full(consts["ebd"].shape),
    ]
    out_specs = [
        pl.BlockSpec((1, ts, d), lambda bi, ji: (bi, ji, 0)),
        pl.BlockSpec((1, ts, d + 2 * LANES), lambda bi, ji: (bi, ji, 0)),
        pl.BlockSpec((1, 8, ts), lambda bi, ji: (bi * nst + ji, 0, 0)),
    ]
    out_shape = [
        jax.ShapeDtypeStruct((b, s, d), F32),
        jax.ShapeDtypeStruct((b, s, d + 2 * LANES), F32),
        jax.ShapeDtypeStruct((b * nst, 8, ts), F32),
    ]
    scratch = [
        pltpu.VMEM((CONV_PAD + ts, D_CONV), F32),
        pltpu.VMEM((SUBLANES - 1, CONV_PAD + ts - SUBLANES, D_CONV), F32),
        pltpu.VMEM((ts, D_CONV), F32),
        pltpu.VMEM((GLA_HEADS, GLA_DK, GLA_DV), F32),
        pltpu.VMEM((D_GLA_K, D_GLA_V), BF16),
        pltpu.VMEM((ts, d), BF16),
    ]
    return pl.pallas_call(
        functools.partial(_mix_kernel, ts=ts),
        grid=(b, nst),
        in_specs=in_specs,
        out_specs=out_specs,
        out_shape=out_shape,
        scratch_shapes=scratch,
        compiler_params=pltpu.CompilerParams(
            dimension_semantics=("arbitrary", "arbitrary"),
            vmem_limit_bytes=56 * 1024 * 1024),
        name="mix",
    )(x, mod, lw["gpre"], lw["gpost"], lw["gffn"], lw["wnat"], lw["wt"], lw["wdw"], lw["bdw"],
      lw["lng"], lw["lnb"], lw["wup"], lw["bup"], lw["gnorm"], lw["wout"], consts["wr"],
      consts["br"], consts["ubd"], consts["ebd"])


def _row_copy(idx_ref, base, r, src_hbm, dst_vmem, sem):
    tok = idx_ref[base + r]
    return pltpu.make_async_copy(src_hbm.at[pl.ds(tok, 1), :], dst_vmem.at[pl.ds(r, 1), :], sem)


def _row_gather_start(idx_ref, base, src_hbm, dst_vmem, sem, n_rows, unrolled):
    if unrolled:
        for r in range(n_rows):
            _row_copy(idx_ref, base, r, src_hbm, dst_vmem, sem).start()
    else:
        def body(r, carry):
            _row_copy(idx_ref, base, r, src_hbm, dst_vmem, sem).start()
            return carry
        lax.fori_loop(0, n_rows, body, 0)


def _row_gather_wait(src_hbm, dst_vmem, sem, n_rows):
    pltpu.make_async_copy(src_hbm.at[pl.ds(0, n_rows), :], dst_vmem, sem).wait()


def _moe_kernel(te1_ref, te2_ref, tvalid_ref, rowtok_ref,
                h2p_hbm, w1a_ref, w3a_ref, w2a_ref, w1b_ref, w3b_ref, w2b_ref, gpost_ref,
                z_ref, xg, sems, *, tm):
    i = pl.program_id(0)
    n = pl.num_programs(0)
    slot = lax.rem(i, 2)
    d = z_ref.shape[1]

    @pl.when(i == 0)
    def _():
        _row_gather_start(rowtok_ref, 0, h2p_hbm, xg.at[0], sems.at[0], tm, unrolled=False)

    @pl.when(i + 1 < n)
    def _():
        _row_gather_start(rowtok_ref, (i + 1) * tm, h2p_hbm, xg.at[1 - slot], sems.at[1 - slot], tm,
                          unrolled=True)

    _row_gather_wait(h2p_hbm, xg.at[slot], sems.at[slot], tm)

    @pl.when(tvalid_ref[i] == 0)
    def _():
        z_ref[...] = jnp.zeros_like(z_ref)

    @pl.when(tvalid_ref[i] != 0)
    def _():
        h = xg[slot, :, 0:d].astype(BF16)
        y = None
        for e, (w1, w3, w2) in enumerate(((w1a_ref, w3a_ref, w2a_ref), (w1b_ref, w3b_ref, w2b_ref))):
            g = jnp.dot(h, w1[0, 0], preferred_element_type=F32)
            u = jnp.dot(h, w3[0, 0], preferred_element_type=F32)
            he = ((g * _sigmoid(g)) * u).astype(BF16)
            ye = jnp.dot(he, w2[0, 0], preferred_element_type=F32)
            wcol = xg[slot, :, d + e * LANES:d + (e + 1) * LANES]
            ye = ye * jnp.concatenate([wcol] * (d // LANES), axis=1)
            y = ye if y is None else y + ye
        z_ref[...] = _rms(y, gpost_ref[...])


def _moe(h2p, rowtok, te1, te2, tvalid, w1, w3, w2, gpost, layer, tm):
    t, dx = h2p.shape
    d = dx - 2 * LANES
    nt = te1.shape[0]
    wspec_a = lambda shape: pl.BlockSpec((1, 1) + shape, lambda i, e1, e2, tv, rt: (layer, e1[i], 0, 0))
    wspec_b = lambda shape: pl.BlockSpec((1, 1) + shape, lambda i, e1, e2, tv, rt: (layer, e2[i], 0, 0))
    grid_spec = pltpu.PrefetchScalarGridSpec(
        num_scalar_prefetch=4,
        grid=(nt,),
        in_specs=[
            pl.BlockSpec(memory_space=pl.ANY),
            wspec_a((d, D_EXPERT_PAD)), wspec_a((d, D_EXPERT_PAD)), wspec_a((D_EXPERT_PAD, d)),
            wspec_b((d, D_EXPERT_PAD)), wspec_b((d, D_EXPERT_PAD)), wspec_b((D_EXPERT_PAD, d)),
            pl.BlockSpec((1, d), lambda i, e1, e2, tv, rt: (0, 0)),
        ],
        out_specs=pl.BlockSpec((tm, d), lambda i, e1, e2, tv, rt: (i, 0)),
        scratch_shapes=[
            pltpu.VMEM((2, tm, dx), F32),
            pltpu.SemaphoreType.DMA((2,)),
        ],
    )
    return pl.pallas_call(
        functools.partial(_moe_kernel, tm=tm),
        grid_spec=grid_spec,
        out_shape=jax.ShapeDtypeStruct((nt * tm, d), F32),
        compiler_params=pltpu.CompilerParams(
            dimension_semantics=("arbitrary",),
            vmem_limit_bytes=48 * 1024 * 1024),
        name="moe",
    )(te1, te2, tvalid, rowtok, h2p, w1, w3, w2, w1, w3, w2, gpost)


def _combine_kernel(pos_ref, x1_ref, g2_ref, z_hbm, o_ref, zg, sems, *, tc):
    i = pl.program_id(0)
    n = pl.num_programs(0)
    slot = lax.rem(i, 2)

    @pl.when(i == 0)
    def _():
        _row_gather_start(pos_ref, 0, z_hbm, zg.at[0], sems.at[0], tc, unrolled=False)

    @pl.when(i + 1 < n)
    def _():
        _row_gather_start(pos_ref, (i + 1) * tc, z_hbm, zg.at[1 - slot], sems.at[1 - slot], tc,
                          unrolled=True)

    _row_gather_wait(z_hbm, zg.at[slot], sems.at[slot], tc)
    o_ref[...] = x1_ref[...] + g2_ref[0] * zg[slot]


def _combine(x1, g2, z, pos, seq_len, tc):
    t, d = x1.shape
    per_seq = seq_len // tc
    grid_spec = pltpu.PrefetchScalarGridSpec(
        num_scalar_prefetch=1,
        grid=(t // tc,),
        in_specs=[
            pl.BlockSpec((tc, d), lambda i, pos: (i, 0)),
            pl.BlockSpec((1, 1, d), lambda i, pos: (i // per_seq, 0, 0)),
            pl.BlockSpec(memory_space=pl.ANY),
        ],
        out_specs=pl.BlockSpec((tc, d), lambda i, pos: (i, 0)),
        scratch_shapes=[
            pltpu.VMEM((2, tc, d), F32),
            pltpu.SemaphoreType.DMA((2,)),
        ],
    )
    return pl.pallas_call(
        functools.partial(_combine_kernel, tc=tc),
        grid_spec=grid_spec,
        out_shape=jax.ShapeDtypeStruct((t, d), F32),
        compiler_params=pltpu.CompilerParams(
            dimension_semantics=("arbitrary",),
            vmem_limit_bytes=32 * 1024 * 1024),
        name="combine",
    )(pos, x1, g2, z)


def _class_tables():
    e1, e2 = [], []
    for g in range(N_GROUPS):
        for a in range(EXPERTS_PER_GROUP):
            for b in range(a + 1, EXPERTS_PER_GROUP):
                e1.append(g * EXPERTS_PER_GROUP + a)
                e2.append(g * EXPERTS_PER_GROUP + b)
    return jnp.asarray(e1, jnp.int32), jnp.asarray(e2, jnp.int32)


def _plan(cls, tm):
    t = cls.shape[0]
    nt = t // tm + N_CLASSES
    onehot = (cls[:, None] == jnp.arange(N_CLASSES, dtype=jnp.int32)[None, :]).astype(jnp.int32)
    csum = jnp.cumsum(onehot, axis=0)
    counts = csum[-1]
    rank = jnp.take_along_axis(csum, cls[:, None], axis=1)[:, 0] - 1
    tiles = (counts + tm - 1) // tm
    tile_end = jnp.cumsum(tiles)
    tile_start = tile_end - tiles
    pos = tile_start[cls] * tm + rank
    tile_ids = jnp.arange(nt, dtype=jnp.int32)
    tile_cls = jnp.sum((tile_end[None, :] <= tile_ids[:, None]).astype(jnp.int32), axis=1)
    tvalid = (tile_ids < tile_end[-1]).astype(jnp.int32)
    last_cls = jnp.max(jnp.where(counts > 0, jnp.arange(N_CLASSES, dtype=jnp.int32), 0))
    tile_cls = jnp.where(tvalid == 1, jnp.minimum(tile_cls, N_CLASSES - 1), last_cls)
    ce1, ce2 = _class_tables()
    rowtok = jnp.zeros((nt * tm,), jnp.int32).at[pos].set(jnp.arange(t, dtype=jnp.int32))
    return pos.astype(jnp.int32), rowtok, ce1[tile_cls], ce2[tile_cls], tvalid


def _layer_weights(l, g_pre_mix, g_post_mix, g_pre_ffn, g_post_ffn, w_in, w_dw, b_dw, conv_ln_g,
                   conv_ln_b, w_gate_up, b_gate, gla_norm_g, w_out):
    d = w_in.shape[1]
    wi = w_in[l]
    o_q = 2 * D_CONV
    o_k = o_q + D_GLA_K
    o_v = o_k + D_GLA_K
    o_g = o_v + D_GLA_V
    o_lr = o_g + D_GLA_V
    wnat = jnp.concatenate([wi[:, 0:o_q], wi[:, o_q:o_k], wi[:, o_v:o_g], wi[:, o_g:o_lr]], axis=1)
    wt = jnp.concatenate([wi[:, o_k:o_v], wi[:, o_lr:]], axis=1).T
    return dict(
        gpre=g_pre_mix[l].reshape(1, d), gpost=g_post_mix[l].reshape(1, d),
        gffn=g_pre_ffn[l].reshape(1, d), gpostffn=g_post_ffn[l].reshape(1, d),
        wnat=wnat.astype(BF16), wt=wt.astype(BF16),
        wdw=jnp.repeat(w_dw[l], SUBLANES, axis=0), bdw=b_dw[l].reshape(1, D_CONV),
        lng=conv_ln_g[l].reshape(1, D_CONV), lnb=conv_ln_b[l].reshape(1, D_CONV),
        wup=w_gate_up[l].T.astype(BF16), bup=b_gate[l].reshape(D_GLA_K, 1),
        gnorm=gla_norm_g[l].reshape(1, GLA_DV),
        wout=w_out[l].astype(BF16),
    )


def kernel(x, c, w_ada, b_ada, g_pre_mix, g_post_mix, g_pre_ffn, g_post_ffn, w_in, w_dw, b_dw,
           conv_ln_g, conv_ln_b, w_gate_up, b_gate, gla_norm_g, w_out, w_router, b_router, w1, w3, w2):
    b, s, d = x.shape
    depth = w_ada.shape[0]
    t = b * s
    ts = min(SEQ_TILE, s)
    tm = min(MOE_TILE, t)
    tc = min(COMBINE_TILE, s)
    assert s % ts == 0 and ts % CHUNK == 0 and t % tm == 0 and s % tc == 0

    mod = _modulation(c, w_ada, b_ada).reshape(depth, b, 6, d)

    wr_hi, wr_lo = _split_bf16(w_router.T)
    tok = jnp.arange(min(ts, MXU_TILE), dtype=jnp.int32)
    same = (tok[:, None] // CHUNK) == (tok[None, :] // CHUNK)
    consts = dict(
        wr=jnp.concatenate([wr_hi, wr_lo], axis=0),
        br=b_router.reshape(N_EXPERTS, 1),
        ubd=(same & (tok[:, None] <= tok[None, :])).astype(BF16),
        ebd=same.astype(BF16),
    )

    pad_n = ((0, 0), (0, 0), (0, 0), (0, D_EXPERT_PAD - D_EXPERT))
    pad_k = ((0, 0), (0, 0), (0, D_EXPERT_PAD - D_EXPERT), (0, 0))
    w1b = jnp.pad(w1.astype(BF16), pad_n)
    w3b = jnp.pad(w3.astype(BF16), pad_n)
    w2b = jnp.pad(w2.astype(BF16), pad_k)

    for l in range(depth):
        lw = _layer_weights(l, g_pre_mix, g_post_mix, g_pre_ffn, g_post_ffn, w_in, w_dw, b_dw,
                            conv_ln_g, conv_ln_b, w_gate_up, b_gate, gla_norm_g, w_out)
        x1, h2p, route = _mix(x, mod[l], lw, consts, ts)
        cls = route[:, 0, :].reshape(t).astype(jnp.int32)
        pos, rowtok, te1, te2, tvalid = _plan(cls, tm)
        z = _moe(h2p.reshape(t, d + 2 * LANES), rowtok, te1, te2, tvalid, w1b, w3b, w2b,
                 lw["gpostffn"], l, tm)
        g2 = mod[l][:, 5:6, :]
        x = _combine(x1.reshape(t, d), g2, z, pos, s, tc).reshape(b, s, d)
    return x
```

```python
import functools

import jax
import jax.numpy as jnp
from jax import lax
from jax.experimental import pallas as pl
from jax.experimental.pallas import tpu as pltpu

CHUNK = 64
CONV_WIDTH = 31
D_CONV = 512
GLA_HEADS = 4
GLA_DV = 128
GLA_DK = 64
D_GLA_K = GLA_HEADS * GLA_DK
D_GLA_V = GLA_HEADS * GLA_DV
GATE_RANK = 16
GATE_TAU = 16.0
N_EXPERTS = 16
N_GROUPS = 4
EXPERTS_PER_GROUP = 4
PAIRS_PER_GROUP = 6
N_CLASSES = N_GROUPS * PAIRS_PER_GROUP
D_EXPERT = 704
D_EXPERT_PAD = 768
EPS = 1e-6

SUBLANES = 8
LANES = 128
MXU_TILE = 256
CONV_PAD = 32
CONV_ROWS = 32
CONV_LANES = 256
SEQ_TILE = 512
MOE_TILE = 256
COMBINE_TILE = 256
INDEX_UNROLL = 16

F32 = jnp.float32
BF16 = jnp.bfloat16
NT_DIMS = (((1,), (1,)), ((), ()))


def _sigmoid(x):
    return 1.0 / (1.0 + jnp.exp(-x))


def _rms(x, g):
    ms = jnp.mean(x * x, axis=-1, keepdims=True)
    return x * lax.rsqrt(ms + EPS) * g


def _split_bf16(x):
    hi = x.astype(BF16)
    lo = (x - hi.astype(F32)).astype(BF16)
    return hi, lo


def _mod_kernel(c_ref, w_ref, b_ref, o_ref):
    c = c_ref[...]
    ca = c * _sigmoid(c)
    a_hi, a_lo = _split_bf16(ca)
    w_hi, w_lo = _split_bf16(w_ref[0])
    acc = jnp.dot(a_hi, w_hi, preferred_element_type=F32)
    acc += jnp.dot(a_lo, w_hi, preferred_element_type=F32)
    acc += jnp.dot(a_hi, w_lo, preferred_element_type=F32)
    o_ref[0] = acc + b_ref[0]


def _modulation(c, w_ada, b_ada):
    depth, d, d6 = w_ada.shape
    b = c.shape[0]
    nblk = d6 // d
    return pl.pallas_call(
        _mod_kernel,
        grid=(depth, nblk),
        in_specs=[
            pl.BlockSpec((b, d), lambda l, n: (0, 0)),
            pl.BlockSpec((1, d, d), lambda l, n: (l, 0, n)),
            pl.BlockSpec((1, 1, d), lambda l, n: (l, 0, n)),
        ],
        out_specs=pl.BlockSpec((1, b, d), lambda l, n: (l, 0, n)),
        out_shape=jax.ShapeDtypeStruct((depth, b, d6), F32),
        compiler_params=pltpu.CompilerParams(
            dimension_semantics=("arbitrary", "arbitrary"),
            vmem_limit_bytes=32 * 1024 * 1024),
        name="adaln_mod",
    )(c, w_ada, b_ada.reshape(depth, 1, d6))


def _cast_pad_kernel(w_ref, o_ref):
    rows, cols = w_ref.shape[2:]
    prow, pcol = o_ref.shape[2:]
    o_ref[0, 0, 0:rows, 0:cols] = w_ref[0, 0].astype(BF16)
    if pcol > cols:
        o_ref[0, 0, :, cols:] = jnp.zeros((prow, pcol - cols), BF16)
    if prow > rows:
        o_ref[0, 0, rows:, 0:cols] = jnp.zeros((prow - rows, cols), BF16)


def _cast_pad(w, prow, pcol):
    depth, ne, rows, cols = w.shape
    return pl.pallas_call(
        _cast_pad_kernel,
        grid=(depth, ne),
        in_specs=[pl.BlockSpec((1, 1, rows, cols), lambda l, e: (l, e, 0, 0))],
        out_specs=pl.BlockSpec((1, 1, prow, pcol), lambda l, e: (l, e, 0, 0)),
        out_shape=jax.ShapeDtypeStruct((depth, ne, prow, pcol), BF16),
        compiler_params=pltpu.CompilerParams(
            dimension_semantics=("arbitrary", "arbitrary"),
            vmem_limit_bytes=32 * 1024 * 1024),
        name="cast_pad",
    )(w)


def _route(logits_t, br):
    m = jnp.max(logits_t, axis=0, keepdims=True)
    e = jnp.exp(logits_t - m)
    probs = e / jnp.sum(e, axis=0, keepdims=True)
    sel = probs + br
    one = jnp.ones_like(m)
    zero = jnp.zeros_like(m)
    picked, gscore = [], []
    for g in range(N_GROUPS):
        rows = [sel[g * EXPERTS_PER_GROUP + k:g * EXPERTS_PER_GROUP + k + 1, :]
                for k in range(EXPERTS_PER_GROUP)]
        score = zero
        for k in range(EXPERTS_PER_GROUP):
            rank = zero
            for k2 in range(EXPERTS_PER_GROUP):
                if k2 == k:
                    continue
                ahead = (rows[k2] >= rows[k]) if k2 < k else (rows[k2] > rows[k])
                rank = rank + jnp.where(ahead, one, zero)
            pk = jnp.where(rank < 2.0, one, zero)
            picked.append(pk)
            score = score + pk * rows[k]
        gscore.append(score)
    best = gscore[0]
    for g in range(1, N_GROUPS):
        best = jnp.maximum(best, gscore[g])
    taken = zero
    flag = [zero] * EXPERTS_PER_GROUP
    prob = [zero] * EXPERTS_PER_GROUP
    gbase = zero
    for g in range(N_GROUPS):
        isb = jnp.where(gscore[g] == best, one, zero) * (one - taken)
        taken = taken + isb
        gbase = gbase + isb * float(g * PAIRS_PER_GROUP)
        for k in range(EXPERTS_PER_GROUP):
            ei = g * EXPERTS_PER_GROUP + k
            flag[k] = flag[k] + isb * picked[ei]
            prob[k] = prob[k] + isb * probs[ei:ei + 1, :]
    f0, f1, f2, f3 = flag
    pair = f0 * (f2 * 1.0 + f3 * 2.0) + (one - f0) * (f1 * (f2 * 3.0 + f3 * 4.0) + (one - f1) * 5.0)
    cls = gbase + pair
    wa_raw = f0 * prob[0] + (one - f0) * (f1 * prob[1] + (one - f1) * prob[2])
    tot = f0 * prob[0] + f1 * prob[1] + f2 * prob[2] + f3 * prob[3]
    wa = wa_raw / tot
    wb = (tot - wa_raw) / tot
    return cls, wa, wb


def _mix_kernel(x_ref, mod_ref, gpre_ref, gpost_ref, gffn_ref, wnat_ref, wt_ref, wdw_ref, bdw_ref,
                lng_ref, lnb_ref, wup_ref, bup_ref, gnorm_ref, wout_ref, wr_ref, br_ref,
                ubd_ref, ebd_ref,
                x1_ref, h2p_ref, route_ref,
                ubuf, ush, cbuf, st_ref, sbd_ref, ycat_ref, *, ts):
    j = pl.program_id(1)
    nch = ts // CHUNK

    @pl.when(j == 0)
    def _():
        ubuf[0:CONV_PAD, :] = jnp.zeros((CONV_PAD, D_CONV), F32)
        st_ref[...] = jnp.zeros_like(st_ref)
        sbd_ref[...] = jnp.zeros_like(sbd_ref)

    @pl.when(j > 0)
    def _():
        ubuf[0:CONV_PAD, :] = ubuf[ts:ts + CONV_PAD, :]

    x = x_ref[0]
    sh1 = mod_ref[0, 0:1, :]
    sc1 = mod_ref[0, 1:2, :]
    g1 = mod_ref[0, 2:3, :]
    sh2 = mod_ref[0, 3:4, :]
    sc2 = mod_ref[0, 4:5, :]

    h = _rms(x, gpre_ref[...]) * (1.0 + sc1) + sh1
    hb = h.astype(BF16)

    cvg = jnp.dot(hb, wnat_ref[:, 0:2 * D_CONV], preferred_element_type=F32)
    ubuf[CONV_PAD:CONV_PAD + ts, :] = cvg[:, 0:D_CONV] * _sigmoid(cvg[:, D_CONV:2 * D_CONV])

    for r in range(1, SUBLANES):
        ush[r - 1] = ubuf[r:r + ts + CONV_PAD - SUBLANES, :]

    groups = CONV_ROWS // SUBLANES
    first_off = CONV_PAD - (CONV_WIDTH - 1)

    def conv_block(rb, carry, c0):
        r0 = rb * CONV_ROWS
        cs = slice(c0, c0 + CONV_LANES)
        acc = [jnp.broadcast_to(bdw_ref[:, cs], (SUBLANES, CONV_LANES)) for _ in range(groups)]
        for r in range(SUBLANES):
            taps = [tap for tap in range(CONV_WIDTH) if (first_off + tap) % SUBLANES == r]
            shifts = [(first_off + tap) // SUBLANES for tap in taps]
            src = ubuf if r == 0 else ush.at[r - 1]
            blk = {m: src[pl.ds(r0 + m * SUBLANES, SUBLANES), cs]
                   for m in range(min(shifts), max(shifts) + groups)}
            for tap, a in zip(taps, shifts):
                w8 = wdw_ref[tap * SUBLANES:(tap + 1) * SUBLANES, cs]
                for g in range(groups):
                    acc[g] = acc[g] + w8 * blk[a + g]
        cbuf[pl.ds(r0, CONV_ROWS), cs] = jnp.concatenate(acc, axis=0)
        return carry

    for c0 in range(0, D_CONV, CONV_LANES):
        for rb in range(ts // CONV_ROWS):
            conv_block(rb, 0, c0)
    cv = cbuf[...]
    mu = jnp.mean(cv, axis=-1, keepdims=True)
    cen = cv - mu
    var = jnp.mean(cen * cen, axis=-1, keepdims=True)
    yn = cen * lax.rsqrt(var + EPS) * lng_ref[...] + lnb_ref[...]
    ycat_ref[:, 0:D_CONV] = (yn * _sigmoid(yn)).astype(BF16)

    qvo = jnp.dot(hb, wnat_ref[:, 2 * D_CONV:], preferred_element_type=F32)
    q = (qvo[:, 0:D_GLA_K] * (GLA_DK ** -0.5)).astype(BF16)
    v = qvo[:, D_GLA_K:D_GLA_K + D_GLA_V].astype(BF16)
    og = qvo[:, D_GLA_K + D_GLA_V:]
    tt = lax.dot_general(wt_ref[...], hb, NT_DIMS, preferred_element_type=F32)
    k_t = tt[0:D_GLA_K, :]
    lr_t = tt[D_GLA_K:, :].astype(BF16)
    z_t = jnp.dot(wup_ref[...], lr_t, preferred_element_type=F32) + bup_ref[...]
    la_t = (jnp.minimum(z_t, 0.0) - jnp.log(1.0 + jnp.exp(-jnp.abs(z_t)))) * (1.0 / GATE_TAU)
    la_b = la_t.astype(BF16)
    slab = ubd_ref.shape[0]
    bcum = jnp.concatenate(
        [jnp.dot(la_b[:, s0:s0 + slab], ubd_ref[...], preferred_element_type=F32)
         for s0 in range(0, ts, slab)], axis=1)
    bend = jnp.concatenate(
        [jnp.dot(la_b[:, s0:s0 + slab], ebd_ref[...], preferred_element_type=F32)
         for s0 in range(0, ts, slab)], axis=1)
    kd_t = (k_t * jnp.exp(bend - bcum)).astype(BF16)
    dec_t = jnp.exp(bend)

    st = [st_ref[hh] for hh in range(GLA_HEADS)]
    gn = gnorm_ref[...]
    for c in range(nch):
        lo, hi_ = c * CHUNK, (c + 1) * CHUNK
        kd_c = kd_t[:, lo:hi_]
        v_c = v[lo:hi_, :]
        for hh in range(GLA_HEADS):
            ks, ke = hh * GLA_DK, (hh + 1) * GLA_DK
            vs, ve = hh * GLA_DV, (hh + 1) * GLA_DV
            upd = jnp.dot(kd_c[ks:ke, :], v_c[:, vs:ve], preferred_element_type=F32)
            dec = jnp.broadcast_to(dec_t[ks:ke, lo:lo + 1], (GLA_DK, GLA_DV))
            st[hh] = dec * st[hh] + upd
            sbd_ref[ks:ke, vs:ve] = st[hh].astype(BF16)
        o_c = jnp.dot(q[lo:hi_, :], sbd_ref[...], preferred_element_type=F32)
        og_c = og[lo:hi_, :]
        for hh in range(GLA_HEADS):
            vs, ve = hh * GLA_DV, (hh + 1) * GLA_DV
            on = _rms(o_c[:, vs:ve], gn)
            gate = og_c[:, vs:ve]
            ycat_ref[lo:hi_, D_CONV + vs:D_CONV + ve] = (on * (gate * _sigmoid(gate))).astype(BF16)
    for hh in range(GLA_HEADS):
        st_ref[hh] = st[hh]

    y = jnp.dot(ycat_ref[...], wout_ref[...], preferred_element_type=F32)
    x1 = x + g1 * _rms(y, gpost_ref[...])
    x1_ref[0] = x1
    h2 = _rms(x1, gffn_ref[...]) * (1.0 + sc2) + sh2
    d = h2.shape[1]
    h2p_ref[0, :, 0:d] = h2
    h2_hi, h2_lo = _split_bf16(h2)
    lg2 = lax.dot_general(wr_ref[...], h2_hi, NT_DIMS, preferred_element_type=F32)
    lg1 = lax.dot_general(wr_ref[0:N_EXPERTS, :], h2_lo, NT_DIMS, preferred_element_type=F32)
    logits_t = lg2[0:N_EXPERTS, :] + lg2[N_EXPERTS:, :] + lg1
    cls, wa, wb = _route(logits_t, br_ref[...])
    route_ref[0] = jnp.concatenate([cls, jnp.zeros((SUBLANES - 1, ts), F32)], axis=0)
    h2p_ref[0, :, d:d + LANES] = jnp.transpose(jnp.broadcast_to(wa, (LANES, ts)))
    h2p_ref[0, :, d + LANES:] = jnp.transpose(jnp.broadcast_to(wb, (LANES, ts)))


def _mix(x, mod, lw, consts, ts):
    b, s, d = x.shape
    nst = s // ts
    full = lambda shape: pl.BlockSpec(shape, lambda bi, ji: (0,) * len(shape))
    in_specs = [
        pl.BlockSpec((1, ts, d), lambda bi, ji: (bi, ji, 0)),
        pl.BlockSpec((1, 6, d), lambda bi, ji: (bi, 0, 0)),
        full((1, d)), full((1, d)), full((1, d)),
        full(lw["wnat"].shape), full(lw["wt"].shape),
        full((CONV_WIDTH * SUBLANES, D_CONV)), full((1, D_CONV)), full((1, D_CONV)), full((1, D_CONV)),
        full((D_GLA_K, GATE_RANK)), full((D_GLA_K, 1)), full((1, GLA_DV)),
        full((d, d)), full((2 * N_EXPERTS, d)), full((N_EXPERTS, 1)),
        full(consts["ubd"].shape), full(consts["ebd"].shape),
    ]
    out_specs = [
        pl.BlockSpec((1, ts, d), lambda bi, ji: (bi, ji, 0)),
        pl.BlockSpec((1, ts, d + 2 * LANES), lambda bi, ji: (bi, ji, 0)),
        pl.BlockSpec((1, 8, ts), lambda bi, ji: (bi * nst + ji, 0, 0)),
    ]
    out_shape = [
        jax.ShapeDtypeStruct((b, s, d), F32),
        jax.ShapeDtypeStruct((b, s, d + 2 * LANES), F32),
        jax.ShapeDtypeStruct((b * nst, 8, ts), F32),
    ]
    scratch = [
        pltpu.VMEM((CONV_PAD + ts, D_CONV), F32),
        pltpu.VMEM((SUBLANES - 1, CONV_PAD + ts - SUBLANES, D_CONV), F32),
        pltpu.VMEM((ts, D_CONV), F32),
        pltpu.VMEM((GLA_HEADS, GLA_DK, GLA_DV), F32),
        pltpu.VMEM((D_GLA_K, D_GLA_V), BF16),
        pltpu.VMEM((ts, d), BF16),
    ]
    return pl.pallas_call(
        functools.partial(_mix_kernel, ts=ts),
        grid=(b, nst),
        in_specs=in_specs,
        out_specs=out_specs,
        out_shape=out_shape,
        scratch_shapes=scratch,
        compiler_params=pltpu.CompilerParams(
            dimension_semantics=("arbitrary", "arbitrary"),
            vmem_limit_bytes=56 * 1024 * 1024),
        name="mix",
    )(x, mod, lw["gpre"], lw["gpost"], lw["gffn"], lw["wnat"], lw["wt"], lw["wdw"], lw["bdw"],
      lw["lng"], lw["lnb"], lw["wup"], lw["bup"], lw["gnorm"], lw["wout"], consts["wr"],
      consts["br"], consts["ubd"], consts["ebd"])


def _row_copy(idx_ref, base, r, src_hbm, dst_vmem, sem):
    tok = idx_ref[base + r]
    return pltpu.make_async_copy(src_hbm.at[pl.ds(tok, 1), :], dst_vmem.at[pl.ds(r, 1), :], sem)


def _row_gather_start(idx_ref, base, src_hbm, dst_vmem, sem, n_rows, unrolled):
    if unrolled:
        for r in range(n_rows):
            _row_copy(idx_ref, base, r, src_hbm, dst_vmem, sem).start()
    else:
        def body(r, carry):
            _row_copy(idx_ref, base, r, src_hbm, dst_vmem, sem).start()
            return carry
        lax.fori_loop(0, n_rows, body, 0)


def _row_gather_wait(src_hbm, dst_vmem, sem, n_rows):
    pltpu.make_async_copy(src_hbm.at[pl.ds(0, n_rows), :], dst_vmem, sem).wait()


def _moe_kernel(te1_ref, te2_ref, tvalid_ref, rowtok_ref,
                h2p_hbm, w1a_ref, w3a_ref, w2a_ref, w1b_ref, w3b_ref, w2b_ref, gpost_ref,
                z_ref, xg, sems, *, tm):
    i = pl.program_id(0)
    n = pl.num_programs(0)
    slot = lax.rem(i, 2)
    d = z_ref.shape[1]

    @pl.when(i == 0)
    def _():
        _row_gather_start(rowtok_ref, 0, h2p_hbm, xg.at[0], sems.at[0], tm, unrolled=False)

    _row_gather_wait(h2p_hbm, xg.at[slot], sems.at[slot], tm)

    n_slices = 6
    bounds = [(tm * k) // n_slices for k in range(n_slices + 1)]
    nxt = jnp.minimum(i + 1, n - 1)

    def issue(k):
        for r in range(bounds[k], bounds[k + 1]):
            _row_copy(rowtok_ref, nxt * tm, r, h2p_hbm, xg.at[1 - slot], sems.at[1 - slot]).start()

    @pl.when(tvalid_ref[i] == 0)
    def _():
        for k in range(n_slices):
            issue(k)
        z_ref[...] = jnp.zeros_like(z_ref)

    @pl.when(tvalid_ref[i] != 0)
    def _():
        h = xg[slot, :, 0:d].astype(BF16)
        wcols = [xg[slot, :, d + e * LANES:d + (e + 1) * LANES] for e in range(2)]
        y = None
        for e, (w1, w3, w2) in enumerate(((w1a_ref, w3a_ref, w2a_ref), (w1b_ref, w3b_ref, w2b_ref))):
            issue(3 * e)
            g = jnp.dot(h, w1[0, 0], preferred_element_type=F32)
            issue(3 * e + 1)
            u = jnp.dot(h, w3[0, 0], preferred_element_type=F32)
            he = ((g * _sigmoid(g)) * u).astype(BF16)
            issue(3 * e + 2)
            ye = jnp.dot(he, w2[0, 0], preferred_element_type=F32)
            ye = ye * jnp.concatenate([wcols[e]] * (d // LANES), axis=1)
            y = ye if y is None else y + ye
        z_ref[...] = _rms(y, gpost_ref[...])

    @pl.when(i == n - 1)
    def _():
        _row_gather_wait(h2p_hbm, xg.at[1 - slot], sems.at[1 - slot], tm)


def _moe(h2p, rowtok, te1, te2, tvalid, w1, w3, w2, gpost, layer, tm):
    t, dx = h2p.shape
    d = dx - 2 * LANES
    nt = te1.shape[0]
    wspec_a = lambda shape: pl.BlockSpec((1, 1) + shape, lambda i, e1, e2, tv, rt: (layer, e1[i], 0, 0))
    wspec_b = lambda shape: pl.BlockSpec((1, 1) + shape, lambda i, e1, e2, tv, rt: (layer, e2[i], 0, 0))
    grid_spec = pltpu.PrefetchScalarGridSpec(
        num_scalar_prefetch=4,
        grid=(nt,),
        in_specs=[
            pl.BlockSpec(memory_space=pl.ANY),
            wspec_a((d, D_EXPERT_PAD)), wspec_a((d, D_EXPERT_PAD)), wspec_a((D_EXPERT_PAD, d)),
            wspec_b((d, D_EXPERT_PAD)), wspec_b((d, D_EXPERT_PAD)), wspec_b((D_EXPERT_PAD, d)),
            pl.BlockSpec((1, d), lambda i, e1, e2, tv, rt: (0, 0)),
        ],
        out_specs=pl.BlockSpec((tm, d), lambda i, e1, e2, tv, rt: (i, 0)),
        scratch_shapes=[
            pltpu.VMEM((2, tm, dx), F32),
            pltpu.SemaphoreType.DMA((2,)),
        ],
    )
    return pl.pallas_call(
        functools.partial(_moe_kernel, tm=tm),
        grid_spec=grid_spec,
        out_shape=jax.ShapeDtypeStruct((nt * tm, d), F32),
        compiler_params=pltpu.CompilerParams(
            dimension_semantics=("arbitrary",),
            vmem_limit_bytes=48 * 1024 * 1024),
        name="moe",
    )(te1, te2, tvalid, rowtok, h2p, w1, w3, w2, w1, w3, w2, gpost)


def _combine_kernel(pos_ref, x1_ref, g2_ref, z_hbm, o_ref, zg, sems, *, tc):
    i = pl.program_id(0)
    n = pl.num_programs(0)
    slot = lax.rem(i, 2)

    @pl.when(i == 0)
    def _():
        _row_gather_start(pos_ref, 0, z_hbm, zg.at[0], sems.at[0], tc, unrolled=False)

    @pl.when(i + 1 < n)
    def _():
        _row_gather_start(pos_ref, (i + 1) * tc, z_hbm, zg.at[1 - slot], sems.at[1 - slot], tc,
                          unrolled=True)

    _row_gather_wait(z_hbm, zg.at[slot], sems.at[slot], tc)
    o_ref[...] = x1_ref[...] + g2_ref[0] * zg[slot]


def _combine(x1, g2, z, pos, seq_len, tc):
    t, d = x1.shape
    per_seq = seq_len // tc
    grid_spec = pltpu.PrefetchScalarGridSpec(
        num_scalar_prefetch=1,
        grid=(t // tc,),
        in_specs=[
            pl.BlockSpec((tc, d), lambda i, pos: (i, 0)),
            pl.BlockSpec((1, 1, d), lambda i, pos: (i // per_seq, 0, 0)),
            pl.BlockSpec(memory_space=pl.ANY),
        ],
        out_specs=pl.BlockSpec((tc, d), lambda i, pos: (i, 0)),
        scratch_shapes=[
            pltpu.VMEM((2, tc, d), F32),
            pltpu.SemaphoreType.DMA((2,)),
        ],
    )
    return pl.pallas_call(
        functools.partial(_combine_kernel, tc=tc),
        grid_spec=grid_spec,
        out_shape=jax.ShapeDtypeStruct((t, d), F32),
        compiler_params=pltpu.CompilerParams(
            dimension_semantics=("arbitrary",),
            vmem_limit_bytes=32 * 1024 * 1024),
        name="combine",
    )(pos, x1, g2, z)


def _class_tables():
    e1, e2 = [], []
    for g in range(N_GROUPS):
        for a in range(EXPERTS_PER_GROUP):
            for b in range(a + 1, EXPERTS_PER_GROUP):
                e1.append(g * EXPERTS_PER_GROUP + a)
                e2.append(g * EXPERTS_PER_GROUP + b)
    return jnp.asarray(e1, jnp.int32), jnp.asarray(e2, jnp.int32)


def _rowtok_kernel(pos_ref, o_ref):
    n_rows = o_ref.shape[0]
    n_tok = pos_ref.shape[0]

    def zero(it, carry):
        for u in range(INDEX_UNROLL):
            o_ref[it * INDEX_UNROLL + u] = 0
        return carry

    def scatter(it, carry):
        for u in range(INDEX_UNROLL):
            t = it * INDEX_UNROLL + u
            o_ref[pos_ref[t]] = t
        return carry

    lax.fori_loop(0, n_rows // INDEX_UNROLL, zero, 0)
    lax.fori_loop(0, n_tok // INDEX_UNROLL, scatter, 0)


def _rowtok(pos, n_rows):
    assert n_rows % INDEX_UNROLL == 0 and pos.shape[0] % INDEX_UNROLL == 0
    return pl.pallas_call(
        _rowtok_kernel,
        in_specs=[pl.BlockSpec(memory_space=pltpu.SMEM)],
        out_specs=pl.BlockSpec(memory_space=pltpu.SMEM),
        out_shape=jax.ShapeDtypeStruct((n_rows,), jnp.int32),
        name="rowtok",
    )(pos)


def _plan(cls, tm):
    t = cls.shape[0]
    nt = t // tm + N_CLASSES
    onehot = (cls[:, None] == jnp.arange(N_CLASSES, dtype=jnp.int32)[None, :]).astype(jnp.int32)
    csum = jnp.cumsum(onehot, axis=0)
    counts = csum[-1]
    rank = jnp.take_along_axis(csum, cls[:, None], axis=1)[:, 0] - 1
    tiles = (counts + tm - 1) // tm
    tile_end = jnp.cumsum(tiles)
    tile_start = tile_end - tiles
    pos = tile_start[cls] * tm + rank
    tile_ids = jnp.arange(nt, dtype=jnp.int32)
    tile_cls = jnp.sum((tile_end[None, :] <= tile_ids[:, None]).astype(jnp.int32), axis=1)
    tvalid = (tile_ids < tile_end[-1]).astype(jnp.int32)
    last_cls = jnp.max(jnp.where(counts > 0, jnp.arange(N_CLASSES, dtype=jnp.int32), 0))
    tile_cls = jnp.where(tvalid == 1, jnp.minimum(tile_cls, N_CLASSES - 1), last_cls)
    ce1, ce2 = _class_tables()
    pos = pos.astype(jnp.int32)
    return pos, _rowtok(pos, nt * tm), ce1[tile_cls], ce2[tile_cls], tvalid


def _layer_weights(l, g_pre_mix, g_post_mix, g_pre_ffn, g_post_ffn, w_in, w_dw, b_dw, conv_ln_g,
                   conv_ln_b, w_gate_up, b_gate, gla_norm_g, w_out):
    d = w_in.shape[1]
    wi = w_in[l]
    o_q = 2 * D_CONV
    o_k = o_q + D_GLA_K
    o_v = o_k + D_GLA_K
    o_g = o_v + D_GLA_V
    o_lr = o_g + D_GLA_V
    wnat = jnp.concatenate([wi[:, 0:o_q], wi[:, o_q:o_k], wi[:, o_v:o_g], wi[:, o_g:o_lr]], axis=1)
    wt = jnp.concatenate([wi[:, o_k:o_v], wi[:, o_lr:]], axis=1).T
    return dict(
        gpre=g_pre_mix[l].reshape(1, d), gpost=g_post_mix[l].reshape(1, d),
        gffn=g_pre_ffn[l].reshape(1, d), gpostffn=g_post_ffn[l].reshape(1, d),
        wnat=wnat.astype(BF16), wt=wt.astype(BF16),
        wdw=jnp.repeat(w_dw[l], SUBLANES, axis=0), bdw=b_dw[l].reshape(1, D_CONV),
        lng=conv_ln_g[l].reshape(1, D_CONV), lnb=conv_ln_b[l].reshape(1, D_CONV),
        wup=w_gate_up[l].T.astype(BF16), bup=b_gate[l].reshape(D_GLA_K, 1),
        gnorm=gla_norm_g[l].reshape(1, GLA_DV),
        wout=w_out[l].astype(BF16),
    )


def kernel(x, c, w_ada, b_ada, g_pre_mix, g_post_mix, g_pre_ffn, g_post_ffn, w_in, w_dw, b_dw,
           conv_ln_g, conv_ln_b, w_gate_up, b_gate, gla_norm_g, w_out, w_router, b_router, w1, w3, w2):
    b, s, d = x.shape
    depth = w_ada.shape[0]
    t = b * s
    ts = min(SEQ_TILE, s)
    tm = min(MOE_TILE, t)
    tc = min(COMBINE_TILE, s)
    assert s % ts == 0 and ts % CHUNK == 0 and t % tm == 0 and s % tc == 0

    mod = _modulation(c, w_ada, b_ada).reshape(depth, b, 6, d)

    wr_hi, wr_lo = _split_bf16(w_router.T)
    tok = jnp.arange(min(ts, MXU_TILE), dtype=jnp.int32)
    same = (tok[:, None] // CHUNK) == (tok[None, :] // CHUNK)
    consts = dict(
        wr=jnp.concatenate([wr_hi, wr_lo], axis=0),
        br=b_router.reshape(N_EXPERTS, 1),
        ubd=(same & (tok[:, None] <= tok[None, :])).astype(BF16),
        ebd=same.astype(BF16),
    )

    w1b = _cast_pad(w1, d, D_EXPERT_PAD)
    w3b = _cast_pad(w3, d, D_EXPERT_PAD)
    w2b = _cast_pad(w2, D_EXPERT_PAD, d)

    for l in range(depth):
        lw = _layer_weights(l, g_pre_mix, g_post_mix, g_pre_ffn, g_post_ffn, w_in, w_dw, b_dw,
                            conv_ln_g, conv_ln_b, w_gate_up, b_gate, gla_norm_g, w_out)
        x1, h2p, route = _mix(x, mod[l], lw, consts, ts)
        cls = route[:, 0, :].reshape(t).astype(jnp.int32)
        pos, rowtok, te1, te2, tvalid = _plan(cls, tm)
        z = _moe(h2p.reshape(t, d + 2 * LANES), rowtok, te1, te2, tvalid, w1b, w3b, w2b,
                 lw["gpostffn"], l, tm)
        g2 = mod[l][:, 5:6, :]
        x = _combine(x1.reshape(t, d), g2, z, pos, s, tc).reshape(b, s, d)
    return x
```

```python
import functools

import jax
import jax.numpy as jnp
from jax import lax
from jax.experimental import pallas as pl
from jax.experimental.pallas import tpu as pltpu

CHUNK = 64
CONV_WIDTH = 31
D_CONV = 512
GLA_HEADS = 4
GLA_DV = 128
GLA_DK = 64
D_GLA_K = GLA_HEADS * GLA_DK
D_GLA_V = GLA_HEADS * GLA_DV
GATE_RANK = 16
GATE_TAU = 16.0
N_EXPERTS = 16
N_GROUPS = 4
EXPERTS_PER_GROUP = 4
PAIRS_PER_GROUP = 6
N_CLASSES = N_GROUPS * PAIRS_PER_GROUP
D_EXPERT = 704
D_EXPERT_PAD = 768
EPS = 1e-6

SUBLANES = 8
LANES = 128
MXU_TILE = 256
CONV_PAD = 32
CONV_ROWS = 32
CONV_LANES = 256
SEQ_TILE = 512
MOE_TILE = 256
COMBINE_TILE = 256
INDEX_UNROLL = 16
DMA_QUEUES = 2
GATHER_SLOTS = 3

F32 = jnp.float32
BF16 = jnp.bfloat16
NT_DIMS = (((1,), (1,)), ((), ()))
TN_DIMS = (((0,), (0,)), ((), ()))


def _sigmoid(x):
    return 1.0 / (1.0 + jnp.exp(-x))


def _rms(x, g):
    ms = jnp.mean(x * x, axis=-1, keepdims=True)
    return x * lax.rsqrt(ms + EPS) * g


def _split_bf16(x):
    hi = x.astype(BF16)
    lo = (x - hi.astype(F32)).astype(BF16)
    return hi, lo


def _mod_kernel(c_ref, w_ref, b_ref, o_ref):
    c = c_ref[...]
    ca = c * _sigmoid(c)
    a_hi, a_lo = _split_bf16(ca)
    w_hi, w_lo = _split_bf16(w_ref[0])
    acc = jnp.dot(a_hi, w_hi, preferred_element_type=F32)
    acc += jnp.dot(a_lo, w_hi, preferred_element_type=F32)
    acc += jnp.dot(a_hi, w_lo, preferred_element_type=F32)
    o_ref[0] = acc + b_ref[0]


def _modulation(c, w_ada, b_ada):
    depth, d, d6 = w_ada.shape
    b = c.shape[0]
    nblk = d6 // d
    return pl.pallas_call(
        _mod_kernel,
        grid=(depth, nblk),
        in_specs=[
            pl.BlockSpec((b, d), lambda l, n: (0, 0)),
            pl.BlockSpec((1, d, d), lambda l, n: (l, 0, n)),
            pl.BlockSpec((1, 1, d), lambda l, n: (l, 0, n)),
        ],
        out_specs=pl.BlockSpec((1, b, d), lambda l, n: (l, 0, n)),
        out_shape=jax.ShapeDtypeStruct((depth, b, d6), F32),
        compiler_params=pltpu.CompilerParams(
            dimension_semantics=("arbitrary", "arbitrary"),
            vmem_limit_bytes=32 * 1024 * 1024),
        name="adaln_mod",
    )(c, w_ada, b_ada.reshape(depth, 1, d6))


def _cast_pad_kernel(w_ref, o_ref):
    rows, cols = w_ref.shape[2:]
    prow, pcol = o_ref.shape[2:]
    o_ref[0, 0, 0:rows, 0:cols] = w_ref[0, 0].astype(BF16)
    if pcol > cols:
        o_ref[0, 0, :, cols:] = jnp.zeros((prow, pcol - cols), BF16)
    if prow > rows:
        o_ref[0, 0, rows:, 0:cols] = jnp.zeros((prow - rows, cols), BF16)


def _cast_pad(w, prow, pcol):
    depth, ne, rows, cols = w.shape
    return pl.pallas_call(
        _cast_pad_kernel,
        grid=(depth, ne),
        in_specs=[pl.BlockSpec((1, 1, rows, cols), lambda l, e: (l, e, 0, 0))],
        out_specs=pl.BlockSpec((1, 1, prow, pcol), lambda l, e: (l, e, 0, 0)),
        out_shape=jax.ShapeDtypeStruct((depth, ne, prow, pcol), BF16),
        compiler_params=pltpu.CompilerParams(
            dimension_semantics=("arbitrary", "arbitrary"),
            vmem_limit_bytes=32 * 1024 * 1024),
        name="cast_pad",
    )(w)


def _route(logits_t, br):
    m = jnp.max(logits_t, axis=0, keepdims=True)
    e = jnp.exp(logits_t - m)
    probs = e / jnp.sum(e, axis=0, keepdims=True)
    sel = probs + br
    one = jnp.ones_like(m)
    zero = jnp.zeros_like(m)
    picked, gscore = [], []
    for g in range(N_GROUPS):
        rows = [sel[g * EXPERTS_PER_GROUP + k:g * EXPERTS_PER_GROUP + k + 1, :]
                for k in range(EXPERTS_PER_GROUP)]
        score = zero
        for k in range(EXPERTS_PER_GROUP):
            rank = zero
            for k2 in range(EXPERTS_PER_GROUP):
                if k2 == k:
                    continue
                ahead = (rows[k2] >= rows[k]) if k2 < k else (rows[k2] > rows[k])
                rank = rank + jnp.where(ahead, one, zero)
            pk = jnp.where(rank < 2.0, one, zero)
            picked.append(pk)
            score = score + pk * rows[k]
        gscore.append(score)
    best = gscore[0]
    for g in range(1, N_GROUPS):
        best = jnp.maximum(best, gscore[g])
    taken = zero
    flag = [zero] * EXPERTS_PER_GROUP
    prob = [zero] * EXPERTS_PER_GROUP
    gbase = zero
    for g in range(N_GROUPS):
        isb = jnp.where(gscore[g] == best, one, zero) * (one - taken)
        taken = taken + isb
        gbase = gbase + isb * float(g * PAIRS_PER_GROUP)
        for k in range(EXPERTS_PER_GROUP):
            ei = g * EXPERTS_PER_GROUP + k
            flag[k] = flag[k] + isb * picked[ei]
            prob[k] = prob[k] + isb * probs[ei:ei + 1, :]
    f0, f1, f2, f3 = flag
    pair = f0 * (f2 * 1.0 + f3 * 2.0) + (one - f0) * (f1 * (f2 * 3.0 + f3 * 4.0) + (one - f1) * 5.0)
    cls = gbase + pair
    wa_raw = f0 * prob[0] + (one - f0) * (f1 * prob[1] + (one - f1) * prob[2])
    tot = f0 * prob[0] + f1 * prob[1] + f2 * prob[2] + f3 * prob[3]
    wa = wa_raw / tot
    wb = (tot - wa_raw) / tot
    return cls, wa, wb


def _mix_kernel(x_ref, mod_ref, gpre_ref, gpost_ref, gffn_ref, wnat_ref, wt_ref, wdw_ref, bdw_ref,
                lng_ref, lnb_ref, wup_ref, bup_ref, gnorm_ref, wout_ref, wr_ref, br_ref,
                ubd_ref, ebd_ref,
                x1_ref, h2p_ref, route_ref,
                ubuf, ush, cbuf, st_ref, sbd_ref, ycat_ref, *, ts):
    j = pl.program_id(1)
    nch = ts // CHUNK

    @pl.when(j == 0)
    def _():
        ubuf[0:CONV_PAD, :] = jnp.zeros((CONV_PAD, D_CONV), F32)
        st_ref[...] = jnp.zeros_like(st_ref)
        sbd_ref[...] = jnp.zeros_like(sbd_ref)

    @pl.when(j > 0)
    def _():
        ubuf[0:CONV_PAD, :] = ubuf[ts:ts + CONV_PAD, :]

    x = x_ref[0]
    sh1 = mod_ref[0, 0:1, :]
    sc1 = mod_ref[0, 1:2, :]
    g1 = mod_ref[0, 2:3, :]
    sh2 = mod_ref[0, 3:4, :]
    sc2 = mod_ref[0, 4:5, :]

    h = _rms(x, gpre_ref[...] * (1.0 + sc1)) + sh1
    hb = h.astype(BF16)

    cvg = jnp.dot(hb, wnat_ref[:, 0:2 * D_CONV], preferred_element_type=F32)
    ubuf[CONV_PAD:CONV_PAD + ts, :] = cvg[:, 0:D_CONV] * _sigmoid(cvg[:, D_CONV:2 * D_CONV])

    for r in range(1, SUBLANES):
        ush[r - 1] = ubuf[r:r + ts + CONV_PAD - SUBLANES, :]

    groups = CONV_ROWS // SUBLANES
    first_off = CONV_PAD - (CONV_WIDTH - 1)

    def conv_block(rb, carry, c0):
        r0 = rb * CONV_ROWS
        cs = slice(c0, c0 + CONV_LANES)
        acc = [jnp.broadcast_to(bdw_ref[:, cs], (SUBLANES, CONV_LANES)) for _ in range(groups)]
        for r in range(SUBLANES):
            taps = [tap for tap in range(CONV_WIDTH) if (first_off + tap) % SUBLANES == r]
            shifts = [(first_off + tap) // SUBLANES for tap in taps]
            src = ubuf if r == 0 else ush.at[r - 1]
            blk = {m: src[pl.ds(r0 + m * SUBLANES, SUBLANES), cs]
                   for m in range(min(shifts), max(shifts) + groups)}
            for tap, a in zip(taps, shifts):
                w8 = wdw_ref[tap * SUBLANES:(tap + 1) * SUBLANES, cs]
                for g in range(groups):
                    acc[g] = acc[g] + w8 * blk[a + g]
        cbuf[pl.ds(r0, CONV_ROWS), cs] = jnp.concatenate(acc, axis=0)
        return carry

    for c0 in range(0, D_CONV, CONV_LANES):
        for rb in range(ts // CONV_ROWS):
            conv_block(rb, 0, c0)
    cv = cbuf[...]
    mu = jnp.mean(cv, axis=-1, keepdims=True)
    cen = cv - mu
    var = jnp.mean(cen * cen, axis=-1, keepdims=True)
    yn = cen * lax.rsqrt(var + EPS) * lng_ref[...] + lnb_ref[...]
    ycat_ref[:, 0:D_CONV] = (yn * _sigmoid(yn)).astype(BF16)

    qvo = jnp.dot(hb, wnat_ref[:, 2 * D_CONV:], preferred_element_type=F32)
    q = (qvo[:, 0:D_GLA_K] * (GLA_DK ** -0.5)).astype(BF16)
    v = qvo[:, D_GLA_K:D_GLA_K + D_GLA_V].astype(BF16)
    og = qvo[:, D_GLA_K + D_GLA_V:]
    tt = lax.dot_general(wt_ref[...], hb, NT_DIMS, preferred_element_type=F32)
    k_t = tt[0:D_GLA_K, :]
    lr_t = tt[D_GLA_K:, :].astype(BF16)
    z_t = jnp.dot(wup_ref[...], lr_t, preferred_element_type=F32) + bup_ref[...]
    la_t = (jnp.minimum(z_t, 0.0) - jnp.log(1.0 + jnp.exp(-jnp.abs(z_t)))) * (1.0 / GATE_TAU)
    la_b = la_t.astype(BF16)
    slab = ubd_ref.shape[0]
    bcum = jnp.concatenate(
        [jnp.dot(la_b[:, s0:s0 + slab], ubd_ref[...], preferred_element_type=F32)
         for s0 in range(0, ts, slab)], axis=1)
    bend = jnp.concatenate(
        [jnp.dot(la_b[:, s0:s0 + slab], ebd_ref[...], preferred_element_type=F32)
         for s0 in range(0, ts, slab)], axis=1)
    kd_t = (k_t * jnp.exp(bend - bcum)).astype(BF16)
    dec_t = jnp.exp(bend)

    st = [st_ref[hh] for hh in range(GLA_HEADS)]
    gn = gnorm_ref[...]
    for c in range(nch):
        lo, hi_ = c * CHUNK, (c + 1) * CHUNK
        kd_c = kd_t[:, lo:hi_]
        v_c = v[lo:hi_, :]
        for hh in range(GLA_HEADS):
            ks, ke = hh * GLA_DK, (hh + 1) * GLA_DK
            vs, ve = hh * GLA_DV, (hh + 1) * GLA_DV
            upd = jnp.dot(kd_c[ks:ke, :], v_c[:, vs:ve], preferred_element_type=F32)
            dec = jnp.broadcast_to(dec_t[ks:ke, lo:lo + 1], (GLA_DK, GLA_DV))
            st[hh] = dec * st[hh] + upd
            sbd_ref[ks:ke, vs:ve] = st[hh].astype(BF16)
        o_c = jnp.dot(q[lo:hi_, :], sbd_ref[...], preferred_element_type=F32)
        og_c = og[lo:hi_, :]
        for hh in range(GLA_HEADS):
            vs, ve = hh * GLA_DV, (hh + 1) * GLA_DV
            on = _rms(o_c[:, vs:ve], gn)
            gate = og_c[:, vs:ve]
            ycat_ref[lo:hi_, D_CONV + vs:D_CONV + ve] = (on * (gate * _sigmoid(gate))).astype(BF16)
    for hh in range(GLA_HEADS):
        st_ref[hh] = st[hh]

    y = jnp.dot(ycat_ref[...], wout_ref[...], preferred_element_type=F32)
    x1 = x + g1 * _rms(y, gpost_ref[...])
    x1_ref[0] = x1
    h2 = _rms(x1, gffn_ref[...] * (1.0 + sc2)) + sh2
    d = h2.shape[1]
    h2p_ref[0, :, 0:d] = h2
    h2_hi, h2_lo = _split_bf16(h2)
    lg2 = lax.dot_general(wr_ref[...], h2_hi, NT_DIMS, preferred_element_type=F32)
    lg1 = lax.dot_general(wr_ref[0:N_EXPERTS, :], h2_lo, NT_DIMS, preferred_element_type=F32)
    logits_t = lg2[0:N_EXPERTS, :] + lg2[N_EXPERTS:, :] + lg1
    cls, wa, wb = _route(logits_t, br_ref[...])
    route_ref[0] = jnp.concatenate([cls, jnp.zeros((SUBLANES - 1, ts), F32)], axis=0)
    h2p_ref[0, :, d:d + LANES] = jnp.transpose(jnp.broadcast_to(wa, (LANES, ts)))
    h2p_ref[0, :, d + LANES:] = jnp.transpose(jnp.broadcast_to(wb, (LANES, ts)))


def _mix(x, mod, lw, consts, ts):
    b, s, d = x.shape
    nst = s // ts
    full = lambda shape: pl.BlockSpec(shape, lambda bi, ji: (0,) * len(shape))
    in_specs = [
        pl.BlockSpec((1, ts, d), lambda bi, ji: (bi, ji, 0)),
        pl.BlockSpec((1, 6, d), lambda bi, ji: (bi, 0, 0)),
        full((1, d)), full((1, d)), full((1, d)),
        full(lw["wnat"].shape), full(lw["wt"].shape),
        full((CONV_WIDTH * SUBLANES, D_CONV)), full((1, D_CONV)), full((1, D_CONV)), full((1, D_CONV)),
        full((D_GLA_K, GATE_RANK)), full((D_GLA_K, 1)), full((1, GLA_DV)),
        full((d, d)), full((2 * N_EXPERTS, d)), full((N_EXPERTS, 1)),
        full(consts["ubd"].shape), full(consts["ebd"].shape),
    ]
    out_specs = [
        pl.BlockSpec((1, ts, d), lambda bi, ji: (bi, ji, 0)),
        pl.BlockSpec((1, ts, d + 2 * LANES), lambda bi, ji: (bi, ji, 0)),
        pl.BlockSpec((1, 8, ts), lambda bi, ji: (bi * nst + ji, 0, 0)),
    ]
    out_shape = [
        jax.ShapeDtypeStruct((b, s, d), F32),
        jax.ShapeDtypeStruct((b, s, d + 2 * LANES), F32),
        jax.ShapeDtypeStruct((b * nst, 8, ts), F32),
    ]
    scratch = [
        pltpu.VMEM((CONV_PAD + ts, D_CONV), F32),
        pltpu.VMEM((SUBLANES - 1, CONV_PAD + ts - SUBLANES, D_CONV), F32),
        pltpu.VMEM((ts, D_CONV), F32),
        pltpu.VMEM((GLA_HEADS, GLA_DK, GLA_DV), F32),
        pltpu.VMEM((D_GLA_K, D_GLA_V), BF16),
        pltpu.VMEM((ts, d), BF16),
    ]
    return pl.pallas_call(
        functools.partial(_mix_kernel, ts=ts),
        grid=(b, nst),
        in_specs=in_specs,
        out_specs=out_specs,
        out_shape=out_shape,
        scratch_shapes=scratch,
        compiler_params=pltpu.CompilerParams(
            dimension_semantics=("arbitrary", "arbitrary"),
            vmem_limit_bytes=56 * 1024 * 1024),
        name="mix",
    )(x, mod, lw["gpre"], lw["gpost"], lw["gffn"], lw["wnat"], lw["wt"], lw["wdw"], lw["bdw"],
      lw["lng"], lw["lnb"], lw["wup"], lw["bup"], lw["gnorm"], lw["wout"], consts["wr"],
      consts["br"], consts["ubd"], consts["ebd"])


def _row_copy(idx_ref, base, r, src_hbm, dst_vmem, sem):
    tok = idx_ref[base + r]
    return pltpu.make_async_copy(src_hbm.at[pl.ds(tok, 1), :], dst_vmem.at[pl.ds(r, 1), :], sem)


def _row_gather_start(idx_ref, base, src_hbm, dst_vmem, sem, n_rows, unrolled):
    if unrolled:
        for r in range(n_rows):
            _row_copy(idx_ref, base, r, src_hbm, dst_vmem, sem).start(priority=r % DMA_QUEUES)
    else:
        def body(r, carry):
            _row_copy(idx_ref, base, r, src_hbm, dst_vmem, sem).start()
            return carry
        lax.fori_loop(0, n_rows, body, 0)


def _row_gather_wait(src_hbm, dst_vmem, sem, n_rows):
    pltpu.make_async_copy(src_hbm.at[pl.ds(0, n_rows), :], dst_vmem, sem).wait()


def _gather_pipeline(idx_ref, src_hbm, buf, sems, n_rows):
    i = pl.program_id(0)
    n = pl.num_programs(0)

    @pl.when(i == 0)
    def _():
        for k in range(GATHER_SLOTS - 1):
            @pl.when(k < n)
            def _():
                _row_gather_start(idx_ref, k * n_rows, src_hbm, buf.at[k], sems.at[k], n_rows, unrolled=False)

    ahead = i + GATHER_SLOTS - 1

    @pl.when(ahead < n)
    def _():
        aslot = lax.rem(ahead, GATHER_SLOTS)
        _row_gather_start(idx_ref, ahead * n_rows, src_hbm, buf.at[aslot], sems.at[aslot], n_rows,
                          unrolled=True)

    slot = lax.rem(i, GATHER_SLOTS)
    _row_gather_wait(src_hbm, buf.at[slot], sems.at[slot], n_rows)
    return slot


def _moe_kernel(te1_ref, te2_ref, tvalid_ref, rowtok_ref,
                h2p_hbm, w1a_ref, w3a_ref, w2a_ref, w1b_ref, w3b_ref, w2b_ref, gpost_ref,
                z_ref, xg, sems, *, tm):
    i = pl.program_id(0)
    d = z_ref.shape[1]
    slot = _gather_pipeline(rowtok_ref, h2p_hbm, xg, sems, tm)

    @pl.when(tvalid_ref[i] == 0)
    def _():
        z_ref[...] = jnp.zeros_like(z_ref)

    @pl.when(tvalid_ref[i] != 0)
    def _():
        h = xg[slot, :, 0:d].astype(BF16)
        y = None
        for e, (w1, w3, w2) in enumerate(((w1a_ref, w3a_ref, w2a_ref), (w1b_ref, w3b_ref, w2b_ref))):
            g_t = lax.dot_general(w1[0, 0], h, NT_DIMS, preferred_element_type=F32)
            u_t = lax.dot_general(w3[0, 0], h, NT_DIMS, preferred_element_type=F32)
            he_t = ((g_t * _sigmoid(g_t)) * u_t).astype(BF16)
            ye = lax.dot_general(he_t, w2[0, 0], TN_DIMS, preferred_element_type=F32)
            wcol = xg[slot, :, d + e * LANES:d + (e + 1) * LANES]
            ye = ye * jnp.concatenate([wcol] * (d // LANES), axis=1)
            y = ye if y is None else y + ye
        z_ref[...] = _rms(y, gpost_ref[...])


def _moe(h2p, rowtok, te1, te2, tvalid, w1, w3, w2, gpost, layer, tm):
    t, dx = h2p.shape
    d = dx - 2 * LANES
    nt = te1.shape[0]
    wspec_a = lambda shape: pl.BlockSpec((1, 1) + shape, lambda i, e1, e2, tv, rt: (layer, e1[i], 0, 0))
    wspec_b = lambda shape: pl.BlockSpec((1, 1) + shape, lambda i, e1, e2, tv, rt: (layer, e2[i], 0, 0))
    grid_spec = pltpu.PrefetchScalarGridSpec(
        num_scalar_prefetch=4,
        grid=(nt,),
        in_specs=[
            pl.BlockSpec(memory_space=pl.ANY),
            wspec_a((D_EXPERT_PAD, d)), wspec_a((D_EXPERT_PAD, d)), wspec_a((D_EXPERT_PAD, d)),
            wspec_b((D_EXPERT_PAD, d)), wspec_b((D_EXPERT_PAD, d)), wspec_b((D_EXPERT_PAD, d)),
            pl.BlockSpec((1, d), lambda i, e1, e2, tv, rt: (0, 0)),
        ],
        out_specs=pl.BlockSpec((tm, d), lambda i, e1, e2, tv, rt: (i, 0)),
        scratch_shapes=[
            pltpu.VMEM((GATHER_SLOTS, tm, dx), F32),
            pltpu.SemaphoreType.DMA((GATHER_SLOTS,)),
        ],
    )
    return pl.pallas_call(
        functools.partial(_moe_kernel, tm=tm),
        grid_spec=grid_spec,
        out_shape=jax.ShapeDtypeStruct((nt * tm, d), F32),
        compiler_params=pltpu.CompilerParams(
            dimension_semantics=("arbitrary",),
            vmem_limit_bytes=48 * 1024 * 1024),
        name="moe",
    )(te1, te2, tvalid, rowtok, h2p, w1, w3, w2, w1, w3, w2, gpost)


def _combine_kernel(pos_ref, x1_ref, g2_ref, z_hbm, o_ref, zg, sems, *, tc):
    slot = _gather_pipeline(pos_ref, z_hbm, zg, sems, tc)
    o_ref[...] = x1_ref[...] + g2_ref[0] * zg[slot]


def _combine(x1, g2, z, pos, seq_len, tc):
    t, d = x1.shape
    per_seq = seq_len // tc
    grid_spec = pltpu.PrefetchScalarGridSpec(
        num_scalar_prefetch=1,
        grid=(t // tc,),
        in_specs=[
            pl.BlockSpec((tc, d), lambda i, pos: (i, 0)),
            pl.BlockSpec((1, 1, d), lambda i, pos: (i // per_seq, 0, 0)),
            pl.BlockSpec(memory_space=pl.ANY),
        ],
        out_specs=pl.BlockSpec((tc, d), lambda i, pos: (i, 0)),
        scratch_shapes=[
            pltpu.VMEM((GATHER_SLOTS, tc, d), F32),
            pltpu.SemaphoreType.DMA((GATHER_SLOTS,)),
        ],
    )
    return pl.pallas_call(
        functools.partial(_combine_kernel, tc=tc),
        grid_spec=grid_spec,
        out_shape=jax.ShapeDtypeStruct((t, d), F32),
        compiler_params=pltpu.CompilerParams(
            dimension_semantics=("arbitrary",),
            vmem_limit_bytes=32 * 1024 * 1024),
        name="combine",
    )(pos, x1, g2, z)


def _class_tables():
    e1, e2 = [], []
    for g in range(N_GROUPS):
        for a in range(EXPERTS_PER_GROUP):
            for b in range(a + 1, EXPERTS_PER_GROUP):
                e1.append(g * EXPERTS_PER_GROUP + a)
                e2.append(g * EXPERTS_PER_GROUP + b)
    return jnp.asarray(e1, jnp.int32), jnp.asarray(e2, jnp.int32)


def _rowtok_kernel(pos_ref, o_ref):
    n_rows = o_ref.shape[0]
    n_tok = pos_ref.shape[0]

    def zero(it, carry):
        for u in range(INDEX_UNROLL):
            o_ref[it * INDEX_UNROLL + u] = 0
        return carry

    def scatter(it, carry):
        for u in range(INDEX_UNROLL):
            t = it * INDEX_UNROLL + u
            o_ref[pos_ref[t]] = t
        return carry

    lax.fori_loop(0, n_rows // INDEX_UNROLL, zero, 0)
    lax.fori_loop(0, n_tok // INDEX_UNROLL, scatter, 0)


def _rowtok(pos, n_rows):
    assert n_rows % INDEX_UNROLL == 0 and pos.shape[0] % INDEX_UNROLL == 0
    return pl.pallas_call(
        _rowtok_kernel,
        in_specs=[pl.BlockSpec(memory_space=pltpu.SMEM)],
        out_specs=pl.BlockSpec(memory_space=pltpu.SMEM),
        out_shape=jax.ShapeDtypeStruct((n_rows,), jnp.int32),
        name="rowtok",
    )(pos)


def _plan(cls, tm):
    t = cls.shape[0]
    nt = t // tm + N_CLASSES
    onehot = (cls[:, None] == jnp.arange(N_CLASSES, dtype=jnp.int32)[None, :]).astype(jnp.int32)
    csum = jnp.cumsum(onehot, axis=0)
    counts = csum[-1]
    rank = jnp.take_along_axis(csum, cls[:, None], axis=1)[:, 0] - 1
    tiles = (counts + tm - 1) // tm
    tile_end = jnp.cumsum(tiles)
    tile_start = tile_end - tiles
    pos = tile_start[cls] * tm + rank
    tile_ids = jnp.arange(nt, dtype=jnp.int32)
    tile_cls = jnp.sum((tile_end[None, :] <= tile_ids[:, None]).astype(jnp.int32), axis=1)
    tvalid = (tile_ids < tile_end[-1]).astype(jnp.int32)
    last_cls = jnp.max(jnp.where(counts > 0, jnp.arange(N_CLASSES, dtype=jnp.int32), 0))
    tile_cls = jnp.where(tvalid == 1, jnp.minimum(tile_cls, N_CLASSES - 1), last_cls)
    ce1, ce2 = _class_tables()
    pos = pos.astype(jnp.int32)
    return pos, _rowtok(pos, nt * tm), ce1[tile_cls], ce2[tile_cls], tvalid


def _layer_weights(l, g_pre_mix, g_post_mix, g_pre_ffn, g_post_ffn, w_in, w_dw, b_dw, conv_ln_g,
                   conv_ln_b, w_gate_up, b_gate, gla_norm_g, w_out):
    d = w_in.shape[1]
    wi = w_in[l]
    o_q = 2 * D_CONV
    o_k = o_q + D_GLA_K
    o_v = o_k + D_GLA_K
    o_g = o_v + D_GLA_V
    o_lr = o_g + D_GLA_V
    wnat = jnp.concatenate([wi[:, 0:o_q], wi[:, o_q:o_k], wi[:, o_v:o_g], wi[:, o_g:o_lr]], axis=1)
    wt = jnp.concatenate([wi[:, o_k:o_v], wi[:, o_lr:]], axis=1).T
    return dict(
        gpre=g_pre_mix[l].reshape(1, d), gpost=g_post_mix[l].reshape(1, d),
        gffn=g_pre_ffn[l].reshape(1, d), gpostffn=g_post_ffn[l].reshape(1, d),
        wnat=wnat.astype(BF16), wt=wt.astype(BF16),
        wdw=jnp.repeat(w_dw[l], SUBLANES, axis=0), bdw=b_dw[l].reshape(1, D_CONV),
        lng=conv_ln_g[l].reshape(1, D_CONV), lnb=conv_ln_b[l].reshape(1, D_CONV),
        wup=w_gate_up[l].T.astype(BF16), bup=b_gate[l].reshape(D_GLA_K, 1),
        gnorm=gla_norm_g[l].reshape(1, GLA_DV),
        wout=w_out[l].astype(BF16),
    )


def kernel(x, c, w_ada, b_ada, g_pre_mix, g_post_mix, g_pre_ffn, g_post_ffn, w_in, w_dw, b_dw,
           conv_ln_g, conv_ln_b, w_gate_up, b_gate, gla_norm_g, w_out, w_router, b_router, w1, w3, w2):
    b, s, d = x.shape
    depth = w_ada.shape[0]
    t = b * s
    ts = min(SEQ_TILE, s)
    tm = min(MOE_TILE, t)
    tc = min(COMBINE_TILE, s)
    assert s % ts == 0 and ts % CHUNK == 0 and t % tm == 0 and s % tc == 0

    mod = _modulation(c, w_ada, b_ada).reshape(depth, b, 6, d)

    wr_hi, wr_lo = _split_bf16(w_router.T)
    tok = jnp.arange(min(ts, MXU_TILE), dtype=jnp.int32)
    same = (tok[:, None] // CHUNK) == (tok[None, :] // CHUNK)
    consts = dict(
        wr=jnp.concatenate([wr_hi, wr_lo], axis=0),
        br=b_router.reshape(N_EXPERTS, 1),
        ubd=(same & (tok[:, None] <= tok[None, :])).astype(BF16),
        ebd=same.astype(BF16),
    )

    w1b = _cast_pad(jnp.swapaxes(w1, 2, 3), D_EXPERT_PAD, d)
    w3b = _cast_pad(jnp.swapaxes(w3, 2, 3), D_EXPERT_PAD, d)
    w2b = _cast_pad(w2, D_EXPERT_PAD, d)

    for l in range(depth):
        lw = _layer_weights(l, g_pre_mix, g_post_mix, g_pre_ffn, g_post_ffn, w_in, w_dw, b_dw,
                            conv_ln_g, conv_ln_b, w_gate_up, b_gate, gla_norm_g, w_out)
        x1, h2p, route = _mix(x, mod[l], lw, consts, ts)
        cls = route[:, 0, :].reshape(t).astype(jnp.int32)
        pos, rowtok, te1, te2, tvalid = _plan(cls, tm)
        z = _moe(h2p.reshape(t, d + 2 * LANES), rowtok, te1, te2, tvalid, w1b, w3b, w2b,
                 lw["gpostffn"], l, tm)
        g2 = mod[l][:, 5:6, :]
        x = _combine(x1.reshape(t, d), g2, z, pos, s, tc).reshape(b, s, d)
    return x
```

```python
import functools

import jax
import jax.numpy as jnp
from jax import lax
from jax.experimental import pallas as pl
from jax.experimental.pallas import tpu as pltpu

CHUNK = 64
CONV_WIDTH = 31
D_CONV = 512
GLA_HEADS = 4
GLA_DV = 128
GLA_DK = 64
D_GLA_K = GLA_HEADS * GLA_DK
D_GLA_V = GLA_HEADS * GLA_DV
GATE_RANK = 16
GATE_TAU = 16.0
N_EXPERTS = 16
N_GROUPS = 4
EXPERTS_PER_GROUP = 4
PAIRS_PER_GROUP = 6
N_CLASSES = N_GROUPS * PAIRS_PER_GROUP
D_EXPERT = 704
D_EXPERT_PAD = 768
EPS = 1e-6

SUBLANES = 8
LANES = 128
MXU_TILE = 256
CONV_PAD = 32
CONV_ROWS = 32
CONV_LANES = 256
SEQ_TILE = 512
MOE_TILE = 256
COMBINE_TILE = 256
DISPATCH_ROWS = 256
DMA_QUEUES = 2
GATHER_SLOTS = 3

F32 = jnp.float32
BF16 = jnp.bfloat16
NT_DIMS = (((1,), (1,)), ((), ()))
TN_DIMS = (((0,), (0,)), ((), ()))


def _sigmoid(x):
    return 1.0 / (1.0 + jnp.exp(-x))


def _rms(x, g):
    ms = jnp.mean(x * x, axis=-1, keepdims=True)
    return x * lax.rsqrt(ms + EPS) * g


def _split_bf16(x):
    hi = x.astype(BF16)
    lo = (x - hi.astype(F32)).astype(BF16)
    return hi, lo


def _mod_kernel(c_ref, w_ref, b_ref, o_ref):
    c = c_ref[...]
    ca = c * _sigmoid(c)
    a_hi, a_lo = _split_bf16(ca)
    w_hi, w_lo = _split_bf16(w_ref[0])
    acc = jnp.dot(a_hi, w_hi, preferred_element_type=F32)
    acc += jnp.dot(a_lo, w_hi, preferred_element_type=F32)
    acc += jnp.dot(a_hi, w_lo, preferred_element_type=F32)
    o_ref[0] = acc + b_ref[0]


def _modulation(c, w_ada, b_ada):
    depth, d, d6 = w_ada.shape
    b = c.shape[0]
    nblk = d6 // d
    return pl.pallas_call(
        _mod_kernel,
        grid=(depth, nblk),
        in_specs=[
            pl.BlockSpec((b, d), lambda l, n: (0, 0)),
            pl.BlockSpec((1, d, d), lambda l, n: (l, 0, n)),
            pl.BlockSpec((1, 1, d), lambda l, n: (l, 0, n)),
        ],
        out_specs=pl.BlockSpec((1, b, d), lambda l, n: (l, 0, n)),
        out_shape=jax.ShapeDtypeStruct((depth, b, d6), F32),
        compiler_params=pltpu.CompilerParams(
            dimension_semantics=("arbitrary", "arbitrary"),
            vmem_limit_bytes=32 * 1024 * 1024),
        name="adaln_mod",
    )(c, w_ada, b_ada.reshape(depth, 1, d6))


def _cast_pad_kernel(w_ref, o_ref):
    rows, cols = w_ref.shape[2:]
    prow, pcol = o_ref.shape[2:]
    o_ref[0, 0, 0:rows, 0:cols] = w_ref[0, 0].astype(BF16)
    if pcol > cols:
        o_ref[0, 0, :, cols:] = jnp.zeros((prow, pcol - cols), BF16)
    if prow > rows:
        o_ref[0, 0, rows:, 0:cols] = jnp.zeros((prow - rows, cols), BF16)


def _cast_pad(w, prow, pcol):
    depth, ne, rows, cols = w.shape
    return pl.pallas_call(
        _cast_pad_kernel,
        grid=(depth, ne),
        in_specs=[pl.BlockSpec((1, 1, rows, cols), lambda l, e: (l, e, 0, 0))],
        out_specs=pl.BlockSpec((1, 1, prow, pcol), lambda l, e: (l, e, 0, 0)),
        out_shape=jax.ShapeDtypeStruct((depth, ne, prow, pcol), BF16),
        compiler_params=pltpu.CompilerParams(
            dimension_semantics=("arbitrary", "arbitrary"),
            vmem_limit_bytes=32 * 1024 * 1024),
        name="cast_pad",
    )(w)


def _route(logits_t, br):
    m = jnp.max(logits_t, axis=0, keepdims=True)
    e = jnp.exp(logits_t - m)
    probs = e / jnp.sum(e, axis=0, keepdims=True)
    sel = probs + br
    one = jnp.ones_like(m)
    zero = jnp.zeros_like(m)
    picked, gscore = [], []
    for g in range(N_GROUPS):
        rows = [sel[g * EXPERTS_PER_GROUP + k:g * EXPERTS_PER_GROUP + k + 1, :]
                for k in range(EXPERTS_PER_GROUP)]
        score = zero
        for k in range(EXPERTS_PER_GROUP):
            rank = zero
            for k2 in range(EXPERTS_PER_GROUP):
                if k2 == k:
                    continue
                ahead = (rows[k2] >= rows[k]) if k2 < k else (rows[k2] > rows[k])
                rank = rank + jnp.where(ahead, one, zero)
            pk = jnp.where(rank < 2.0, one, zero)
            picked.append(pk)
            score = score + pk * rows[k]
        gscore.append(score)
    best = gscore[0]
    for g in range(1, N_GROUPS):
        best = jnp.maximum(best, gscore[g])
    taken = zero
    flag = [zero] * EXPERTS_PER_GROUP
    prob = [zero] * EXPERTS_PER_GROUP
    gbase = zero
    for g in range(N_GROUPS):
        isb = jnp.where(gscore[g] == best, one, zero) * (one - taken)
        taken = taken + isb
        gbase = gbase + isb * float(g * PAIRS_PER_GROUP)
        for k in range(EXPERTS_PER_GROUP):
            ei = g * EXPERTS_PER_GROUP + k
            flag[k] = flag[k] + isb * picked[ei]
            prob[k] = prob[k] + isb * probs[ei:ei + 1, :]
    f0, f1, f2, f3 = flag
    pair = f0 * (f2 * 1.0 + f3 * 2.0) + (one - f0) * (f1 * (f2 * 3.0 + f3 * 4.0) + (one - f1) * 5.0)
    cls = gbase + pair
    wa_raw = f0 * prob[0] + (one - f0) * (f1 * prob[1] + (one - f1) * prob[2])
    tot = f0 * prob[0] + f1 * prob[1] + f2 * prob[2] + f3 * prob[3]
    wa = wa_raw / tot
    wb = (tot - wa_raw) / tot
    return cls, wa, wb


def _mix_kernel(x_ref, mod_ref, gpre_ref, gpost_ref, gffn_ref, wnat_ref, wt_ref, wdw_ref, bdw_ref,
                lng_ref, lnb_ref, wup_ref, bup_ref, gnorm_ref, wout_ref, wr_ref, br_ref,
                ubd_ref, ebd_ref,
                x1_ref, h2p_ref, route_ref,
                ubuf, ush, cbuf, st_ref, sbd_ref, ycat_ref, *, ts):
    j = pl.program_id(1)
    nch = ts // CHUNK

    @pl.when(j == 0)
    def _():
        ubuf[0:CONV_PAD, :] = jnp.zeros((CONV_PAD, D_CONV), F32)
        st_ref[...] = jnp.zeros_like(st_ref)
        sbd_ref[...] = jnp.zeros_like(sbd_ref)

    @pl.when(j > 0)
    def _():
        ubuf[0:CONV_PAD, :] = ubuf[ts:ts + CONV_PAD, :]

    x = x_ref[0]
    sh1 = mod_ref[0, 0:1, :]
    sc1 = mod_ref[0, 1:2, :]
    g1 = mod_ref[0, 2:3, :]
    sh2 = mod_ref[0, 3:4, :]
    sc2 = mod_ref[0, 4:5, :]

    h = _rms(x, gpre_ref[...] * (1.0 + sc1)) + sh1
    hb = h.astype(BF16)

    cvg = jnp.dot(hb, wnat_ref[:, 0:2 * D_CONV], preferred_element_type=F32)
    ubuf[CONV_PAD:CONV_PAD + ts, :] = cvg[:, 0:D_CONV] * _sigmoid(cvg[:, D_CONV:2 * D_CONV])

    for r in range(1, SUBLANES):
        ush[r - 1] = ubuf[r:r + ts + CONV_PAD - SUBLANES, :]

    groups = CONV_ROWS // SUBLANES
    first_off = CONV_PAD - (CONV_WIDTH - 1)

    def conv_block(rb, carry, c0):
        r0 = rb * CONV_ROWS
        cs = slice(c0, c0 + CONV_LANES)
        acc = [jnp.broadcast_to(bdw_ref[:, cs], (SUBLANES, CONV_LANES)) for _ in range(groups)]
        for r in range(SUBLANES):
            taps = [tap for tap in range(CONV_WIDTH) if (first_off + tap) % SUBLANES == r]
            shifts = [(first_off + tap) // SUBLANES for tap in taps]
            src = ubuf if r == 0 else ush.at[r - 1]
            blk = {m: src[pl.ds(r0 + m * SUBLANES, SUBLANES), cs]
                   for m in range(min(shifts), max(shifts) + groups)}
            for tap, a in zip(taps, shifts):
                w8 = wdw_ref[tap * SUBLANES:(tap + 1) * SUBLANES, cs]
                for g in range(groups):
                    acc[g] = acc[g] + w8 * blk[a + g]
        cbuf[pl.ds(r0, CONV_ROWS), cs] = jnp.concatenate(acc, axis=0)
        return carry

    for c0 in range(0, D_CONV, CONV_LANES):
        for rb in range(ts // CONV_ROWS):
            conv_block(rb, 0, c0)
    cv = cbuf[...]
    mu = jnp.mean(cv, axis=-1, keepdims=True)
    cen = cv - mu
    var = jnp.mean(cen * cen, axis=-1, keepdims=True)
    yn = cen * lax.rsqrt(var + EPS) * lng_ref[...] + lnb_ref[...]
    ycat_ref[:, 0:D_CONV] = (yn * _sigmoid(yn)).astype(BF16)

    qvo = jnp.dot(hb, wnat_ref[:, 2 * D_CONV:], preferred_element_type=F32)
    q = (qvo[:, 0:D_GLA_K] * (GLA_DK ** -0.5)).astype(BF16)
    v = qvo[:, D_GLA_K:D_GLA_K + D_GLA_V].astype(BF16)
    og = qvo[:, D_GLA_K + D_GLA_V:]
    tt = lax.dot_general(wt_ref[...], hb, NT_DIMS, preferred_element_type=F32)
    k_t = tt[0:D_GLA_K, :]
    lr_t = tt[D_GLA_K:, :].astype(BF16)
    z_t = jnp.dot(wup_ref[...], lr_t, preferred_element_type=F32) + bup_ref[...]
    la_t = (jnp.minimum(z_t, 0.0) - jnp.log(1.0 + jnp.exp(-jnp.abs(z_t)))) * (1.0 / GATE_TAU)
    la_b = la_t.astype(BF16)
    slab = ubd_ref.shape[0]
    bcum = jnp.concatenate(
        [jnp.dot(la_b[:, s0:s0 + slab], ubd_ref[...], preferred_element_type=F32)
         for s0 in range(0, ts, slab)], axis=1)
    bend = jnp.concatenate(
        [jnp.dot(la_b[:, s0:s0 + slab], ebd_ref[...], preferred_element_type=F32)
         for s0 in range(0, ts, slab)], axis=1)
    kd_t = (k_t * jnp.exp(bend - bcum)).astype(BF16)
    dec_t = jnp.exp(bend)

    st = [st_ref[hh] for hh in range(GLA_HEADS)]
    gn = gnorm_ref[...]
    for c in range(nch):
        lo, hi_ = c * CHUNK, (c + 1) * CHUNK
        kd_c = kd_t[:, lo:hi_]
        v_c = v[lo:hi_, :]
        for hh in range(GLA_HEADS):
            ks, ke = hh * GLA_DK, (hh + 1) * GLA_DK
            vs, ve = hh * GLA_DV, (hh + 1) * GLA_DV
            upd = jnp.dot(kd_c[ks:ke, :], v_c[:, vs:ve], preferred_element_type=F32)
            dec = jnp.broadcast_to(dec_t[ks:ke, lo:lo + 1], (GLA_DK, GLA_DV))
            st[hh] = dec * st[hh] + upd
            sbd_ref[ks:ke, vs:ve] = st[hh].astype(BF16)
        o_c = jnp.dot(q[lo:hi_, :], sbd_ref[...], preferred_element_type=F32)
        og_c = og[lo:hi_, :]
        for hh in range(GLA_HEADS):
            vs, ve = hh * GLA_DV, (hh + 1) * GLA_DV
            on = _rms(o_c[:, vs:ve], gn)
            gate = og_c[:, vs:ve]
            ycat_ref[lo:hi_, D_CONV + vs:D_CONV + ve] = (on * (gate * _sigmoid(gate))).astype(BF16)
    for hh in range(GLA_HEADS):
        st_ref[hh] = st[hh]

    y = jnp.dot(ycat_ref[...], wout_ref[...], preferred_element_type=F32)
    x1 = x + g1 * _rms(y, gpost_ref[...])
    x1_ref[0] = x1
    h2 = _rms(x1, gffn_ref[...] * (1.0 + sc2)) + sh2
    d = h2.shape[1]
    h2p_ref[0, :, 0:d] = h2
    h2_hi, h2_lo = _split_bf16(h2)
    lg2 = lax.dot_general(wr_ref[...], h2_hi, NT_DIMS, preferred_element_type=F32)
    lg1 = lax.dot_general(wr_ref[0:N_EXPERTS, :], h2_lo, NT_DIMS, preferred_element_type=F32)
    logits_t = lg2[0:N_EXPERTS, :] + lg2[N_EXPERTS:, :] + lg1
    cls, wa, wb = _route(logits_t, br_ref[...])
    route_ref[0] = jnp.concatenate([cls, jnp.zeros((SUBLANES - 1, ts), F32)], axis=0)
    h2p_ref[0, :, d:d + LANES] = jnp.transpose(jnp.broadcast_to(wa, (LANES, ts)))
    h2p_ref[0, :, d + LANES:] = jnp.transpose(jnp.broadcast_to(wb, (LANES, ts)))


def _mix(x, mod, lw, consts, ts):
    b, s, d = x.shape
    nst = s // ts
    full = lambda shape: pl.BlockSpec(shape, lambda bi, ji: (0,) * len(shape))
    in_specs = [
        pl.BlockSpec((1, ts, d), lambda bi, ji: (bi, ji, 0)),
        pl.BlockSpec((1, 6, d), lambda bi, ji: (bi, 0, 0)),
        full((1, d)), full((1, d)), full((1, d)),
        full(lw["wnat"].shape), full(lw["wt"].shape),
        full((CONV_WIDTH * SUBLANES, D_CONV)), full((1, D_CONV)), full((1, D_CONV)), full((1, D_CONV)),
        full((D_GLA_K, GATE_RANK)), full((D_GLA_K, 1)), full((1, GLA_DV)),
        full((d, d)), full((2 * N_EXPERTS, d)), full((N_EXPERTS, 1)),
        full(consts["ubd"].shape), full(consts["ebd"].shape),
    ]
    out_specs = [
        pl.BlockSpec((1, ts, d), lambda bi, ji: (bi, ji, 0)),
        pl.BlockSpec((1, ts, d + 2 * LANES), lambda bi, ji: (bi, ji, 0)),
        pl.BlockSpec((1, 8, ts), lambda bi, ji: (bi * nst + ji, 0, 0)),
    ]
    out_shape = [
        jax.ShapeDtypeStruct((b, s, d), F32),
        jax.ShapeDtypeStruct((b, s, d + 2 * LANES), F32),
        jax.ShapeDtypeStruct((b * nst, 8, ts), F32),
    ]
    scratch = [
        pltpu.VMEM((CONV_PAD + ts, D_CONV), F32),
        pltpu.VMEM((SUBLANES - 1, CONV_PAD + ts - SUBLANES, D_CONV), F32),
        pltpu.VMEM((ts, D_CONV), F32),
        pltpu.VMEM((GLA_HEADS, GLA_DK, GLA_DV), F32),
        pltpu.VMEM((D_GLA_K, D_GLA_V), BF16),
        pltpu.VMEM((ts, d), BF16),
    ]
    return pl.pallas_call(
        functools.partial(_mix_kernel, ts=ts),
        grid=(b, nst),
        in_specs=in_specs,
        out_specs=out_specs,
        out_shape=out_shape,
        scratch_shapes=scratch,
        compiler_params=pltpu.CompilerParams(
            dimension_semantics=("arbitrary", "arbitrary"),
            vmem_limit_bytes=56 * 1024 * 1024),
        name="mix",
    )(x, mod, lw["gpre"], lw["gpost"], lw["gffn"], lw["wnat"], lw["wt"], lw["wdw"], lw["bdw"],
      lw["lng"], lw["lnb"], lw["wup"], lw["bup"], lw["gnorm"], lw["wout"], consts["wr"],
      consts["br"], consts["ubd"], consts["ebd"])


def _row_copy(idx_ref, base, r, src_hbm, dst_vmem, sem):
    tok = idx_ref[base + r]
    return pltpu.make_async_copy(src_hbm.at[pl.ds(tok, 1), :], dst_vmem.at[pl.ds(r, 1), :], sem)


def _row_gather_start(idx_ref, base, src_hbm, dst_vmem, sem, n_rows, unrolled):
    if unrolled:
        for r in range(n_rows):
            _row_copy(idx_ref, base, r, src_hbm, dst_vmem, sem).start(priority=r % DMA_QUEUES)
    else:
        def body(r, carry):
            _row_copy(idx_ref, base, r, src_hbm, dst_vmem, sem).start()
            return carry
        lax.fori_loop(0, n_rows, body, 0)


def _row_gather_wait(src_hbm, dst_vmem, sem, n_rows):
    pltpu.make_async_copy(src_hbm.at[pl.ds(0, n_rows), :], dst_vmem, sem).wait()


def _gather_pipeline(idx_ref, src_hbm, buf, sems, n_rows):
    i = pl.program_id(0)
    n = pl.num_programs(0)

    @pl.when(i == 0)
    def _():
        for k in range(GATHER_SLOTS - 1):
            @pl.when(k < n)
            def _():
                _row_gather_start(idx_ref, k * n_rows, src_hbm, buf.at[k], sems.at[k], n_rows, unrolled=False)

    ahead = i + GATHER_SLOTS - 1

    @pl.when(ahead < n)
    def _():
        aslot = lax.rem(ahead, GATHER_SLOTS)
        _row_gather_start(idx_ref, ahead * n_rows, src_hbm, buf.at[aslot], sems.at[aslot], n_rows,
                          unrolled=True)

    slot = lax.rem(i, GATHER_SLOTS)
    _row_gather_wait(src_hbm, buf.at[slot], sems.at[slot], n_rows)
    return slot


def _moe_kernel(te1_ref, te2_ref, tvalid_ref,
                hs_ref, w1a_ref, w3a_ref, w2a_ref, w1b_ref, w3b_ref, w2b_ref, gpost_ref,
                z_ref):
    i = pl.program_id(0)
    d = z_ref.shape[1]

    @pl.when(tvalid_ref[i] == 0)
    def _():
        z_ref[...] = jnp.zeros_like(z_ref)

    @pl.when(tvalid_ref[i] != 0)
    def _():
        h = hs_ref[:, 0:d].astype(BF16)
        y = None
        for e, (w1, w3, w2) in enumerate(((w1a_ref, w3a_ref, w2a_ref), (w1b_ref, w3b_ref, w2b_ref))):
            g_t = lax.dot_general(w1[0, 0], h, NT_DIMS, preferred_element_type=F32)
            u_t = lax.dot_general(w3[0, 0], h, NT_DIMS, preferred_element_type=F32)
            he_t = ((g_t * _sigmoid(g_t)) * u_t).astype(BF16)
            ye = lax.dot_general(he_t, w2[0, 0], TN_DIMS, preferred_element_type=F32)
            wcol = hs_ref[:, d + e * LANES:d + (e + 1) * LANES]
            ye = ye * jnp.concatenate([wcol] * (d // LANES), axis=1)
            y = ye if y is None else y + ye
        z_ref[...] = _rms(y, gpost_ref[...])


def _moe(hs, te1, te2, tvalid, w1, w3, w2, gpost, layer, tm):
    dx = hs.shape[1]
    d = dx - 2 * LANES
    nt = te1.shape[0]
    wspec_a = lambda shape: pl.BlockSpec((1, 1) + shape, lambda i, e1, e2, tv: (layer, e1[i], 0, 0))
    wspec_b = lambda shape: pl.BlockSpec((1, 1) + shape, lambda i, e1, e2, tv: (layer, e2[i], 0, 0))
    grid_spec = pltpu.PrefetchScalarGridSpec(
        num_scalar_prefetch=3,
        grid=(nt,),
        in_specs=[
            pl.BlockSpec((tm, dx), lambda i, e1, e2, tv: (i, 0)),
            wspec_a((D_EXPERT_PAD, d)), wspec_a((D_EXPERT_PAD, d)), wspec_a((D_EXPERT_PAD, d)),
            wspec_b((D_EXPERT_PAD, d)), wspec_b((D_EXPERT_PAD, d)), wspec_b((D_EXPERT_PAD, d)),
            pl.BlockSpec((1, d), lambda i, e1, e2, tv: (0, 0)),
        ],
        out_specs=pl.BlockSpec((tm, d), lambda i, e1, e2, tv: (i, 0)),
    )
    return pl.pallas_call(
        _moe_kernel,
        grid_spec=grid_spec,
        out_shape=jax.ShapeDtypeStruct((nt * tm, d), F32),
        compiler_params=pltpu.CompilerParams(
            dimension_semantics=("arbitrary",),
            vmem_limit_bytes=48 * 1024 * 1024),
        name="moe",
    )(te1, te2, tvalid, hs, w1, w3, w2, w1, w3, w2, gpost)


def _combine_kernel(pos_ref, x1_ref, g2_ref, z_hbm, o_ref, zg, sems, *, tc):
    slot = _gather_pipeline(pos_ref, z_hbm, zg, sems, tc)
    o_ref[...] = x1_ref[...] + g2_ref[0] * zg[slot]


def _combine(x1, g2, z, pos, seq_len, tc):
    t, d = x1.shape
    per_seq = seq_len // tc
    grid_spec = pltpu.PrefetchScalarGridSpec(
        num_scalar_prefetch=1,
        grid=(t // tc,),
        in_specs=[
            pl.BlockSpec((tc, d), lambda i, pos: (i, 0)),
            pl.BlockSpec((1, 1, d), lambda i, pos: (i // per_seq, 0, 0)),
            pl.BlockSpec(memory_space=pl.ANY),
        ],
        out_specs=pl.BlockSpec((tc, d), lambda i, pos: (i, 0)),
        scratch_shapes=[
            pltpu.VMEM((GATHER_SLOTS, tc, d), F32),
            pltpu.SemaphoreType.DMA((GATHER_SLOTS,)),
        ],
    )
    return pl.pallas_call(
        functools.partial(_combine_kernel, tc=tc),
        grid_spec=grid_spec,
        out_shape=jax.ShapeDtypeStruct((t, d), F32),
        compiler_params=pltpu.CompilerParams(
            dimension_semantics=("arbitrary",),
            vmem_limit_bytes=32 * 1024 * 1024),
        name="combine",
    )(pos, x1, g2, z)


def _class_tables():
    e1, e2 = [], []
    for g in range(N_GROUPS):
        for a in range(EXPERTS_PER_GROUP):
            for b in range(a + 1, EXPERTS_PER_GROUP):
                e1.append(g * EXPERTS_PER_GROUP + a)
                e2.append(g * EXPERTS_PER_GROUP + b)
    return jnp.asarray(e1, jnp.int32), jnp.asarray(e2, jnp.int32)


def _dispatch_kernel(pos_ref, pad_start_ref, pad_len_ref, used_ref, h_ref, zero_ref, hs_hbm, row_sem, pad_sem,
                     *, rows, tm):
    i = pl.program_id(0)
    n = pl.num_programs(0)
    n_tiles = hs_hbm.shape[0] // tm

    def for_each_pad_row(fn):
        for c in range(N_CLASSES):
            def body(j, carry, c=c):
                fn(pltpu.make_async_copy(zero_ref.at[pl.ds(0, 1), :],
                                         hs_hbm.at[pl.ds(pad_start_ref[c] + j, 1), :], pad_sem))
                return carry
            lax.fori_loop(0, pad_len_ref[c], body, 0)

        def tile_body(k, carry):
            fn(pltpu.make_async_copy(zero_ref, hs_hbm.at[pl.ds(pl.multiple_of(k * tm, tm), tm), :], pad_sem))
            return carry
        lax.fori_loop(used_ref[0], n_tiles, tile_body, 0)

    @pl.when(i == 0)
    def _():
        for_each_pad_row(lambda cp: cp.start())

    base = i * rows
    for u in range(rows):
        pltpu.make_async_copy(h_ref.at[pl.ds(u, 1), :], hs_hbm.at[pl.ds(pos_ref[base + u], 1), :],
                              row_sem).start()
    pltpu.make_async_copy(h_ref, hs_hbm.at[pl.ds(0, rows), :], row_sem).wait()

    @pl.when(i == n - 1)
    def _():
        for_each_pad_row(lambda cp: cp.wait())


def _dispatch(h2p, pos, pad_start, pad_len, used_tiles, n_rows, rows, tm):
    t, dx = h2p.shape
    grid_spec = pltpu.PrefetchScalarGridSpec(
        num_scalar_prefetch=4,
        grid=(t // rows,),
        in_specs=[pl.BlockSpec((rows, dx), lambda i, *_: (i, 0)),
                  pl.BlockSpec((tm, dx), lambda i, *_: (0, 0))],
        out_specs=pl.BlockSpec(memory_space=pl.ANY),
        scratch_shapes=[pltpu.SemaphoreType.DMA, pltpu.SemaphoreType.DMA],
    )
    return pl.pallas_call(
        functools.partial(_dispatch_kernel, rows=rows, tm=tm),
        grid_spec=grid_spec,
        out_shape=jax.ShapeDtypeStruct((n_rows, dx), F32),
        compiler_params=pltpu.CompilerParams(dimension_semantics=("arbitrary",)),
        name="dispatch",
    )(pos, pad_start, pad_len, used_tiles, h2p, jnp.zeros((tm, dx), F32))


def _plan(cls, tm):
    t = cls.shape[0]
    nt = t // tm + N_CLASSES
    onehot = (cls[:, None] == jnp.arange(N_CLASSES, dtype=jnp.int32)[None, :]).astype(jnp.int32)
    csum = jnp.cumsum(onehot, axis=0)
    counts = csum[-1]
    rank = jnp.take_along_axis(csum, cls[:, None], axis=1)[:, 0] - 1
    tiles = (counts + tm - 1) // tm
    tile_end = jnp.cumsum(tiles)
    tile_start = tile_end - tiles
    pos = tile_start[cls] * tm + rank
    tile_ids = jnp.arange(nt, dtype=jnp.int32)
    tile_cls = jnp.sum((tile_end[None, :] <= tile_ids[:, None]).astype(jnp.int32), axis=1)
    tvalid = (tile_ids < tile_end[-1]).astype(jnp.int32)
    last_cls = jnp.max(jnp.where(counts > 0, jnp.arange(N_CLASSES, dtype=jnp.int32), 0))
    tile_cls = jnp.where(tvalid == 1, jnp.minimum(tile_cls, N_CLASSES - 1), last_cls)
    ce1, ce2 = _class_tables()
    pad_start = (tile_start * tm + counts).astype(jnp.int32)
    pad_len = (tiles * tm - counts).astype(jnp.int32)
    used_tiles = tile_end[-1:].astype(jnp.int32)
    return pos.astype(jnp.int32), pad_start, pad_len, used_tiles, ce1[tile_cls], ce2[tile_cls], tvalid


def _layer_weights(l, g_pre_mix, g_post_mix, g_pre_ffn, g_post_ffn, w_in, w_dw, b_dw, conv_ln_g,
                   conv_ln_b, w_gate_up, b_gate, gla_norm_g, w_out):
    d = w_in.shape[1]
    wi = w_in[l]
    o_q = 2 * D_CONV
    o_k = o_q + D_GLA_K
    o_v = o_k + D_GLA_K
    o_g = o_v + D_GLA_V
    o_lr = o_g + D_GLA_V
    wnat = jnp.concatenate([wi[:, 0:o_q], wi[:, o_q:o_k], wi[:, o_v:o_g], wi[:, o_g:o_lr]], axis=1)
    wt = jnp.concatenate([wi[:, o_k:o_v], wi[:, o_lr:]], axis=1).T
    return dict(
        gpre=g_pre_mix[l].reshape(1, d), gpost=g_post_mix[l].reshape(1, d),
        gffn=g_pre_ffn[l].reshape(1, d), gpostffn=g_post_ffn[l].reshape(1, d),
        wnat=wnat.astype(BF16), wt=wt.astype(BF16),
        wdw=jnp.repeat(w_dw[l], SUBLANES, axis=0), bdw=b_dw[l].reshape(1, D_CONV),
        lng=conv_ln_g[l].reshape(1, D_CONV), lnb=conv_ln_b[l].reshape(1, D_CONV),
        wup=w_gate_up[l].T.astype(BF16), bup=b_gate[l].reshape(D_GLA_K, 1),
        gnorm=gla_norm_g[l].reshape(1, GLA_DV),
        wout=w_out[l].astype(BF16),
    )


def kernel(x, c, w_ada, b_ada, g_pre_mix, g_post_mix, g_pre_ffn, g_post_ffn, w_in, w_dw, b_dw,
           conv_ln_g, conv_ln_b, w_gate_up, b_gate, gla_norm_g, w_out, w_router, b_router, w1, w3, w2):
    b, s, d = x.shape
    depth = w_ada.shape[0]
    t = b * s
    ts = min(SEQ_TILE, s)
    tm = min(MOE_TILE, t)
    tc = min(COMBINE_TILE, s)
    assert s % ts == 0 and ts % CHUNK == 0 and t % tm == 0 and s % tc == 0

    mod = _modulation(c, w_ada, b_ada).reshape(depth, b, 6, d)

    wr_hi, wr_lo = _split_bf16(w_router.T)
    tok = jnp.arange(min(ts, MXU_TILE), dtype=jnp.int32)
    same = (tok[:, None] // CHUNK) == (tok[None, :] // CHUNK)
    consts = dict(
        wr=jnp.concatenate([wr_hi, wr_lo], axis=0),
        br=b_router.reshape(N_EXPERTS, 1),
        ubd=(same & (tok[:, None] <= tok[None, :])).astype(BF16),
        ebd=same.astype(BF16),
    )

    w1b = _cast_pad(jnp.swapaxes(w1, 2, 3), D_EXPERT_PAD, d)
    w3b = _cast_pad(jnp.swapaxes(w3, 2, 3), D_EXPERT_PAD, d)
    w2b = _cast_pad(w2, D_EXPERT_PAD, d)

    for l in range(depth):
        lw = _layer_weights(l, g_pre_mix, g_post_mix, g_pre_ffn, g_post_ffn, w_in, w_dw, b_dw,
                            conv_ln_g, conv_ln_b, w_gate_up, b_gate, gla_norm_g, w_out)
        x1, h2p, route = _mix(x, mod[l], lw, consts, ts)
        cls = route[:, 0, :].reshape(t).astype(jnp.int32)
        pos, pad_start, pad_len, used_tiles, te1, te2, tvalid = _plan(cls, tm)
        hs = _dispatch(h2p.reshape(t, d + 2 * LANES), pos, pad_start, pad_len, used_tiles,
                       te1.shape[0] * tm, min(DISPATCH_ROWS, t), tm)
        z = _moe(hs, te1, te2, tvalid, w1b, w3b, w2b, lw["gpostffn"], l, tm)
        g2 = mod[l][:, 5:6, :]
        x = _combine(x1.reshape(t, d), g2, z, pos, s, tc).reshape(b, s, d)
    return x
```

```python
import functools

import jax
import jax.numpy as jnp
from jax import lax
from jax.experimental import pallas as pl
from jax.experimental.pallas import tpu as pltpu

CHUNK = 64
CONV_WIDTH = 31
D_CONV = 512
GLA_HEADS = 4
GLA_DV = 128
GLA_DK = 64
D_GLA_K = GLA_HEADS * GLA_DK
D_GLA_V = GLA_HEADS * GLA_DV
GATE_RANK = 16
GATE_TAU = 16.0
N_EXPERTS = 16
N_GROUPS = 4
EXPERTS_PER_GROUP = 4
PAIRS_PER_GROUP = 6
N_CLASSES = N_GROUPS * PAIRS_PER_GROUP
D_EXPERT = 704
D_EXPERT_PAD = 768
EPS = 1e-6

SUBLANES = 8
LANES = 128
MXU_TILE = 256
CONV_PAD = 32
CONV_ROWS = 32
CONV_LANES = 256
SEQ_TILE = 512
MOE_TILE = 256
COMBINE_TILE = 256
DISPATCH_ROWS = 256
DISPATCH_AHEAD = 2
DISPATCH_SLOTS = 2 * DISPATCH_AHEAD
DMA_QUEUES = 2
GATHER_SLOTS = 3

F32 = jnp.float32
BF16 = jnp.bfloat16
NT_DIMS = (((1,), (1,)), ((), ()))
TN_DIMS = (((0,), (0,)), ((), ()))


def _sigmoid(x):
    return 1.0 / (1.0 + jnp.exp(-x))


def _rms(x, g):
    ms = jnp.mean(x * x, axis=-1, keepdims=True)
    return x * lax.rsqrt(ms + EPS) * g


def _split_bf16(x):
    hi = x.astype(BF16)
    lo = (x - hi.astype(F32)).astype(BF16)
    return hi, lo


def _mod_kernel(c_ref, w_ref, b_ref, o_ref):
    c = c_ref[...]
    ca = c * _sigmoid(c)
    a_hi, a_lo = _split_bf16(ca)
    w_hi, w_lo = _split_bf16(w_ref[0])
    acc = jnp.dot(a_hi, w_hi, preferred_element_type=F32)
    acc += jnp.dot(a_lo, w_hi, preferred_element_type=F32)
    acc += jnp.dot(a_hi, w_lo, preferred_element_type=F32)
    o_ref[0] = acc + b_ref[0]


def _modulation(c, w_ada, b_ada):
    depth, d, d6 = w_ada.shape
    b = c.shape[0]
    nblk = d6 // d
    return pl.pallas_call(
        _mod_kernel,
        grid=(depth, nblk),
        in_specs=[
            pl.BlockSpec((b, d), lambda l, n: (0, 0)),
            pl.BlockSpec((1, d, d), lambda l, n: (l, 0, n)),
            pl.BlockSpec((1, 1, d), lambda l, n: (l, 0, n)),
        ],
        out_specs=pl.BlockSpec((1, b, d), lambda l, n: (l, 0, n)),
        out_shape=jax.ShapeDtypeStruct((depth, b, d6), F32),
        compiler_params=pltpu.CompilerParams(
            dimension_semantics=("arbitrary", "arbitrary"),
            vmem_limit_bytes=32 * 1024 * 1024),
        name="adaln_mod",
    )(c, w_ada, b_ada.reshape(depth, 1, d6))


def _cast_pad_kernel(w_ref, o_ref):
    rows, cols = w_ref.shape[2:]
    prow, pcol = o_ref.shape[2:]
    o_ref[0, 0, 0:rows, 0:cols] = w_ref[0, 0].astype(BF16)
    if pcol > cols:
        o_ref[0, 0, :, cols:] = jnp.zeros((prow, pcol - cols), BF16)
    if prow > rows:
        o_ref[0, 0, rows:, 0:cols] = jnp.zeros((prow - rows, cols), BF16)


def _cast_pad(w, prow, pcol):
    depth, ne, rows, cols = w.shape
    return pl.pallas_call(
        _cast_pad_kernel,
        grid=(depth, ne),
        in_specs=[pl.BlockSpec((1, 1, rows, cols), lambda l, e: (l, e, 0, 0))],
        out_specs=pl.BlockSpec((1, 1, prow, pcol), lambda l, e: (l, e, 0, 0)),
        out_shape=jax.ShapeDtypeStruct((depth, ne, prow, pcol), BF16),
        compiler_params=pltpu.CompilerParams(
            dimension_semantics=("arbitrary", "arbitrary"),
            vmem_limit_bytes=32 * 1024 * 1024),
        name="cast_pad",
    )(w)


def _route(logits_t, br):
    m = jnp.max(logits_t, axis=0, keepdims=True)
    e = jnp.exp(logits_t - m)
    probs = e / jnp.sum(e, axis=0, keepdims=True)
    sel = probs + br
    one = jnp.ones_like(m)
    zero = jnp.zeros_like(m)
    picked, gscore = [], []
    for g in range(N_GROUPS):
        rows = [sel[g * EXPERTS_PER_GROUP + k:g * EXPERTS_PER_GROUP + k + 1, :]
                for k in range(EXPERTS_PER_GROUP)]
        score = zero
        for k in range(EXPERTS_PER_GROUP):
            rank = zero
            for k2 in range(EXPERTS_PER_GROUP):
                if k2 == k:
                    continue
                ahead = (rows[k2] >= rows[k]) if k2 < k else (rows[k2] > rows[k])
                rank = rank + jnp.where(ahead, one, zero)
            pk = jnp.where(rank < 2.0, one, zero)
            picked.append(pk)
            score = score + pk * rows[k]
        gscore.append(score)
    best = gscore[0]
    for g in range(1, N_GROUPS):
        best = jnp.maximum(best, gscore[g])
    taken = zero
    flag = [zero] * EXPERTS_PER_GROUP
    prob = [zero] * EXPERTS_PER_GROUP
    gbase = zero
    for g in range(N_GROUPS):
        isb = jnp.where(gscore[g] == best, one, zero) * (one - taken)
        taken = taken + isb
        gbase = gbase + isb * float(g * PAIRS_PER_GROUP)
        for k in range(EXPERTS_PER_GROUP):
            ei = g * EXPERTS_PER_GROUP + k
            flag[k] = flag[k] + isb * picked[ei]
            prob[k] = prob[k] + isb * probs[ei:ei + 1, :]
    f0, f1, f2, f3 = flag
    pair = f0 * (f2 * 1.0 + f3 * 2.0) + (one - f0) * (f1 * (f2 * 3.0 + f3 * 4.0) + (one - f1) * 5.0)
    cls = gbase + pair
    wa_raw = f0 * prob[0] + (one - f0) * (f1 * prob[1] + (one - f1) * prob[2])
    tot = f0 * prob[0] + f1 * prob[1] + f2 * prob[2] + f3 * prob[3]
    wa = wa_raw / tot
    wb = (tot - wa_raw) / tot
    return cls, wa, wb


def _mix_kernel(x_ref, mod_ref, gpre_ref, gpost_ref, gffn_ref, wnat_ref, wt_ref, wdw_ref, bdw_ref,
                lng_ref, lnb_ref, wup_ref, bup_ref, gnorm_ref, wout_ref, wr_ref, br_ref,
                ubd_ref, ebd_ref,
                x1_ref, h2p_ref, route_ref,
                ubuf, ush, cbuf, st_ref, sbd_ref, ycat_ref, *, ts):
    j = pl.program_id(1)
    nch = ts // CHUNK

    @pl.when(j == 0)
    def _():
        ubuf[0:CONV_PAD, :] = jnp.zeros((CONV_PAD, D_CONV), F32)
        st_ref[...] = jnp.zeros_like(st_ref)
        sbd_ref[...] = jnp.zeros_like(sbd_ref)

    @pl.when(j > 0)
    def _():
        ubuf[0:CONV_PAD, :] = ubuf[ts:ts + CONV_PAD, :]

    x = x_ref[0]
    sh1 = mod_ref[0, 0:1, :]
    sc1 = mod_ref[0, 1:2, :]
    g1 = mod_ref[0, 2:3, :]
    sh2 = mod_ref[0, 3:4, :]
    sc2 = mod_ref[0, 4:5, :]

    h = _rms(x, gpre_ref[...] * (1.0 + sc1)) + sh1
    hb = h.astype(BF16)

    cvg = jnp.dot(hb, wnat_ref[:, 0:2 * D_CONV], preferred_element_type=F32)
    ubuf[CONV_PAD:CONV_PAD + ts, :] = cvg[:, 0:D_CONV] * _sigmoid(cvg[:, D_CONV:2 * D_CONV])

    for r in range(1, SUBLANES):
        ush[r - 1] = ubuf[r:r + ts + CONV_PAD - SUBLANES, :]

    groups = CONV_ROWS // SUBLANES
    first_off = CONV_PAD - (CONV_WIDTH - 1)

    def conv_block(rb, carry, c0):
        r0 = rb * CONV_ROWS
        cs = slice(c0, c0 + CONV_LANES)
        acc = [jnp.broadcast_to(bdw_ref[:, cs], (SUBLANES, CONV_LANES)) for _ in range(groups)]
        for r in range(SUBLANES):
            taps = [tap for tap in range(CONV_WIDTH) if (first_off + tap) % SUBLANES == r]
            shifts = [(first_off + tap) // SUBLANES for tap in taps]
            src = ubuf if r == 0 else ush.at[r - 1]
            blk = {m: src[pl.ds(r0 + m * SUBLANES, SUBLANES), cs]
                   for m in range(min(shifts), max(shifts) + groups)}
            for tap, a in zip(taps, shifts):
                w8 = wdw_ref[tap * SUBLANES:(tap + 1) * SUBLANES, cs]
                for g in range(groups):
                    acc[g] = acc[g] + w8 * blk[a + g]
        cbuf[pl.ds(r0, CONV_ROWS), cs] = jnp.concatenate(acc, axis=0)
        return carry

    for c0 in range(0, D_CONV, CONV_LANES):
        for rb in range(ts // CONV_ROWS):
            conv_block(rb, 0, c0)
    cv = cbuf[...]
    mu = jnp.mean(cv, axis=-1, keepdims=True)
    cen = cv - mu
    var = jnp.mean(cen * cen, axis=-1, keepdims=True)
    yn = cen * lax.rsqrt(var + EPS) * lng_ref[...] + lnb_ref[...]
    ycat_ref[:, 0:D_CONV] = (yn * _sigmoid(yn)).astype(BF16)

    qvo = jnp.dot(hb, wnat_ref[:, 2 * D_CONV:], preferred_element_type=F32)
    q = (qvo[:, 0:D_GLA_K] * (GLA_DK ** -0.5)).astype(BF16)
    v = qvo[:, D_GLA_K:D_GLA_K + D_GLA_V].astype(BF16)
    og = qvo[:, D_GLA_K + D_GLA_V:]
    tt = lax.dot_general(wt_ref[...], hb, NT_DIMS, preferred_element_type=F32)
    k_t = tt[0:D_GLA_K, :]
    lr_t = tt[D_GLA_K:, :].astype(BF16)
    z_t = jnp.dot(wup_ref[...], lr_t, preferred_element_type=F32) + bup_ref[...]
    la_t = (jnp.minimum(z_t, 0.0) - jnp.log(1.0 + jnp.exp(-jnp.abs(z_t)))) * (1.0 / GATE_TAU)
    la_b = la_t.astype(BF16)
    slab = ubd_ref.shape[0]
    bcum = jnp.concatenate(
        [jnp.dot(la_b[:, s0:s0 + slab], ubd_ref[...], preferred_element_type=F32)
         for s0 in range(0, ts, slab)], axis=1)
    bend = jnp.concatenate(
        [jnp.dot(la_b[:, s0:s0 + slab], ebd_ref[...], preferred_element_type=F32)
         for s0 in range(0, ts, slab)], axis=1)
    kd_t = (k_t * jnp.exp(bend - bcum)).astype(BF16)
    dec_t = jnp.exp(bend)

    st = [st_ref[hh] for hh in range(GLA_HEADS)]
    gn = gnorm_ref[...]
    for c in range(nch):
        lo, hi_ = c * CHUNK, (c + 1) * CHUNK
        kd_c = kd_t[:, lo:hi_]
        v_c = v[lo:hi_, :]
        for hh in range(GLA_HEADS):
            ks, ke = hh * GLA_DK, (hh + 1) * GLA_DK
            vs, ve = hh * GLA_DV, (hh + 1) * GLA_DV
            upd = jnp.dot(kd_c[ks:ke, :], v_c[:, vs:ve], preferred_element_type=F32)
            dec = jnp.broadcast_to(dec_t[ks:ke, lo:lo + 1], (GLA_DK, GLA_DV))
            st[hh] = dec * st[hh] + upd
            sbd_ref[ks:ke, vs:ve] = st[hh].astype(BF16)
        o_c = jnp.dot(q[lo:hi_, :], sbd_ref[...], preferred_element_type=F32)
        og_c = og[lo:hi_, :]
        for hh in range(GLA_HEADS):
            vs, ve = hh * GLA_DV, (hh + 1) * GLA_DV
            on = _rms(o_c[:, vs:ve], gn)
            gate = og_c[:, vs:ve]
            ycat_ref[lo:hi_, D_CONV + vs:D_CONV + ve] = (on * (gate * _sigmoid(gate))).astype(BF16)
    for hh in range(GLA_HEADS):
        st_ref[hh] = st[hh]

    y = jnp.dot(ycat_ref[...], wout_ref[...], preferred_element_type=F32)
    x1 = x + g1 * _rms(y, gpost_ref[...])
    x1_ref[0] = x1
    h2 = _rms(x1, gffn_ref[...] * (1.0 + sc2)) + sh2
    d = h2.shape[1]
    h2p_ref[0, :, 0:d] = h2
    h2_hi, h2_lo = _split_bf16(h2)
    lg2 = lax.dot_general(wr_ref[...], h2_hi, NT_DIMS, preferred_element_type=F32)
    lg1 = lax.dot_general(wr_ref[0:N_EXPERTS, :], h2_lo, NT_DIMS, preferred_element_type=F32)
    logits_t = lg2[0:N_EXPERTS, :] + lg2[N_EXPERTS:, :] + lg1
    cls, wa, wb = _route(logits_t, br_ref[...])
    route_ref[0] = jnp.concatenate([cls, jnp.zeros((SUBLANES - 1, ts), F32)], axis=0)
    h2p_ref[0, :, d:d + LANES] = jnp.transpose(jnp.broadcast_to(wa, (LANES, ts)))
    h2p_ref[0, :, d + LANES:] = jnp.transpose(jnp.broadcast_to(wb, (LANES, ts)))


def _mix(x, mod, lw, consts, ts):
    b, s, d = x.shape
    nst = s // ts
    full = lambda shape: pl.BlockSpec(shape, lambda bi, ji: (0,) * len(shape))
    in_specs = [
        pl.BlockSpec((1, ts, d), lambda bi, ji: (bi, ji, 0)),
        pl.BlockSpec((1, 6, d), lambda bi, ji: (bi, 0, 0)),
        full((1, d)), full((1, d)), full((1, d)),
        full(lw["wnat"].shape), full(lw["wt"].shape),
        full((CONV_WIDTH * SUBLANES, D_CONV)), full((1, D_CONV)), full((1, D_CONV)), full((1, D_CONV)),
        full((D_GLA_K, GATE_RANK)), full((D_GLA_K, 1)), full((1, GLA_DV)),
        full((d, d)), full((2 * N_EXPERTS, d)), full((N_EXPERTS, 1)),
        full(consts["ubd"].shape), full(consts["ebd"].shape),
    ]
    out_specs = [
        pl.BlockSpec((1, ts, d), lambda bi, ji: (bi, ji, 0)),
        pl.BlockSpec((1, ts, d + 2 * LANES), lambda bi, ji: (bi, ji, 0)),
        pl.BlockSpec((1, 8, ts), lambda bi, ji: (bi * nst + ji, 0, 0)),
    ]
    out_shape = [
        jax.ShapeDtypeStruct((b, s, d), F32),
        jax.ShapeDtypeStruct((b, s, d + 2 * LANES), F32),
        jax.ShapeDtypeStruct((b * nst, 8, ts), F32),
    ]
    scratch = [
        pltpu.VMEM((CONV_PAD + ts, D_CONV), F32),
        pltpu.VMEM((SUBLANES - 1, CONV_PAD + ts - SUBLANES, D_CONV), F32),
        pltpu.VMEM((ts, D_CONV), F32),
        pltpu.VMEM((GLA_HEADS, GLA_DK, GLA_DV), F32),
        pltpu.VMEM((D_GLA_K, D_GLA_V), BF16),
        pltpu.VMEM((ts, d), BF16),
    ]
    return pl.pallas_call(
        functools.partial(_mix_kernel, ts=ts),
        grid=(b, nst),
        in_specs=in_specs,
        out_specs=out_specs,
        out_shape=out_shape,
        scratch_shapes=scratch,
        compiler_params=pltpu.CompilerParams(
            dimension_semantics=("arbitrary", "arbitrary"),
            vmem_limit_bytes=56 * 1024 * 1024),
        name="mix",
    )(x, mod, lw["gpre"], lw["gpost"], lw["gffn"], lw["wnat"], lw["wt"], lw["wdw"], lw["bdw"],
      lw["lng"], lw["lnb"], lw["wup"], lw["bup"], lw["gnorm"], lw["wout"], consts["wr"],
      consts["br"], consts["ubd"], consts["ebd"])


def _row_copy(idx_ref, base, r, src_hbm, dst_vmem, sem):
    tok = idx_ref[base + r]
    return pltpu.make_async_copy(src_hbm.at[pl.ds(tok, 1), :], dst_vmem.at[pl.ds(r, 1), :], sem)


def _row_gather_start(idx_ref, base, src_hbm, dst_vmem, sem, n_rows, unrolled):
    if unrolled:
        for r in range(n_rows):
            _row_copy(idx_ref, base, r, src_hbm, dst_vmem, sem).start(priority=r % DMA_QUEUES)
    else:
        def body(r, carry):
            _row_copy(idx_ref, base, r, src_hbm, dst_vmem, sem).start()
            return carry
        lax.fori_loop(0, n_rows, body, 0)


def _row_gather_wait(src_hbm, dst_vmem, sem, n_rows):
    pltpu.make_async_copy(src_hbm.at[pl.ds(0, n_rows), :], dst_vmem, sem).wait()


def _gather_pipeline(idx_ref, src_hbm, buf, sems, n_rows):
    i = pl.program_id(0)
    n = pl.num_programs(0)

    @pl.when(i == 0)
    def _():
        for k in range(GATHER_SLOTS - 1):
            @pl.when(k < n)
            def _():
                _row_gather_start(idx_ref, k * n_rows, src_hbm, buf.at[k], sems.at[k], n_rows, unrolled=False)

    ahead = i + GATHER_SLOTS - 1

    @pl.when(ahead < n)
    def _():
        aslot = lax.rem(ahead, GATHER_SLOTS)
        _row_gather_start(idx_ref, ahead * n_rows, src_hbm, buf.at[aslot], sems.at[aslot], n_rows,
                          unrolled=True)

    slot = lax.rem(i, GATHER_SLOTS)
    _row_gather_wait(src_hbm, buf.at[slot], sems.at[slot], n_rows)
    return slot


def _moe_kernel(te1_ref, te2_ref, tvalid_ref,
                hs_ref, w1a_ref, w3a_ref, w2a_ref, w1b_ref, w3b_ref, w2b_ref, gpost_ref,
                z_ref):
    i = pl.program_id(0)
    d = z_ref.shape[1]

    @pl.when(tvalid_ref[i] == 0)
    def _():
        z_ref[...] = jnp.zeros_like(z_ref)

    @pl.when(tvalid_ref[i] != 0)
    def _():
        h = hs_ref[:, 0:d].astype(BF16)
        y = None
        for e, (w1, w3, w2) in enumerate(((w1a_ref, w3a_ref, w2a_ref), (w1b_ref, w3b_ref, w2b_ref))):
            g_t = lax.dot_general(w1[0, 0], h, NT_DIMS, preferred_element_type=F32)
            u_t = lax.dot_general(w3[0, 0], h, NT_DIMS, preferred_element_type=F32)
            he_t = ((g_t * _sigmoid(g_t)) * u_t).astype(BF16)
            ye = lax.dot_general(he_t, w2[0, 0], TN_DIMS, preferred_element_type=F32)
            wcol = hs_ref[:, d + e * LANES:d + (e + 1) * LANES]
            ye = ye * jnp.concatenate([wcol] * (d // LANES), axis=1)
            y = ye if y is None else y + ye
        z_ref[...] = _rms(y, gpost_ref[...])


def _moe(hs, te1, te2, tvalid, w1, w3, w2, gpost, layer, tm):
    dx = hs.shape[1]
    d = dx - 2 * LANES
    nt = te1.shape[0]
    wspec_a = lambda shape: pl.BlockSpec((1, 1) + shape, lambda i, e1, e2, tv: (layer, e1[i], 0, 0))
    wspec_b = lambda shape: pl.BlockSpec((1, 1) + shape, lambda i, e1, e2, tv: (layer, e2[i], 0, 0))
    grid_spec = pltpu.PrefetchScalarGridSpec(
        num_scalar_prefetch=3,
        grid=(nt,),
        in_specs=[
            pl.BlockSpec((tm, dx), lambda i, e1, e2, tv: (i, 0)),
            wspec_a((D_EXPERT_PAD, d)), wspec_a((D_EXPERT_PAD, d)), wspec_a((D_EXPERT_PAD, d)),
            wspec_b((D_EXPERT_PAD, d)), wspec_b((D_EXPERT_PAD, d)), wspec_b((D_EXPERT_PAD, d)),
            pl.BlockSpec((1, d), lambda i, e1, e2, tv: (0, 0)),
        ],
        out_specs=pl.BlockSpec((tm, d), lambda i, e1, e2, tv: (i, 0)),
    )
    return pl.pallas_call(
        _moe_kernel,
        grid_spec=grid_spec,
        out_shape=jax.ShapeDtypeStruct((nt * tm, d), F32),
        compiler_params=pltpu.CompilerParams(
            dimension_semantics=("arbitrary",),
            vmem_limit_bytes=48 * 1024 * 1024),
        name="moe",
    )(te1, te2, tvalid, hs, w1, w3, w2, w1, w3, w2, gpost)


def _combine_kernel(pos_ref, x1_ref, g2_ref, z_hbm, o_ref, zg, sems, *, tc):
    slot = _gather_pipeline(pos_ref, z_hbm, zg, sems, tc)
    o_ref[...] = x1_ref[...] + g2_ref[0] * zg[slot]


def _combine(x1, g2, z, pos, seq_len, tc):
    t, d = x1.shape
    per_seq = seq_len // tc
    grid_spec = pltpu.PrefetchScalarGridSpec(
        num_scalar_prefetch=1,
        grid=(t // tc,),
        in_specs=[
            pl.BlockSpec((tc, d), lambda i, pos: (i, 0)),
            pl.BlockSpec((1, 1, d), lambda i, pos: (i // per_seq, 0, 0)),
            pl.BlockSpec(memory_space=pl.ANY),
        ],
        out_specs=pl.BlockSpec((tc, d), lambda i, pos: (i, 0)),
        scratch_shapes=[
            pltpu.VMEM((GATHER_SLOTS, tc, d), F32),
            pltpu.SemaphoreType.DMA((GATHER_SLOTS,)),
        ],
    )
    return pl.pallas_call(
        functools.partial(_combine_kernel, tc=tc),
        grid_spec=grid_spec,
        out_shape=jax.ShapeDtypeStruct((t, d), F32),
        compiler_params=pltpu.CompilerParams(
            dimension_semantics=("arbitrary",),
            vmem_limit_bytes=32 * 1024 * 1024),
        name="combine",
    )(pos, x1, g2, z)


def _class_tables():
    e1, e2 = [], []
    for g in range(N_GROUPS):
        for a in range(EXPERTS_PER_GROUP):
            for b in range(a + 1, EXPERTS_PER_GROUP):
                e1.append(g * EXPERTS_PER_GROUP + a)
                e2.append(g * EXPERTS_PER_GROUP + b)
    return jnp.asarray(e1, jnp.int32), jnp.asarray(e2, jnp.int32)


def _dispatch_kernel(pos_ref, pad_start_ref, pad_len_ref, used_ref, h_hbm, zero_ref, hs_hbm,
                     buf, in_sems, row_sems, pad_sem, *, rows, tm):
    i = pl.program_id(0)
    n = pl.num_programs(0)
    n_tiles = hs_hbm.shape[0] // tm

    def for_each_pad_row(fn):
        for c in range(N_CLASSES):
            def body(j, carry, c=c):
                fn(pltpu.make_async_copy(zero_ref.at[pl.ds(0, 1), :],
                                         hs_hbm.at[pl.ds(pad_start_ref[c] + j, 1), :], pad_sem))
                return carry
            lax.fori_loop(0, pad_len_ref[c], body, 0)

        def tile_body(k, carry):
            fn(pltpu.make_async_copy(zero_ref, hs_hbm.at[pl.ds(pl.multiple_of(k * tm, tm), tm), :], pad_sem))
            return carry
        lax.fori_loop(used_ref[0], n_tiles, tile_body, 0)

    def block_in(blk, slot):
        start = pl.multiple_of(blk * rows, rows)
        return pltpu.make_async_copy(h_hbm.at[pl.ds(start, rows), :], buf.at[slot], in_sems.at[slot])

    def rows_wait(slot):
        pltpu.make_async_copy(buf.at[slot], hs_hbm.at[pl.ds(0, rows), :], row_sems.at[slot]).wait()

    @pl.when(i == 0)
    def _():
        for_each_pad_row(lambda cp: cp.start())
        for k in range(DISPATCH_AHEAD):
            @pl.when(k < n)
            def _():
                block_in(k, k).start()

    @pl.when(i >= DISPATCH_AHEAD)
    def _():
        rows_wait(lax.rem(i - DISPATCH_AHEAD, DISPATCH_SLOTS))

    @pl.when(i + DISPATCH_AHEAD < n)
    def _():
        block_in(i + DISPATCH_AHEAD, lax.rem(i + DISPATCH_AHEAD, DISPATCH_SLOTS)).start()

    slot = lax.rem(i, DISPATCH_SLOTS)
    block_in(i, slot).wait()
    base = i * rows
    for u in range(rows):
        pltpu.make_async_copy(buf.at[slot, pl.ds(u, 1), :], hs_hbm.at[pl.ds(pos_ref[base + u], 1), :],
                              row_sems.at[slot]).start()

    @pl.when(i == n - 1)
    def _():
        for k in range(DISPATCH_AHEAD):
            @pl.when(i - k >= 0)
            def _():
                rows_wait(lax.rem(i - k, DISPATCH_SLOTS))
        for_each_pad_row(lambda cp: cp.wait())


def _dispatch(h2p, pos, pad_start, pad_len, used_tiles, n_rows, rows, tm):
    t, dx = h2p.shape
    grid_spec = pltpu.PrefetchScalarGridSpec(
        num_scalar_prefetch=4,
        grid=(t // rows,),
        in_specs=[pl.BlockSpec(memory_space=pl.ANY),
                  pl.BlockSpec((tm, dx), lambda i, *_: (0, 0))],
        out_specs=pl.BlockSpec(memory_space=pl.ANY),
        scratch_shapes=[pltpu.VMEM((DISPATCH_SLOTS, rows, dx), F32),
                        pltpu.SemaphoreType.DMA((DISPATCH_SLOTS,)),
                        pltpu.SemaphoreType.DMA((DISPATCH_SLOTS,)),
                        pltpu.SemaphoreType.DMA],
    )
    return pl.pallas_call(
        functools.partial(_dispatch_kernel, rows=rows, tm=tm),
        grid_spec=grid_spec,
        out_shape=jax.ShapeDtypeStruct((n_rows, dx), F32),
        compiler_params=pltpu.CompilerParams(dimension_semantics=("arbitrary",)),
        name="dispatch",
    )(pos, pad_start, pad_len, used_tiles, h2p, jnp.zeros((tm, dx), F32))


def _plan(cls, tm):
    t = cls.shape[0]
    nt = t // tm + N_CLASSES
    onehot = (cls[:, None] == jnp.arange(N_CLASSES, dtype=jnp.int32)[None, :]).astype(jnp.int32)
    csum = jnp.cumsum(onehot, axis=0)
    counts = csum[-1]
    rank = jnp.take_along_axis(csum, cls[:, None], axis=1)[:, 0] - 1
    tiles = (counts + tm - 1) // tm
    tile_end = jnp.cumsum(tiles)
    tile_start = tile_end - tiles
    pos = tile_start[cls] * tm + rank
    tile_ids = jnp.arange(nt, dtype=jnp.int32)
    tile_cls = jnp.sum((tile_end[None, :] <= tile_ids[:, None]).astype(jnp.int32), axis=1)
    tvalid = (tile_ids < tile_end[-1]).astype(jnp.int32)
    last_cls = jnp.max(jnp.where(counts > 0, jnp.arange(N_CLASSES, dtype=jnp.int32), 0))
    tile_cls = jnp.where(tvalid == 1, jnp.minimum(tile_cls, N_CLASSES - 1), last_cls)
    ce1, ce2 = _class_tables()
    pad_start = (tile_start * tm + counts).astype(jnp.int32)
    pad_len = (tiles * tm - counts).astype(jnp.int32)
    used_tiles = tile_end[-1:].astype(jnp.int32)
    return pos.astype(jnp.int32), pad_start, pad_len, used_tiles, ce1[tile_cls], ce2[tile_cls], tvalid


def _layer_weights(l, g_pre_mix, g_post_mix, g_pre_ffn, g_post_ffn, w_in, w_dw, b_dw, conv_ln_g,
                   conv_ln_b, w_gate_up, b_gate, gla_norm_g, w_out):
    d = w_in.shape[1]
    wi = w_in[l]
    o_q = 2 * D_CONV
    o_k = o_q + D_GLA_K
    o_v = o_k + D_GLA_K
    o_g = o_v + D_GLA_V
    o_lr = o_g + D_GLA_V
    wnat = jnp.concatenate([wi[:, 0:o_q], wi[:, o_q:o_k], wi[:, o_v:o_g], wi[:, o_g:o_lr]], axis=1)
    wt = jnp.concatenate([wi[:, o_k:o_v], wi[:, o_lr:]], axis=1).T
    return dict(
        gpre=g_pre_mix[l].reshape(1, d), gpost=g_post_mix[l].reshape(1, d),
        gffn=g_pre_ffn[l].reshape(1, d), gpostffn=g_post_ffn[l].reshape(1, d),
        wnat=wnat.astype(BF16), wt=wt.astype(BF16),
        wdw=jnp.repeat(w_dw[l], SUBLANES, axis=0), bdw=b_dw[l].reshape(1, D_CONV),
        lng=conv_ln_g[l].reshape(1, D_CONV), lnb=conv_ln_b[l].reshape(1, D_CONV),
        wup=w_gate_up[l].T.astype(BF16), bup=b_gate[l].reshape(D_GLA_K, 1),
        gnorm=gla_norm_g[l].reshape(1, GLA_DV),
        wout=w_out[l].astype(BF16),
    )


def kernel(x, c, w_ada, b_ada, g_pre_mix, g_post_mix, g_pre_ffn, g_post_ffn, w_in, w_dw, b_dw,
           conv_ln_g, conv_ln_b, w_gate_up, b_gate, gla_norm_g, w_out, w_router, b_router, w1, w3, w2):
    b, s, d = x.shape
    depth = w_ada.shape[0]
    t = b * s
    ts = min(SEQ_TILE, s)
    tm = min(MOE_TILE, t)
    tc = min(COMBINE_TILE, s)
    assert s % ts == 0 and ts % CHUNK == 0 and t % tm == 0 and s % tc == 0

    mod = _modulation(c, w_ada, b_ada).reshape(depth, b, 6, d)

    wr_hi, wr_lo = _split_bf16(w_router.T)
    tok = jnp.arange(min(ts, MXU_TILE), dtype=jnp.int32)
    same = (tok[:, None] // CHUNK) == (tok[None, :] // CHUNK)
    consts = dict(
        wr=jnp.concatenate([wr_hi, wr_lo], axis=0),
        br=b_router.reshape(N_EXPERTS, 1),
        ubd=(same & (tok[:, None] <= tok[None, :])).astype(BF16),
        ebd=same.astype(BF16),
    )

    w1b = _cast_pad(jnp.swapaxes(w1, 2, 3), D_EXPERT_PAD, d)
    w3b = _cast_pad(jnp.swapaxes(w3, 2, 3), D_EXPERT_PAD, d)
    w2b = _cast_pad(w2, D_EXPERT_PAD, d)

    for l in range(depth):
        lw = _layer_weights(l, g_pre_mix, g_post_mix, g_pre_ffn, g_post_ffn, w_in, w_dw, b_dw,
                            conv_ln_g, conv_ln_b, w_gate_up, b_gate, gla_norm_g, w_out)
        x1, h2p, route = _mix(x, mod[l], lw, consts, ts)
        cls = route[:, 0, :].reshape(t).astype(jnp.int32)
        pos, pad_start, pad_len, used_tiles, te1, te2, tvalid = _plan(cls, tm)
        hs = _dispatch(h2p.reshape(t, d + 2 * LANES), pos, pad_start, pad_len, used_tiles,
                       te1.shape[0] * tm, min(DISPATCH_ROWS, t), tm)
        z = _moe(hs, te1, te2, tvalid, w1b, w3b, w2b, lw["gpostffn"], l, tm)
        g2 = mod[l][:, 5:6, :]
        x = _combine(x1.reshape(t, d), g2, z, pos, s, tc).reshape(b, s, d)
    return x
```

```python
import functools

import jax
import jax.numpy as jnp
from jax import lax
from jax.experimental import pallas as pl
from jax.experimental.pallas import tpu as pltpu

CHUNK = 64
CONV_WIDTH = 31
D_CONV = 512
GLA_HEADS = 4
GLA_DV = 128
GLA_DK = 64
D_GLA_K = GLA_HEADS * GLA_DK
D_GLA_V = GLA_HEADS * GLA_DV
GATE_RANK = 16
GATE_TAU = 16.0
N_EXPERTS = 16
N_GROUPS = 4
EXPERTS_PER_GROUP = 4
PAIRS_PER_GROUP = 6
N_CLASSES = N_GROUPS * PAIRS_PER_GROUP
CLASS_ROWS = 32
D_EXPERT = 704
D_EXPERT_PAD = 768
EPS = 1e-6
LOG2_E = 1.4426950408889634

SUBLANES = 8
LANES = 128
MXU_TILE = 256
CONV_PAD = 32
CONV_ROWS = 32
CONV_LANES = 256
SEQ_TILE = 512
MOE_TILE = 256
COMBINE_TILE = 256
DISPATCH_ROWS = 256
DISPATCH_AHEAD = 2
DISPATCH_SLOTS = 2 * DISPATCH_AHEAD
DMA_QUEUES = 2
GATHER_SLOTS = 3

F32 = jnp.float32
BF16 = jnp.bfloat16
NT_DIMS = (((1,), (1,)), ((), ()))
TN_DIMS = (((0,), (0,)), ((), ()))


def _sigmoid(x):
    return 1.0 / (1.0 + jnp.exp2(x * -LOG2_E))


def _rms(x, g):
    ms = jnp.mean(x * x, axis=-1, keepdims=True)
    return x * lax.rsqrt(ms + EPS) * g


def _split_bf16(x):
    hi = x.astype(BF16)
    lo = (x - hi.astype(F32)).astype(BF16)
    return hi, lo


def _mod_kernel(c_ref, w_ref, b_ref, o_ref):
    c = c_ref[...]
    ca = c * _sigmoid(c)
    a_hi, a_lo = _split_bf16(ca)
    w_hi, w_lo = _split_bf16(w_ref[0])
    acc = jnp.dot(a_hi, w_hi, preferred_element_type=F32)
    acc += jnp.dot(a_lo, w_hi, preferred_element_type=F32)
    acc += jnp.dot(a_hi, w_lo, preferred_element_type=F32)
    o_ref[0] = acc + b_ref[0]


def _modulation(c, w_ada, b_ada):
    depth, d, d6 = w_ada.shape
    b = c.shape[0]
    nblk = d6 // d
    return pl.pallas_call(
        _mod_kernel,
        grid=(depth, nblk),
        in_specs=[
            pl.BlockSpec((b, d), lambda l, n: (0, 0)),
            pl.BlockSpec((1, d, d), lambda l, n: (l, 0, n)),
            pl.BlockSpec((1, 1, d), lambda l, n: (l, 0, n)),
        ],
        out_specs=pl.BlockSpec((1, b, d), lambda l, n: (l, 0, n)),
        out_shape=jax.ShapeDtypeStruct((depth, b, d6), F32),
        compiler_params=pltpu.CompilerParams(
            dimension_semantics=("arbitrary", "arbitrary"),
            vmem_limit_bytes=32 * 1024 * 1024),
        name="adaln_mod",
    )(c, w_ada, b_ada.reshape(depth, 1, d6))


def _cast_pad_kernel(w_ref, o_ref):
    rows, cols = w_ref.shape[2:]
    prow, pcol = o_ref.shape[2:]
    o_ref[0, 0, 0:rows, 0:cols] = w_ref[0, 0].astype(BF16)
    if pcol > cols:
        o_ref[0, 0, :, cols:] = jnp.zeros((prow, pcol - cols), BF16)
    if prow > rows:
        o_ref[0, 0, rows:, 0:cols] = jnp.zeros((prow - rows, cols), BF16)


def _cast_pad(w, prow, pcol):
    depth, ne, rows, cols = w.shape
    return pl.pallas_call(
        _cast_pad_kernel,
        grid=(depth, ne),
        in_specs=[pl.BlockSpec((1, 1, rows, cols), lambda l, e: (l, e, 0, 0))],
        out_specs=pl.BlockSpec((1, 1, prow, pcol), lambda l, e: (l, e, 0, 0)),
        out_shape=jax.ShapeDtypeStruct((depth, ne, prow, pcol), BF16),
        compiler_params=pltpu.CompilerParams(
            dimension_semantics=("arbitrary", "arbitrary"),
            vmem_limit_bytes=32 * 1024 * 1024),
        name="cast_pad",
    )(w)


def _route(logits_t, br):
    m = jnp.max(logits_t, axis=0, keepdims=True)
    e = jnp.exp(logits_t - m)
    probs = e / jnp.sum(e, axis=0, keepdims=True)
    sel = probs + br
    one = jnp.ones_like(m)
    zero = jnp.zeros_like(m)
    picked, gscore = [], []
    for g in range(N_GROUPS):
        rows = [sel[g * EXPERTS_PER_GROUP + k:g * EXPERTS_PER_GROUP + k + 1, :]
                for k in range(EXPERTS_PER_GROUP)]
        score = zero
        for k in range(EXPERTS_PER_GROUP):
            rank = zero
            for k2 in range(EXPERTS_PER_GROUP):
                if k2 == k:
                    continue
                ahead = (rows[k2] >= rows[k]) if k2 < k else (rows[k2] > rows[k])
                rank = rank + jnp.where(ahead, one, zero)
            pk = jnp.where(rank < 2.0, one, zero)
            picked.append(pk)
            score = score + pk * rows[k]
        gscore.append(score)
    best = gscore[0]
    for g in range(1, N_GROUPS):
        best = jnp.maximum(best, gscore[g])
    taken = zero
    flag = [zero] * EXPERTS_PER_GROUP
    prob = [zero] * EXPERTS_PER_GROUP
    gbase = zero
    for g in range(N_GROUPS):
        isb = jnp.where(gscore[g] == best, one, zero) * (one - taken)
        taken = taken + isb
        gbase = gbase + isb * float(g * PAIRS_PER_GROUP)
        for k in range(EXPERTS_PER_GROUP):
            ei = g * EXPERTS_PER_GROUP + k
            flag[k] = flag[k] + isb * picked[ei]
            prob[k] = prob[k] + isb * probs[ei:ei + 1, :]
    f0, f1, f2, f3 = flag
    pair = f0 * (f2 * 1.0 + f3 * 2.0) + (one - f0) * (f1 * (f2 * 3.0 + f3 * 4.0) + (one - f1) * 5.0)
    cls = gbase + pair
    wa_raw = f0 * prob[0] + (one - f0) * (f1 * prob[1] + (one - f1) * prob[2])
    tot = f0 * prob[0] + f1 * prob[1] + f2 * prob[2] + f3 * prob[3]
    wa = wa_raw / tot
    wb = (tot - wa_raw) / tot
    return cls, wa, wb


def _mix_kernel(x_ref, mod_ref, gpre_ref, gpost_ref, gffn_ref, wnat_ref, wt_ref, wdw_ref, bdw_ref,
                lng_ref, lnb_ref, wup_ref, bup_ref, gnorm_ref, wout_ref, wr_ref, br_ref,
                ubd_ref, ebd_ref, tri_ref,
                x1_ref, h2p_ref, route_ref,
                ubuf, ush, cbuf, st_ref, sbd_ref, ycat_ref, *, ts):
    j = pl.program_id(1)
    nch = ts // CHUNK

    @pl.when(j == 0)
    def _():
        ubuf[0:CONV_PAD, :] = jnp.zeros((CONV_PAD, D_CONV), F32)
        st_ref[...] = jnp.zeros_like(st_ref)
        sbd_ref[...] = jnp.zeros_like(sbd_ref)

    @pl.when(j > 0)
    def _():
        ubuf[0:CONV_PAD, :] = ubuf[ts:ts + CONV_PAD, :]

    x = x_ref[0]
    sh1 = mod_ref[0, 0:1, :]
    sc1 = mod_ref[0, 1:2, :]
    g1 = mod_ref[0, 2:3, :]
    sh2 = mod_ref[0, 3:4, :]
    sc2 = mod_ref[0, 4:5, :]

    h = _rms(x, gpre_ref[...] * (1.0 + sc1)) + sh1
    hb = h.astype(BF16)

    cvg = jnp.dot(hb, wnat_ref[:, 0:2 * D_CONV], preferred_element_type=F32)
    ubuf[CONV_PAD:CONV_PAD + ts, :] = cvg[:, 0:D_CONV] * _sigmoid(cvg[:, D_CONV:2 * D_CONV])

    for r in range(1, SUBLANES):
        ush[r - 1] = ubuf[r:r + ts + CONV_PAD - SUBLANES, :]

    groups = CONV_ROWS // SUBLANES
    first_off = CONV_PAD - (CONV_WIDTH - 1)

    def conv_block(rb, carry, c0):
        r0 = rb * CONV_ROWS
        cs = slice(c0, c0 + CONV_LANES)
        acc = [jnp.broadcast_to(bdw_ref[:, cs], (SUBLANES, CONV_LANES)) for _ in range(groups)]
        for r in range(SUBLANES):
            taps = [tap for tap in range(CONV_WIDTH) if (first_off + tap) % SUBLANES == r]
            shifts = [(first_off + tap) // SUBLANES for tap in taps]
            src = ubuf if r == 0 else ush.at[r - 1]
            blk = {m: src[pl.ds(r0 + m * SUBLANES, SUBLANES), cs]
                   for m in range(min(shifts), max(shifts) + groups)}
            for tap, a in zip(taps, shifts):
                w8 = wdw_ref[tap * SUBLANES:(tap + 1) * SUBLANES, cs]
                for g in range(groups):
                    acc[g] = acc[g] + w8 * blk[a + g]
        cbuf[pl.ds(r0, CONV_ROWS), cs] = jnp.concatenate(acc, axis=0)
        return carry

    for c0 in range(0, D_CONV, CONV_LANES):
        for rb in range(ts // CONV_ROWS):
            conv_block(rb, 0, c0)
    cv = cbuf[...]
    mu = jnp.mean(cv, axis=-1, keepdims=True)
    cen = cv - mu
    var = jnp.mean(cen * cen, axis=-1, keepdims=True)
    yn = cen * lax.rsqrt(var + EPS) * lng_ref[...] + lnb_ref[...]
    ycat_ref[:, 0:D_CONV] = (yn * _sigmoid(yn)).astype(BF16)

    qvo = jnp.dot(hb, wnat_ref[:, 2 * D_CONV:], preferred_element_type=F32)
    q = (qvo[:, 0:D_GLA_K] * (GLA_DK ** -0.5)).astype(BF16)
    v = qvo[:, D_GLA_K:D_GLA_K + D_GLA_V].astype(BF16)
    og = qvo[:, D_GLA_K + D_GLA_V:]
    tt = lax.dot_general(wt_ref[...], hb, NT_DIMS, preferred_element_type=F32)
    k_t = tt[0:D_GLA_K, :]
    lr_t = tt[D_GLA_K:, :].astype(BF16)
    z_t = jnp.dot(wup_ref[...], lr_t, preferred_element_type=F32) + bup_ref[...]
    la_t = (jnp.minimum(z_t, 0.0) - jnp.log(1.0 + jnp.exp(-jnp.abs(z_t)))) * (1.0 / GATE_TAU)
    la_b = la_t.astype(BF16)
    slab = ubd_ref.shape[0]
    bcum = jnp.concatenate(
        [jnp.dot(la_b[:, s0:s0 + slab], ubd_ref[...], preferred_element_type=F32)
         for s0 in range(0, ts, slab)], axis=1)
    bend = jnp.concatenate(
        [jnp.dot(la_b[:, s0:s0 + slab], ebd_ref[...], preferred_element_type=F32)
         for s0 in range(0, ts, slab)], axis=1)
    kd_t = (k_t * jnp.exp(bend - bcum)).astype(BF16)
    dec_t = jnp.exp(bend)

    st = [st_ref[hh] for hh in range(GLA_HEADS)]
    gn = gnorm_ref[...]
    for c in range(nch):
        lo, hi_ = c * CHUNK, (c + 1) * CHUNK
        kd_c = kd_t[:, lo:hi_]
        v_c = v[lo:hi_, :]
        for hh in range(GLA_HEADS):
            ks, ke = hh * GLA_DK, (hh + 1) * GLA_DK
            vs, ve = hh * GLA_DV, (hh + 1) * GLA_DV
            upd = jnp.dot(kd_c[ks:ke, :], v_c[:, vs:ve], preferred_element_type=F32)
            dec = jnp.broadcast_to(dec_t[ks:ke, lo:lo + 1], (GLA_DK, GLA_DV))
            st[hh] = dec * st[hh] + upd
            sbd_ref[ks:ke, vs:ve] = st[hh].astype(BF16)
        o_c = jnp.dot(q[lo:hi_, :], sbd_ref[...], preferred_element_type=F32)
        og_c = og[lo:hi_, :]
        for hh in range(GLA_HEADS):
            vs, ve = hh * GLA_DV, (hh + 1) * GLA_DV
            on = _rms(o_c[:, vs:ve], gn)
            gate = og_c[:, vs:ve]
            ycat_ref[lo:hi_, D_CONV + vs:D_CONV + ve] = (on * (gate * _sigmoid(gate))).astype(BF16)
    for hh in range(GLA_HEADS):
        st_ref[hh] = st[hh]

    y = jnp.dot(ycat_ref[...], wout_ref[...], preferred_element_type=F32)
    x1 = x + g1 * _rms(y, gpost_ref[...])
    x1_ref[0] = x1
    h2 = _rms(x1, gffn_ref[...] * (1.0 + sc2)) + sh2
    d = h2.shape[1]
    h2p_ref[0, :, 0:d] = h2
    h2_hi, h2_lo = _split_bf16(h2)
    lg2 = lax.dot_general(wr_ref[...], h2_hi, NT_DIMS, preferred_element_type=F32)
    lg1 = lax.dot_general(wr_ref[0:N_EXPERTS, :], h2_lo, NT_DIMS, preferred_element_type=F32)
    logits_t = lg2[0:N_EXPERTS, :] + lg2[N_EXPERTS:, :] + lg1
    cls, wa, wb = _route(logits_t, br_ref[...])
    class_ids = lax.broadcasted_iota(jnp.int32, (CLASS_ROWS, ts), 0).astype(F32)
    onehot_t = jnp.where(class_ids == cls, 1.0, 0.0)
    seen = jnp.dot(onehot_t.astype(BF16), tri_ref[...], preferred_element_type=F32)
    rank = jnp.sum(onehot_t * seen, axis=0, keepdims=True) - 1.0
    route_ref[0] = jnp.concatenate([cls, rank, jnp.zeros((SUBLANES - 2, ts), F32)], axis=0)
    h2p_ref[0, :, d:d + LANES] = jnp.transpose(jnp.broadcast_to(wa, (LANES, ts)))
    h2p_ref[0, :, d + LANES:] = jnp.transpose(jnp.broadcast_to(wb, (LANES, ts)))


def _mix(x, mod, lw, consts, ts):
    b, s, d = x.shape
    nst = s // ts
    full = lambda shape: pl.BlockSpec(shape, lambda bi, ji: (0,) * len(shape))
    in_specs = [
        pl.BlockSpec((1, ts, d), lambda bi, ji: (bi, ji, 0)),
        pl.BlockSpec((1, 6, d), lambda bi, ji: (bi, 0, 0)),
        full((1, d)), full((1, d)), full((1, d)),
        full(lw["wnat"].shape), full(lw["wt"].shape),
        full((CONV_WIDTH * SUBLANES, D_CONV)), full((1, D_CONV)), full((1, D_CONV)), full((1, D_CONV)),
        full((D_GLA_K, GATE_RANK)), full((D_GLA_K, 1)), full((1, GLA_DV)),
        full((d, d)), full((2 * N_EXPERTS, d)), full((N_EXPERTS, 1)),
        full(consts["ubd"].shape), full(consts["ebd"].shape), full((ts, ts)),
    ]
    out_specs = [
        pl.BlockSpec((1, ts, d), lambda bi, ji: (bi, ji, 0)),
        pl.BlockSpec((1, ts, d + 2 * LANES), lambda bi, ji: (bi, ji, 0)),
        pl.BlockSpec((1, 8, ts), lambda bi, ji: (bi * nst + ji, 0, 0)),
    ]
    out_shape = [
        jax.ShapeDtypeStruct((b, s, d), F32),
        jax.ShapeDtypeStruct((b, s, d + 2 * LANES), F32),
        jax.ShapeDtypeStruct((b * nst, 8, ts), F32),
    ]
    scratch = [
        pltpu.VMEM((CONV_PAD + ts, D_CONV), F32),
        pltpu.VMEM((SUBLANES - 1, CONV_PAD + ts - SUBLANES, D_CONV), F32),
        pltpu.VMEM((ts, D_CONV), F32),
        pltpu.VMEM((GLA_HEADS, GLA_DK, GLA_DV), F32),
        pltpu.VMEM((D_GLA_K, D_GLA_V), BF16),
        pltpu.VMEM((ts, d), BF16),
    ]
    return pl.pallas_call(
        functools.partial(_mix_kernel, ts=ts),
        grid=(b, nst),
        in_specs=in_specs,
        out_specs=out_specs,
        out_shape=out_shape,
        scratch_shapes=scratch,
        compiler_params=pltpu.CompilerParams(
            dimension_semantics=("arbitrary", "arbitrary"),
            vmem_limit_bytes=56 * 1024 * 1024),
        name="mix",
    )(x, mod, lw["gpre"], lw["gpost"], lw["gffn"], lw["wnat"], lw["wt"], lw["wdw"], lw["bdw"],
      lw["lng"], lw["lnb"], lw["wup"], lw["bup"], lw["gnorm"], lw["wout"], consts["wr"],
      consts["br"], consts["ubd"], consts["ebd"], consts["tri"])


def _row_copy(idx_ref, base, r, src_hbm, dst_vmem, sem):
    tok = idx_ref[base + r]
    return pltpu.make_async_copy(src_hbm.at[pl.ds(tok, 1), :], dst_vmem.at[pl.ds(r, 1), :], sem)


def _row_gather_start(idx_ref, base, src_hbm, dst_vmem, sem, n_rows, unrolled):
    if unrolled:
        for r in range(n_rows):
            _row_copy(idx_ref, base, r, src_hbm, dst_vmem, sem).start(priority=r % DMA_QUEUES)
    else:
        def body(r, carry):
            _row_copy(idx_ref, base, r, src_hbm, dst_vmem, sem).start()
            return carry
        lax.fori_loop(0, n_rows, body, 0)


def _row_gather_wait(src_hbm, dst_vmem, sem, n_rows):
    pltpu.make_async_copy(src_hbm.at[pl.ds(0, n_rows), :], dst_vmem, sem).wait()


def _gather_pipeline(idx_ref, src_hbm, buf, sems, n_rows):
    i = pl.program_id(0)
    n = pl.num_programs(0)

    @pl.when(i == 0)
    def _():
        for k in range(GATHER_SLOTS - 1):
            @pl.when(k < n)
            def _():
                _row_gather_start(idx_ref, k * n_rows, src_hbm, buf.at[k], sems.at[k], n_rows, unrolled=False)

    ahead = i + GATHER_SLOTS - 1

    @pl.when(ahead < n)
    def _():
        aslot = lax.rem(ahead, GATHER_SLOTS)
        _row_gather_start(idx_ref, ahead * n_rows, src_hbm, buf.at[aslot], sems.at[aslot], n_rows,
                          unrolled=True)

    slot = lax.rem(i, GATHER_SLOTS)
    _row_gather_wait(src_hbm, buf.at[slot], sems.at[slot], n_rows)
    return slot


def _moe_kernel(te1_ref, te2_ref, tvalid_ref,
                hs_ref, w1a_ref, w3a_ref, w2a_ref, w1b_ref, w3b_ref, w2b_ref, gpost_ref,
                z_ref):
    i = pl.program_id(0)
    d = z_ref.shape[1]

    @pl.when(tvalid_ref[i] == 0)
    def _():
        z_ref[...] = jnp.zeros_like(z_ref)

    @pl.when(tvalid_ref[i] != 0)
    def _():
        h = hs_ref[:, 0:d].astype(BF16)
        y = None
        for e, (w1, w3, w2) in enumerate(((w1a_ref, w3a_ref, w2a_ref), (w1b_ref, w3b_ref, w2b_ref))):
            g_t = lax.dot_general(w1[0, 0], h, NT_DIMS, preferred_element_type=F32)
            u_t = lax.dot_general(w3[0, 0], h, NT_DIMS, preferred_element_type=F32)
            he_t = ((g_t * _sigmoid(g_t)) * u_t).astype(BF16)
            ye = lax.dot_general(he_t, w2[0, 0], TN_DIMS, preferred_element_type=F32)
            wcol = hs_ref[:, d + e * LANES:d + (e + 1) * LANES]
            ye = ye * jnp.concatenate([wcol] * (d // LANES), axis=1)
            y = ye if y is None else y + ye
        z_ref[...] = _rms(y, gpost_ref[...])


def _moe(hs, te1, te2, tvalid, w1, w3, w2, gpost, layer, tm):
    dx = hs.shape[1]
    d = dx - 2 * LANES
    nt = te1.shape[0]
    wspec_a = lambda shape: pl.BlockSpec((1, 1) + shape, lambda i, e1, e2, tv: (layer, e1[i], 0, 0))
    wspec_b = lambda shape: pl.BlockSpec((1, 1) + shape, lambda i, e1, e2, tv: (layer, e2[i], 0, 0))
    grid_spec = pltpu.PrefetchScalarGridSpec(
        num_scalar_prefetch=3,
        grid=(nt,),
        in_specs=[
            pl.BlockSpec((tm, dx), lambda i, e1, e2, tv: (i, 0)),
            wspec_a((D_EXPERT_PAD, d)), wspec_a((D_EXPERT_PAD, d)), wspec_a((D_EXPERT_PAD, d)),
            wspec_b((D_EXPERT_PAD, d)), wspec_b((D_EXPERT_PAD, d)), wspec_b((D_EXPERT_PAD, d)),
            pl.BlockSpec((1, d), lambda i, e1, e2, tv: (0, 0)),
        ],
        out_specs=pl.BlockSpec((tm, d), lambda i, e1, e2, tv: (i, 0)),
    )
    return pl.pallas_call(
        _moe_kernel,
        grid_spec=grid_spec,
        out_shape=jax.ShapeDtypeStruct((nt * tm, d), F32),
        compiler_params=pltpu.CompilerParams(
            dimension_semantics=("arbitrary",),
            vmem_limit_bytes=48 * 1024 * 1024),
        name="moe",
    )(te1, te2, tvalid, hs, w1, w3, w2, w1, w3, w2, gpost)


def _combine_kernel(pos_ref, x1_ref, g2_ref, z_hbm, o_ref, zg, sems, *, tc):
    slot = _gather_pipeline(pos_ref, z_hbm, zg, sems, tc)
    o_ref[...] = x1_ref[...] + g2_ref[0] * zg[slot]


def _combine(x1, g2, z, pos, seq_len, tc):
    t, d = x1.shape
    per_seq = seq_len // tc
    grid_spec = pltpu.PrefetchScalarGridSpec(
        num_scalar_prefetch=1,
        grid=(t // tc,),
        in_specs=[
            pl.BlockSpec((tc, d), lambda i, pos: (i, 0)),
            pl.BlockSpec((1, 1, d), lambda i, pos: (i // per_seq, 0, 0)),
            pl.BlockSpec(memory_space=pl.ANY),
        ],
        out_specs=pl.BlockSpec((tc, d), lambda i, pos: (i, 0)),
        scratch_shapes=[
            pltpu.VMEM((GATHER_SLOTS, tc, d), F32),
            pltpu.SemaphoreType.DMA((GATHER_SLOTS,)),
        ],
    )
    return pl.pallas_call(
        functools.partial(_combine_kernel, tc=tc),
        grid_spec=grid_spec,
        out_shape=jax.ShapeDtypeStruct((t, d), F32),
        compiler_params=pltpu.CompilerParams(
            dimension_semantics=("arbitrary",),
            vmem_limit_bytes=32 * 1024 * 1024),
        name="combine",
    )(pos, x1, g2, z)


def _class_tables():
    e1, e2 = [], []
    for g in range(N_GROUPS):
        for a in range(EXPERTS_PER_GROUP):
            for b in range(a + 1, EXPERTS_PER_GROUP):
                e1.append(g * EXPERTS_PER_GROUP + a)
                e2.append(g * EXPERTS_PER_GROUP + b)
    return jnp.asarray(e1, jnp.int32), jnp.asarray(e2, jnp.int32)


def _dispatch_kernel(pos_ref, pad_start_ref, pad_len_ref, used_ref, h_hbm, zero_ref, hs_hbm,
                     buf, in_sems, row_sems, pad_sem, *, rows, tm):
    i = pl.program_id(0)
    n = pl.num_programs(0)
    n_tiles = hs_hbm.shape[0] // tm

    def for_each_pad_row(fn):
        for c in range(N_CLASSES):
            def body(j, carry, c=c):
                fn(pltpu.make_async_copy(zero_ref.at[pl.ds(0, 1), :],
                                         hs_hbm.at[pl.ds(pad_start_ref[c] + j, 1), :], pad_sem))
                return carry
            lax.fori_loop(0, pad_len_ref[c], body, 0)

        def tile_body(k, carry):
            fn(pltpu.make_async_copy(zero_ref, hs_hbm.at[pl.ds(pl.multiple_of(k * tm, tm), tm), :], pad_sem))
            return carry
        lax.fori_loop(used_ref[0], n_tiles, tile_body, 0)

    def block_in(blk, slot):
        start = pl.multiple_of(blk * rows, rows)
        return pltpu.make_async_copy(h_hbm.at[pl.ds(start, rows), :], buf.at[slot], in_sems.at[slot])

    def rows_wait(slot):
        pltpu.make_async_copy(buf.at[slot], hs_hbm.at[pl.ds(0, rows), :], row_sems.at[slot]).wait()

    @pl.when(i == 0)
    def _():
        for_each_pad_row(lambda cp: cp.start())
        for k in range(DISPATCH_AHEAD):
            @pl.when(k < n)
            def _():
                block_in(k, k).start()

    @pl.when(i >= DISPATCH_AHEAD)
    def _():
        rows_wait(lax.rem(i - DISPATCH_AHEAD, DISPATCH_SLOTS))

    @pl.when(i + DISPATCH_AHEAD < n)
    def _():
        block_in(i + DISPATCH_AHEAD, lax.rem(i + DISPATCH_AHEAD, DISPATCH_SLOTS)).start()

    slot = lax.rem(i, DISPATCH_SLOTS)
    block_in(i, slot).wait()
    base = i * rows
    for u in range(rows):
        pltpu.make_async_copy(buf.at[slot, pl.ds(u, 1), :], hs_hbm.at[pl.ds(pos_ref[base + u], 1), :],
                              row_sems.at[slot]).start()

    @pl.when(i == n - 1)
    def _():
        for k in range(DISPATCH_AHEAD):
            @pl.when(i - k >= 0)
            def _():
                rows_wait(lax.rem(i - k, DISPATCH_SLOTS))
        for_each_pad_row(lambda cp: cp.wait())


def _dispatch(h2p, pos, pad_start, pad_len, used_tiles, n_rows, rows, tm):
    t, dx = h2p.shape
    grid_spec = pltpu.PrefetchScalarGridSpec(
        num_scalar_prefetch=4,
        grid=(t // rows,),
        in_specs=[pl.BlockSpec(memory_space=pl.ANY),
                  pl.BlockSpec((tm, dx), lambda i, *_: (0, 0))],
        out_specs=pl.BlockSpec(memory_space=pl.ANY),
        scratch_shapes=[pltpu.VMEM((DISPATCH_SLOTS, rows, dx), F32),
                        pltpu.SemaphoreType.DMA((DISPATCH_SLOTS,)),
                        pltpu.SemaphoreType.DMA((DISPATCH_SLOTS,)),
                        pltpu.SemaphoreType.DMA],
    )
    return pl.pallas_call(
        functools.partial(_dispatch_kernel, rows=rows, tm=tm),
        grid_spec=grid_spec,
        out_shape=jax.ShapeDtypeStruct((n_rows, dx), F32),
        compiler_params=pltpu.CompilerParams(dimension_semantics=("arbitrary",)),
        name="dispatch",
    )(pos, pad_start, pad_len, used_tiles, h2p, jnp.zeros((tm, dx), F32))


def _plan(cls, rank, tm):
    t = cls.size
    nt = t // tm + N_CLASSES
    onehot = (cls[:, :, None] == jnp.arange(N_CLASSES, dtype=jnp.int32)[None, None, :]).astype(jnp.int32)
    per_tile = jnp.sum(onehot, axis=1)
    before = jnp.cumsum(per_tile, axis=0) - per_tile
    counts = jnp.sum(per_tile, axis=0)
    tiles = (counts + tm - 1) // tm
    tile_end = jnp.cumsum(tiles)
    tile_start = tile_end - tiles
    base = tile_start[None, :] * tm + before
    pos = (jnp.take_along_axis(base, cls, axis=1) + rank).reshape(t)
    tile_ids = jnp.arange(nt, dtype=jnp.int32)
    tile_cls = jnp.sum((tile_end[None, :] <= tile_ids[:, None]).astype(jnp.int32), axis=1)
    tvalid = (tile_ids < tile_end[-1]).astype(jnp.int32)
    last_cls = jnp.max(jnp.where(counts > 0, jnp.arange(N_CLASSES, dtype=jnp.int32), 0))
    tile_cls = jnp.where(tvalid == 1, jnp.minimum(tile_cls, N_CLASSES - 1), last_cls)
    ce1, ce2 = _class_tables()
    pad_start = (tile_start * tm + counts).astype(jnp.int32)
    pad_len = (tiles * tm - counts).astype(jnp.int32)
    used_tiles = tile_end[-1:].astype(jnp.int32)
    return pos.astype(jnp.int32), pad_start, pad_len, used_tiles, ce1[tile_cls], ce2[tile_cls], tvalid


def _layer_weights(l, g_pre_mix, g_post_mix, g_pre_ffn, g_post_ffn, w_in, w_dw, b_dw, conv_ln_g,
                   conv_ln_b, w_gate_up, b_gate, gla_norm_g, w_out):
    d = w_in.shape[1]
    wi = w_in[l]
    o_q = 2 * D_CONV
    o_k = o_q + D_GLA_K
    o_v = o_k + D_GLA_K
    o_g = o_v + D_GLA_V
    o_lr = o_g + D_GLA_V
    wnat = jnp.concatenate([wi[:, 0:o_q], wi[:, o_q:o_k], wi[:, o_v:o_g], wi[:, o_g:o_lr]], axis=1)
    wt = jnp.concatenate([wi[:, o_k:o_v], wi[:, o_lr:]], axis=1).T
    return dict(
        gpre=g_pre_mix[l].reshape(1, d), gpost=g_post_mix[l].reshape(1, d),
        gffn=g_pre_ffn[l].reshape(1, d), gpostffn=g_post_ffn[l].reshape(1, d),
        wnat=wnat.astype(BF16), wt=wt.astype(BF16),
        wdw=jnp.repeat(w_dw[l], SUBLANES, axis=0), bdw=b_dw[l].reshape(1, D_CONV),
        lng=conv_ln_g[l].reshape(1, D_CONV), lnb=conv_ln_b[l].reshape(1, D_CONV),
        wup=w_gate_up[l].T.astype(BF16), bup=b_gate[l].reshape(D_GLA_K, 1),
        gnorm=gla_norm_g[l].reshape(1, GLA_DV),
        wout=w_out[l].astype(BF16),
    )


def kernel(x, c, w_ada, b_ada, g_pre_mix, g_post_mix, g_pre_ffn, g_post_ffn, w_in, w_dw, b_dw,
           conv_ln_g, conv_ln_b, w_gate_up, b_gate, gla_norm_g, w_out, w_router, b_router, w1, w3, w2):
    b, s, d = x.shape
    depth = w_ada.shape[0]
    t = b * s
    ts = min(SEQ_TILE, s)
    tm = min(MOE_TILE, t)
    tc = min(COMBINE_TILE, s)
    assert s % ts == 0 and ts % CHUNK == 0 and t % tm == 0 and s % tc == 0

    mod = _modulation(c, w_ada, b_ada).reshape(depth, b, 6, d)

    wr_hi, wr_lo = _split_bf16(w_router.T)
    tok = jnp.arange(min(ts, MXU_TILE), dtype=jnp.int32)
    same = (tok[:, None] // CHUNK) == (tok[None, :] // CHUNK)
    consts = dict(
        wr=jnp.concatenate([wr_hi, wr_lo], axis=0),
        br=b_router.reshape(N_EXPERTS, 1),
        ubd=(same & (tok[:, None] <= tok[None, :])).astype(BF16),
        ebd=same.astype(BF16),
        tri=(jnp.arange(ts)[:, None] <= jnp.arange(ts)[None, :]).astype(BF16),
    )

    w1b = _cast_pad(jnp.swapaxes(w1, 2, 3), D_EXPERT_PAD, d)
    w3b = _cast_pad(jnp.swapaxes(w3, 2, 3), D_EXPERT_PAD, d)
    w2b = _cast_pad(w2, D_EXPERT_PAD, d)

    for l in range(depth):
        lw = _layer_weights(l, g_pre_mix, g_post_mix, g_pre_ffn, g_post_ffn, w_in, w_dw, b_dw,
                            conv_ln_g, conv_ln_b, w_gate_up, b_gate, gla_norm_g, w_out)
        x1, h2p, route = _mix(x, mod[l], lw, consts, ts)
        cls = route[:, 0, :].astype(jnp.int32)
        rank = route[:, 1, :].astype(jnp.int32)
        pos, pad_start, pad_len, used_tiles, te1, te2, tvalid = _plan(cls, rank, tm)
        hs = _dispatch(h2p.reshape(t, d + 2 * LANES), pos, pad_start, pad_len, used_tiles,
                       te1.shape[0] * tm, min(DISPATCH_ROWS, t), tm)
        z = _moe(hs, te1, te2, tvalid, w1b, w3b, w2b, lw["gpostffn"], l, tm)
        g2 = mod[l][:, 5:6, :]
        x = _combine(x1.reshape(t, d), g2, z, pos, s, tc).reshape(b, s, d)
    return x
```

```python
import functools

import jax
import jax.numpy as jnp
from jax import lax
from jax.experimental import pallas as pl
from jax.experimental.pallas import tpu as pltpu

CHUNK = 64
CONV_WIDTH = 31
D_CONV = 512
GLA_HEADS = 4
GLA_DV = 128
GLA_DK = 64
D_GLA_K = GLA_HEADS * GLA_DK
D_GLA_V = GLA_HEADS * GLA_DV
GATE_RANK = 16
GATE_TAU = 16.0
N_EXPERTS = 16
N_GROUPS = 4
EXPERTS_PER_GROUP = 4
PAIRS_PER_GROUP = 6
N_CLASSES = N_GROUPS * PAIRS_PER_GROUP
CLASS_ROWS = 32
D_EXPERT = 704
D_EXPERT_PAD = 768
EPS = 1e-6
LOG2_E = 1.4426950408889634

SUBLANES = 8
LANES = 128
MXU_TILE = 256
CONV_PAD = 32
CONV_ROWS = 32
CONV_LANES = 256
SEQ_TILE = 512
MOE_TILE = 256
COMBINE_TILE = 256
DISPATCH_ROWS = 256
DISPATCH_AHEAD = 2
DISPATCH_SLOTS = 2 * DISPATCH_AHEAD
DMA_QUEUES = 2
GATHER_SLOTS = 3

F32 = jnp.float32
BF16 = jnp.bfloat16
NT_DIMS = (((1,), (1,)), ((), ()))
TN_DIMS = (((0,), (0,)), ((), ()))


def _sigmoid(x):
    return 1.0 / (1.0 + jnp.exp2(x * -LOG2_E))


def _rms(x, g):
    ms = jnp.mean(x * x, axis=-1, keepdims=True)
    return x * lax.rsqrt(ms + EPS) * g


def _split_bf16(x):
    hi = x.astype(BF16)
    lo = (x - hi.astype(F32)).astype(BF16)
    return hi, lo


def _mod_kernel(c_ref, w_ref, b_ref, o_ref):
    c = c_ref[...]
    ca = c * _sigmoid(c)
    a_hi, a_lo = _split_bf16(ca)
    w_hi, w_lo = _split_bf16(w_ref[0])
    acc = jnp.dot(a_hi, w_hi, preferred_element_type=F32)
    acc += jnp.dot(a_lo, w_hi, preferred_element_type=F32)
    acc += jnp.dot(a_hi, w_lo, preferred_element_type=F32)
    o_ref[0] = acc + b_ref[0]


def _modulation(c, w_ada, b_ada):
    depth, d, d6 = w_ada.shape
    b = c.shape[0]
    nblk = d6 // d
    return pl.pallas_call(
        _mod_kernel,
        grid=(depth, nblk),
        in_specs=[
            pl.BlockSpec((b, d), lambda l, n: (0, 0)),
            pl.BlockSpec((1, d, d), lambda l, n: (l, 0, n)),
            pl.BlockSpec((1, 1, d), lambda l, n: (l, 0, n)),
        ],
        out_specs=pl.BlockSpec((1, b, d), lambda l, n: (l, 0, n)),
        out_shape=jax.ShapeDtypeStruct((depth, b, d6), F32),
        compiler_params=pltpu.CompilerParams(
            dimension_semantics=("arbitrary", "arbitrary"),
            vmem_limit_bytes=32 * 1024 * 1024),
        name="adaln_mod",
    )(c, w_ada, b_ada.reshape(depth, 1, d6))


def _cast_pad_kernel(w_ref, o_ref):
    rows, cols = w_ref.shape[2:]
    prow, pcol = o_ref.shape[2:]
    o_ref[0, 0, 0:rows, 0:cols] = w_ref[0, 0].astype(BF16)
    if pcol > cols:
        o_ref[0, 0, :, cols:] = jnp.zeros((prow, pcol - cols), BF16)
    if prow > rows:
        o_ref[0, 0, rows:, 0:cols] = jnp.zeros((prow - rows, cols), BF16)


def _cast_pad(w, prow, pcol):
    depth, ne, rows, cols = w.shape
    return pl.pallas_call(
        _cast_pad_kernel,
        grid=(depth, ne),
        in_specs=[pl.BlockSpec((1, 1, rows, cols), lambda l, e: (l, e, 0, 0))],
        out_specs=pl.BlockSpec((1, 1, prow, pcol), lambda l, e: (l, e, 0, 0)),
        out_shape=jax.ShapeDtypeStruct((depth, ne, prow, pcol), BF16),
        compiler_params=pltpu.CompilerParams(
            dimension_semantics=("arbitrary", "arbitrary"),
            vmem_limit_bytes=32 * 1024 * 1024),
        name="cast_pad",
    )(w)


def _route(logits_t, br):
    m = jnp.max(logits_t, axis=0, keepdims=True)
    e = jnp.exp(logits_t - m)
    probs = e / jnp.sum(e, axis=0, keepdims=True)
    sel = probs + br
    one = jnp.ones_like(m)
    zero = jnp.zeros_like(m)
    picked, gscore = [], []
    for g in range(N_GROUPS):
        rows = [sel[g * EXPERTS_PER_GROUP + k:g * EXPERTS_PER_GROUP + k + 1, :]
                for k in range(EXPERTS_PER_GROUP)]
        score = zero
        for k in range(EXPERTS_PER_GROUP):
            rank = zero
            for k2 in range(EXPERTS_PER_GROUP):
                if k2 == k:
                    continue
                ahead = (rows[k2] >= rows[k]) if k2 < k else (rows[k2] > rows[k])
                rank = rank + jnp.where(ahead, one, zero)
            pk = jnp.where(rank < 2.0, one, zero)
            picked.append(pk)
            score = score + pk * rows[k]
        gscore.append(score)
    best = gscore[0]
    for g in range(1, N_GROUPS):
        best = jnp.maximum(best, gscore[g])
    taken = zero
    flag = [zero] * EXPERTS_PER_GROUP
    prob = [zero] * EXPERTS_PER_GROUP
    gbase = zero
    for g in range(N_GROUPS):
        isb = jnp.where(gscore[g] == best, one, zero) * (one - taken)
        taken = taken + isb
        gbase = gbase + isb * float(g * PAIRS_PER_GROUP)
        for k in range(EXPERTS_PER_GROUP):
            ei = g * EXPERTS_PER_GROUP + k
            flag[k] = flag[k] + isb * picked[ei]
            prob[k] = prob[k] + isb * probs[ei:ei + 1, :]
    f0, f1, f2, f3 = flag
    pair = f0 * (f2 * 1.0 + f3 * 2.0) + (one - f0) * (f1 * (f2 * 3.0 + f3 * 4.0) + (one - f1) * 5.0)
    cls = gbase + pair
    wa_raw = f0 * prob[0] + (one - f0) * (f1 * prob[1] + (one - f1) * prob[2])
    tot = f0 * prob[0] + f1 * prob[1] + f2 * prob[2] + f3 * prob[3]
    wa = wa_raw / tot
    wb = (tot - wa_raw) / tot
    return cls, wa, wb


def _mix_kernel(x_ref, mod_ref, gpre_ref, gpost_ref, gffn_ref, wnat_ref, wt_ref, wdw_ref, bdw_ref,
                lng_ref, lnb_ref, wup_ref, bup_ref, gnorm_ref, wout_ref, wr_ref, br_ref,
                ubd_ref, ebd_ref, tri_ref,
                x1_ref, h2p_ref, route_ref,
                ubuf, ush, cbuf, st_ref, sbd_ref, ycat_ref, *, ts):
    j = pl.program_id(1)
    nch = ts // CHUNK

    @pl.when(j == 0)
    def _():
        ubuf[0:CONV_PAD, :] = jnp.zeros((CONV_PAD, D_CONV), F32)
        st_ref[...] = jnp.zeros_like(st_ref)
        sbd_ref[...] = jnp.zeros_like(sbd_ref)

    @pl.when(j > 0)
    def _():
        ubuf[0:CONV_PAD, :] = ubuf[ts:ts + CONV_PAD, :]

    x = x_ref[0]
    sh1 = mod_ref[0, 0:1, :]
    sc1 = mod_ref[0, 1:2, :]
    g1 = mod_ref[0, 2:3, :]
    sh2 = mod_ref[0, 3:4, :]
    sc2 = mod_ref[0, 4:5, :]

    h = _rms(x, gpre_ref[...] * (1.0 + sc1)) + sh1
    hb = h.astype(BF16)

    cvg = jnp.dot(hb, wnat_ref[:, 0:2 * D_CONV], preferred_element_type=F32)
    ubuf[CONV_PAD:CONV_PAD + ts, :] = cvg[:, 0:D_CONV] * _sigmoid(cvg[:, D_CONV:2 * D_CONV])

    for r in range(1, SUBLANES):
        ush[r - 1] = ubuf[r:r + ts + CONV_PAD - SUBLANES, :]

    groups = CONV_ROWS // SUBLANES
    first_off = CONV_PAD - (CONV_WIDTH - 1)

    def conv_block(rb, carry, c0):
        r0 = rb * CONV_ROWS
        cs = slice(c0, c0 + CONV_LANES)
        acc = [jnp.broadcast_to(bdw_ref[:, cs], (SUBLANES, CONV_LANES)) for _ in range(groups)]
        for r in range(SUBLANES):
            taps = [tap for tap in range(CONV_WIDTH) if (first_off + tap) % SUBLANES == r]
            shifts = [(first_off + tap) // SUBLANES for tap in taps]
            src = ubuf if r == 0 else ush.at[r - 1]
            blk = {m: src[pl.ds(r0 + m * SUBLANES, SUBLANES), cs]
                   for m in range(min(shifts), max(shifts) + groups)}
            for tap, a in zip(taps, shifts):
                w8 = wdw_ref[tap * SUBLANES:(tap + 1) * SUBLANES, cs]
                for g in range(groups):
                    acc[g] = acc[g] + w8 * blk[a + g]
        cbuf[pl.ds(r0, CONV_ROWS), cs] = jnp.concatenate(acc, axis=0)
        return carry

    for c0 in range(0, D_CONV, CONV_LANES):
        for rb in range(ts // CONV_ROWS):
            conv_block(rb, 0, c0)
    cv = cbuf[...]
    mu = jnp.mean(cv, axis=-1, keepdims=True)
    cen = cv - mu
    var = jnp.mean(cen * cen, axis=-1, keepdims=True)
    yn = cen * lax.rsqrt(var + EPS) * lng_ref[...] + lnb_ref[...]
    ycat_ref[:, 0:D_CONV] = (yn * _sigmoid(yn)).astype(BF16)

    qvo = jnp.dot(hb, wnat_ref[:, 2 * D_CONV:], preferred_element_type=F32)
    q = (qvo[:, 0:D_GLA_K] * (GLA_DK ** -0.5)).astype(BF16)
    v = qvo[:, D_GLA_K:D_GLA_K + D_GLA_V].astype(BF16)
    og = qvo[:, D_GLA_K + D_GLA_V:]
    tt = lax.dot_general(wt_ref[...], hb, NT_DIMS, preferred_element_type=F32)
    k_t = tt[0:D_GLA_K, :]
    lr_t = tt[D_GLA_K:, :].astype(BF16)
    z_t = jnp.dot(wup_ref[...], lr_t, preferred_element_type=F32) + bup_ref[...]
    la_t = (jnp.minimum(z_t, 0.0) - jnp.log(1.0 + jnp.exp(-jnp.abs(z_t)))) * (1.0 / GATE_TAU)
    la_b = la_t.astype(BF16)
    slab = ubd_ref.shape[0]
    bcum = jnp.concatenate(
        [jnp.dot(la_b[:, s0:s0 + slab], ubd_ref[...], preferred_element_type=F32)
         for s0 in range(0, ts, slab)], axis=1)
    bend = jnp.concatenate(
        [jnp.dot(la_b[:, s0:s0 + slab], ebd_ref[...], preferred_element_type=F32)
         for s0 in range(0, ts, slab)], axis=1)
    kd_t = (k_t * jnp.exp(bend - bcum)).astype(BF16)
    dec_t = jnp.exp(bend)

    st = [st_ref[hh] for hh in range(GLA_HEADS)]
    gn = gnorm_ref[...]
    for c in range(nch):
        lo, hi_ = c * CHUNK, (c + 1) * CHUNK
        kd_c = kd_t[:, lo:hi_]
        v_c = v[lo:hi_, :]
        for hh in range(GLA_HEADS):
            ks, ke = hh * GLA_DK, (hh + 1) * GLA_DK
            vs, ve = hh * GLA_DV, (hh + 1) * GLA_DV
            upd = jnp.dot(kd_c[ks:ke, :], v_c[:, vs:ve], preferred_element_type=F32)
            dec = jnp.broadcast_to(dec_t[ks:ke, lo:lo + 1], (GLA_DK, GLA_DV))
            st[hh] = dec * st[hh] + upd
            sbd_ref[ks:ke, vs:ve] = st[hh].astype(BF16)
        o_c = jnp.dot(q[lo:hi_, :], sbd_ref[...], preferred_element_type=F32)
        og_c = og[lo:hi_, :]
        for hh in range(GLA_HEADS):
            vs, ve = hh * GLA_DV, (hh + 1) * GLA_DV
            on = _rms(o_c[:, vs:ve], gn)
            gate = og_c[:, vs:ve]
            ycat_ref[lo:hi_, D_CONV + vs:D_CONV + ve] = (on * (gate * _sigmoid(gate))).astype(BF16)
    for hh in range(GLA_HEADS):
        st_ref[hh] = st[hh]

    y = jnp.dot(ycat_ref[...], wout_ref[...], preferred_element_type=F32)
    x1 = x + g1 * _rms(y, gpost_ref[...])
    x1_ref[0] = x1
    h2 = _rms(x1, gffn_ref[...] * (1.0 + sc2)) + sh2
    d = h2.shape[1]
    h2p_ref[0, :, 0:d] = h2
    h2_hi, h2_lo = _split_bf16(h2)
    lg2 = lax.dot_general(wr_ref[...], h2_hi, NT_DIMS, preferred_element_type=F32)
    lg1 = lax.dot_general(wr_ref[0:N_EXPERTS, :], h2_lo, NT_DIMS, preferred_element_type=F32)
    logits_t = lg2[0:N_EXPERTS, :] + lg2[N_EXPERTS:, :] + lg1
    cls, wa, wb = _route(logits_t, br_ref[...])
    class_ids = lax.broadcasted_iota(jnp.int32, (CLASS_ROWS, ts), 0).astype(F32)
    onehot_t = jnp.where(class_ids == cls, 1.0, 0.0)
    seen = jnp.dot(onehot_t.astype(BF16), tri_ref[...], preferred_element_type=F32)
    rank = jnp.sum(onehot_t * seen, axis=0, keepdims=True) - 1.0
    route_ref[0] = jnp.concatenate([cls, rank, jnp.zeros((SUBLANES - 2, ts), F32)], axis=0)
    h2p_ref[0, :, d:d + LANES] = jnp.transpose(jnp.broadcast_to(wa, (LANES, ts)))
    h2p_ref[0, :, d + LANES:] = jnp.transpose(jnp.broadcast_to(wb, (LANES, ts)))


def _mix(x, mod, lw, consts, ts):
    b, s, d = x.shape
    nst = s // ts
    full = lambda shape: pl.BlockSpec(shape, lambda bi, ji: (0,) * len(shape))
    in_specs = [
        pl.BlockSpec((1, ts, d), lambda bi, ji: (bi, ji, 0)),
        pl.BlockSpec((1, 6, d), lambda bi, ji: (bi, 0, 0)),
        full((1, d)), full((1, d)), full((1, d)),
        full(lw["wnat"].shape), full(lw["wt"].shape),
        full((CONV_WIDTH * SUBLANES, D_CONV)), full((1, D_CONV)), full((1, D_CONV)), full((1, D_CONV)),
        full((D_GLA_K, GATE_RANK)), full((D_GLA_K, 1)), full((1, GLA_DV)),
        full((d, d)), full((2 * N_EXPERTS, d)), full((N_EXPERTS, 1)),
        full(consts["ubd"].shape), full(consts["ebd"].shape), full((ts, ts)),
    ]
    out_specs = [
        pl.BlockSpec((1, ts, d), lambda bi, ji: (bi, ji, 0)),
        pl.BlockSpec((1, ts, d + 2 * LANES), lambda bi, ji: (bi, ji, 0)),
        pl.BlockSpec((1, 8, ts), lambda bi, ji: (bi * nst + ji, 0, 0)),
    ]
    out_shape = [
        jax.ShapeDtypeStruct((b, s, d), F32),
        jax.ShapeDtypeStruct((b, s, d + 2 * LANES), F32),
        jax.ShapeDtypeStruct((b * nst, 8, ts), F32),
    ]
    scratch = [
        pltpu.VMEM((CONV_PAD + ts, D_CONV), F32),
        pltpu.VMEM((SUBLANES - 1, CONV_PAD + ts - SUBLANES, D_CONV), F32),
        pltpu.VMEM((ts, D_CONV), F32),
        pltpu.VMEM((GLA_HEADS, GLA_DK, GLA_DV), F32),
        pltpu.VMEM((D_GLA_K, D_GLA_V), BF16),
        pltpu.VMEM((ts, d), BF16),
    ]
    return pl.pallas_call(
        functools.partial(_mix_kernel, ts=ts),
        grid=(b, nst),
        in_specs=in_specs,
        out_specs=out_specs,
        out_shape=out_shape,
        scratch_shapes=scratch,
        compiler_params=pltpu.CompilerParams(
            dimension_semantics=("arbitrary", "arbitrary"),
            vmem_limit_bytes=56 * 1024 * 1024),
        name="mix",
    )(x, mod, lw["gpre"], lw["gpost"], lw["gffn"], lw["wnat"], lw["wt"], lw["wdw"], lw["bdw"],
      lw["lng"], lw["lnb"], lw["wup"], lw["bup"], lw["gnorm"], lw["wout"], consts["wr"],
      consts["br"], consts["ubd"], consts["ebd"], consts["tri"])


def _row_copy(idx_ref, base, r, src_hbm, dst_vmem, sem):
    tok = idx_ref[base + r]
    return pltpu.make_async_copy(src_hbm.at[pl.ds(tok, 1), :], dst_vmem.at[pl.ds(r, 1), :], sem)


def _row_gather_start(idx_ref, base, src_hbm, dst_vmem, sem, n_rows, unrolled):
    if unrolled:
        for r in range(n_rows):
            _row_copy(idx_ref, base, r, src_hbm, dst_vmem, sem).start(priority=r % DMA_QUEUES)
    else:
        def body(r, carry):
            _row_copy(idx_ref, base, r, src_hbm, dst_vmem, sem).start()
            return carry
        lax.fori_loop(0, n_rows, body, 0)


def _row_gather_wait(src_hbm, dst_vmem, sem, n_rows):
    pltpu.make_async_copy(src_hbm.at[pl.ds(0, n_rows), :], dst_vmem, sem).wait()


def _gather_pipeline(idx_ref, src_hbm, buf, sems, n_rows):
    i = pl.program_id(0)
    n = pl.num_programs(0)

    @pl.when(i == 0)
    def _():
        for k in range(GATHER_SLOTS - 1):
            @pl.when(k < n)
            def _():
                _row_gather_start(idx_ref, k * n_rows, src_hbm, buf.at[k], sems.at[k], n_rows, unrolled=False)

    ahead = i + GATHER_SLOTS - 1

    @pl.when(ahead < n)
    def _():
        aslot = lax.rem(ahead, GATHER_SLOTS)
        _row_gather_start(idx_ref, ahead * n_rows, src_hbm, buf.at[aslot], sems.at[aslot], n_rows,
                          unrolled=True)

    slot = lax.rem(i, GATHER_SLOTS)
    _row_gather_wait(src_hbm, buf.at[slot], sems.at[slot], n_rows)
    return slot


def _moe_kernel(te1_ref, te2_ref, tvalid_ref,
                hs_ref, w1a_ref, w3a_ref, w2a_ref, w1b_ref, w3b_ref, w2b_ref, gpost_ref,
                z_ref):
    i = pl.program_id(0)
    d = z_ref.shape[1]

    @pl.when(tvalid_ref[i] == 0)
    def _():
        z_ref[...] = jnp.zeros_like(z_ref)

    @pl.when(tvalid_ref[i] != 0)
    def _():
        h = hs_ref[:, 0:d].astype(BF16)
        y = None
        for e, (w1, w3, w2) in enumerate(((w1a_ref, w3a_ref, w2a_ref), (w1b_ref, w3b_ref, w2b_ref))):
            g_t = lax.dot_general(w1[0, 0], h, NT_DIMS, preferred_element_type=F32)
            u_t = lax.dot_general(w3[0, 0], h, NT_DIMS, preferred_element_type=F32)
            he_t = ((g_t * _sigmoid(g_t)) * u_t).astype(BF16)
            ye = lax.dot_general(he_t, w2[0, 0], TN_DIMS, preferred_element_type=F32)
            wcol = hs_ref[:, d + e * LANES:d + (e + 1) * LANES]
            ye = ye * jnp.concatenate([wcol] * (d // LANES), axis=1)
            y = ye if y is None else y + ye
        z_ref[...] = _rms(y, gpost_ref[...])


def _moe(hs, te1, te2, tvalid, w1, w3, w2, gpost, layer, tm):
    dx = hs.shape[1]
    d = dx - 2 * LANES
    nt = te1.shape[0]
    wspec_a = lambda shape: pl.BlockSpec((1, 1) + shape, lambda i, e1, e2, tv: (layer, e1[i], 0, 0))
    wspec_b = lambda shape: pl.BlockSpec((1, 1) + shape, lambda i, e1, e2, tv: (layer, e2[i], 0, 0))
    grid_spec = pltpu.PrefetchScalarGridSpec(
        num_scalar_prefetch=3,
        grid=(nt,),
        in_specs=[
            pl.BlockSpec((tm, dx), lambda i, e1, e2, tv: (i, 0)),
            wspec_a((D_EXPERT_PAD, d)), wspec_a((D_EXPERT_PAD, d)), wspec_a((D_EXPERT_PAD, d)),
            wspec_b((D_EXPERT_PAD, d)), wspec_b((D_EXPERT_PAD, d)), wspec_b((D_EXPERT_PAD, d)),
            pl.BlockSpec((1, d), lambda i, e1, e2, tv: (0, 0)),
        ],
        out_specs=pl.BlockSpec((tm, d), lambda i, e1, e2, tv: (i, 0)),
    )
    return pl.pallas_call(
        _moe_kernel,
        grid_spec=grid_spec,
        out_shape=jax.ShapeDtypeStruct((nt * tm, d), F32),
        compiler_params=pltpu.CompilerParams(
            dimension_semantics=("arbitrary",),
            vmem_limit_bytes=48 * 1024 * 1024),
        name="moe",
    )(te1, te2, tvalid, hs, w1, w3, w2, w1, w3, w2, gpost)


def _combine_kernel(pos_ref, x1_ref, g2_ref, z_hbm, o_ref, zg, sems, *, tc):
    slot = _gather_pipeline(pos_ref, z_hbm, zg, sems, tc)
    o_ref[...] = x1_ref[...] + g2_ref[0] * zg[slot]


def _combine(x1, g2, z, pos, seq_len, tc):
    t, d = x1.shape
    per_seq = seq_len // tc
    grid_spec = pltpu.PrefetchScalarGridSpec(
        num_scalar_prefetch=1,
        grid=(t // tc,),
        in_specs=[
            pl.BlockSpec((tc, d), lambda i, pos: (i, 0)),
            pl.BlockSpec((1, 1, d), lambda i, pos: (i // per_seq, 0, 0)),
            pl.BlockSpec(memory_space=pl.ANY),
        ],
        out_specs=pl.BlockSpec((tc, d), lambda i, pos: (i, 0)),
        scratch_shapes=[
            pltpu.VMEM((GATHER_SLOTS, tc, d), F32),
            pltpu.SemaphoreType.DMA((GATHER_SLOTS,)),
        ],
    )
    return pl.pallas_call(
        functools.partial(_combine_kernel, tc=tc),
        grid_spec=grid_spec,
        out_shape=jax.ShapeDtypeStruct((t, d), F32),
        compiler_params=pltpu.CompilerParams(
            dimension_semantics=("arbitrary",),
            vmem_limit_bytes=32 * 1024 * 1024),
        name="combine",
    )(pos, x1, g2, z)


def _class_tables():
    e1, e2 = [], []
    for g in range(N_GROUPS):
        for a in range(EXPERTS_PER_GROUP):
            for b in range(a + 1, EXPERTS_PER_GROUP):
                e1.append(g * EXPERTS_PER_GROUP + a)
                e2.append(g * EXPERTS_PER_GROUP + b)
    return jnp.asarray(e1, jnp.int32), jnp.asarray(e2, jnp.int32)


def _dispatch_kernel(pos_ref, pad_start_ref, pad_len_ref, used_ref, h_hbm, zero_ref, hs_hbm,
                     buf, in_sems, row_sems, pad_sem, *, rows, tm):
    i = pl.program_id(0)
    n = pl.num_programs(0)
    n_tiles = hs_hbm.shape[0] // tm

    def for_each_pad_row(fn):
        for c in range(N_CLASSES):
            def body(j, carry, c=c):
                fn(pltpu.make_async_copy(zero_ref.at[pl.ds(0, 1), :],
                                         hs_hbm.at[pl.ds(pad_start_ref[c] + j, 1), :], pad_sem))
                return carry
            lax.fori_loop(0, pad_len_ref[c], body, 0)

        def tile_body(k, carry):
            fn(pltpu.make_async_copy(zero_ref, hs_hbm.at[pl.ds(pl.multiple_of(k * tm, tm), tm), :], pad_sem))
            return carry
        lax.fori_loop(used_ref[0], n_tiles, tile_body, 0)

    def block_in(blk, slot):
        start = pl.multiple_of(blk * rows, rows)
        return pltpu.make_async_copy(h_hbm.at[pl.ds(start, rows), :], buf.at[slot], in_sems.at[slot])

    def rows_wait(slot):
        pltpu.make_async_copy(buf.at[slot], hs_hbm.at[pl.ds(0, rows), :], row_sems.at[slot]).wait()

    @pl.when(i == 0)
    def _():
        for_each_pad_row(lambda cp: cp.start())
        for k in range(DISPATCH_AHEAD):
            @pl.when(k < n)
            def _():
                block_in(k, k).start()

    @pl.when(i >= DISPATCH_AHEAD)
    def _():
        rows_wait(lax.rem(i - DISPATCH_AHEAD, DISPATCH_SLOTS))

    @pl.when(i + DISPATCH_AHEAD < n)
    def _():
        block_in(i + DISPATCH_AHEAD, lax.rem(i + DISPATCH_AHEAD, DISPATCH_SLOTS)).start()

    slot = lax.rem(i, DISPATCH_SLOTS)
    block_in(i, slot).wait()
    base = i * rows
    for u in range(rows):
        pltpu.make_async_copy(buf.at[slot, pl.ds(u, 1), :], hs_hbm.at[pl.ds(pos_ref[base + u], 1), :],
                              row_sems.at[slot]).start()

    @pl.when(i == n - 1)
    def _():
        for k in range(DISPATCH_AHEAD):
            @pl.when(i - k >= 0)
            def _():
                rows_wait(lax.rem(i - k, DISPATCH_SLOTS))
        for_each_pad_row(lambda cp: cp.wait())


def _dispatch(h2p, pos, pad_start, pad_len, used_tiles, n_rows, rows, tm):
    t, dx = h2p.shape
    grid_spec = pltpu.PrefetchScalarGridSpec(
        num_scalar_prefetch=4,
        grid=(t // rows,),
        in_specs=[pl.BlockSpec(memory_space=pl.ANY),
                  pl.BlockSpec((tm, dx), lambda i, *_: (0, 0))],
        out_specs=pl.BlockSpec(memory_space=pl.ANY),
        scratch_shapes=[pltpu.VMEM((DISPATCH_SLOTS, rows, dx), F32),
                        pltpu.SemaphoreType.DMA((DISPATCH_SLOTS,)),
                        pltpu.SemaphoreType.DMA((DISPATCH_SLOTS,)),
                        pltpu.SemaphoreType.DMA],
    )
    return pl.pallas_call(
        functools.partial(_dispatch_kernel, rows=rows, tm=tm),
        grid_spec=grid_spec,
        out_shape=jax.ShapeDtypeStruct((n_rows, dx), F32),
        compiler_params=pltpu.CompilerParams(dimension_semantics=("arbitrary",)),
        name="dispatch",
    )(pos, pad_start, pad_len, used_tiles, h2p, jnp.zeros((tm, dx), F32))


def _plan(cls, rank, tm):
    t = cls.size
    nt = t // tm + N_CLASSES
    onehot = (cls[:, :, None] == jnp.arange(N_CLASSES, dtype=jnp.int32)[None, None, :]).astype(jnp.int32)
    per_tile = jnp.sum(onehot, axis=1)
    before = jnp.cumsum(per_tile, axis=0) - per_tile
    counts = jnp.sum(per_tile, axis=0)
    tiles = (counts + tm - 1) // tm
    tile_end = jnp.cumsum(tiles)
    tile_start = tile_end - tiles
    base = tile_start[None, :] * tm + before
    pos = (jnp.sum(onehot * base[:, None, :], axis=2) + rank).reshape(t)
    tile_ids = jnp.arange(nt, dtype=jnp.int32)
    tile_cls = jnp.sum((tile_end[None, :] <= tile_ids[:, None]).astype(jnp.int32), axis=1)
    tvalid = (tile_ids < tile_end[-1]).astype(jnp.int32)
    last_cls = jnp.max(jnp.where(counts > 0, jnp.arange(N_CLASSES, dtype=jnp.int32), 0))
    tile_cls = jnp.where(tvalid == 1, jnp.minimum(tile_cls, N_CLASSES - 1), last_cls)
    ce1, ce2 = _class_tables()
    pad_start = (tile_start * tm + counts).astype(jnp.int32)
    pad_len = (tiles * tm - counts).astype(jnp.int32)
    used_tiles = tile_end[-1:].astype(jnp.int32)
    return pos.astype(jnp.int32), pad_start, pad_len, used_tiles, ce1[tile_cls], ce2[tile_cls], tvalid


def _layer_weights(l, g_pre_mix, g_post_mix, g_pre_ffn, g_post_ffn, w_in, w_dw, b_dw, conv_ln_g,
                   conv_ln_b, w_gate_up, b_gate, gla_norm_g, w_out):
    d = w_in.shape[1]
    wi = w_in[l]
    o_q = 2 * D_CONV
    o_k = o_q + D_GLA_K
    o_v = o_k + D_GLA_K
    o_g = o_v + D_GLA_V
    o_lr = o_g + D_GLA_V
    wnat = jnp.concatenate([wi[:, 0:o_q], wi[:, o_q:o_k], wi[:, o_v:o_g], wi[:, o_g:o_lr]], axis=1)
    wt = jnp.concatenate([wi[:, o_k:o_v], wi[:, o_lr:]], axis=1).T
    return dict(
        gpre=g_pre_mix[l].reshape(1, d), gpost=g_post_mix[l].reshape(1, d),
        gffn=g_pre_ffn[l].reshape(1, d), gpostffn=g_post_ffn[l].reshape(1, d),
        wnat=wnat.astype(BF16), wt=wt.astype(BF16),
        wdw=jnp.repeat(w_dw[l], SUBLANES, axis=0), bdw=b_dw[l].reshape(1, D_CONV),
        lng=conv_ln_g[l].reshape(1, D_CONV), lnb=conv_ln_b[l].reshape(1, D_CONV),
        wup=w_gate_up[l].T.astype(BF16), bup=b_gate[l].reshape(D_GLA_K, 1),
        gnorm=gla_norm_g[l].reshape(1, GLA_DV),
        wout=w_out[l].astype(BF16),
    )


def kernel(x, c, w_ada, b_ada, g_pre_mix, g_post_mix, g_pre_ffn, g_post_ffn, w_in, w_dw, b_dw,
           conv_ln_g, conv_ln_b, w_gate_up, b_gate, gla_norm_g, w_out, w_router, b_router, w1, w3, w2):
    b, s, d = x.shape
    depth = w_ada.shape[0]
    t = b * s
    ts = min(SEQ_TILE, s)
    tm = min(MOE_TILE, t)
    tc = min(COMBINE_TILE, s)
    assert s % ts == 0 and ts % CHUNK == 0 and t % tm == 0 and s % tc == 0

    mod = _modulation(c, w_ada, b_ada).reshape(depth, b, 6, d)

    wr_hi, wr_lo = _split_bf16(w_router.T)
    tok = jnp.arange(min(ts, MXU_TILE), dtype=jnp.int32)
    same = (tok[:, None] // CHUNK) == (tok[None, :] // CHUNK)
    consts = dict(
        wr=jnp.concatenate([wr_hi, wr_lo], axis=0),
        br=b_router.reshape(N_EXPERTS, 1),
        ubd=(same & (tok[:, None] <= tok[None, :])).astype(BF16),
        ebd=same.astype(BF16),
        tri=(jnp.arange(ts)[:, None] <= jnp.arange(ts)[None, :]).astype(BF16),
    )

    w1b = _cast_pad(jnp.swapaxes(w1, 2, 3), D_EXPERT_PAD, d)
    w3b = _cast_pad(jnp.swapaxes(w3, 2, 3), D_EXPERT_PAD, d)
    w2b = _cast_pad(w2, D_EXPERT_PAD, d)

    for l in range(depth):
        lw = _layer_weights(l, g_pre_mix, g_post_mix, g_pre_ffn, g_post_ffn, w_in, w_dw, b_dw,
                            conv_ln_g, conv_ln_b, w_gate_up, b_gate, gla_norm_g, w_out)
        x1, h2p, route = _mix(x, mod[l], lw, consts, ts)
        cls = route[:, 0, :].astype(jnp.int32)
        rank = route[:, 1, :].astype(jnp.int32)
        pos, pad_start, pad_len, used_tiles, te1, te2, tvalid = _plan(cls, rank, tm)
        hs = _dispatch(h2p.reshape(t, d + 2 * LANES), pos, pad_start, pad_len, used_tiles,
                       te1.shape[0] * tm, min(DISPATCH_ROWS, t), tm)
        z = _moe(hs, te1, te2, tvalid, w1b, w3b, w2b, lw["gpostffn"], l, tm)
        g2 = mod[l][:, 5:6, :]
        x = _combine(x1.reshape(t, d), g2, z, pos, s, tc).reshape(b, s, d)
    return x
```

```python
import functools

import jax
import jax.numpy as jnp
from jax import lax
from jax.experimental import pallas as pl
from jax.experimental.pallas import tpu as pltpu

CHUNK = 64
CONV_WIDTH = 31
D_CONV = 512
GLA_HEADS = 4
GLA_DV = 128
GLA_DK = 64
D_GLA_K = GLA_HEADS * GLA_DK
D_GLA_V = GLA_HEADS * GLA_DV
GATE_RANK = 16
GATE_TAU = 16.0
N_EXPERTS = 16
N_GROUPS = 4
EXPERTS_PER_GROUP = 4
PAIRS_PER_GROUP = 6
N_CLASSES = N_GROUPS * PAIRS_PER_GROUP
CLASS_ROWS = 32
D_EXPERT = 704
D_EXPERT_PAD = 768
EPS = 1e-6
LOG2_E = 1.4426950408889634

SUBLANES = 8
LANES = 128
MXU_TILE = 256
CONV_PAD = 32
CONV_ROWS = 32
CONV_LANES = 256
SEQ_TILE = 512
MOE_TILE = 256
COMBINE_TILE = 512
DISPATCH_ROWS = 256
DISPATCH_AHEAD = 2
DISPATCH_SLOTS = 2 * DISPATCH_AHEAD
DMA_QUEUES = 2
GATHER_SLOTS = 3

F32 = jnp.float32
BF16 = jnp.bfloat16
NT_DIMS = (((1,), (1,)), ((), ()))
TN_DIMS = (((0,), (0,)), ((), ()))


def _sigmoid(x):
    return 1.0 / (1.0 + jnp.exp2(x * -LOG2_E))


def _rms(x, g):
    ms = jnp.mean(x * x, axis=-1, keepdims=True)
    return x * lax.rsqrt(ms + EPS) * g


def _split_bf16(x):
    hi = x.astype(BF16)
    lo = (x - hi.astype(F32)).astype(BF16)
    return hi, lo


def _mod_kernel(c_ref, w_ref, b_ref, o_ref):
    c = c_ref[...]
    ca = c * _sigmoid(c)
    a_hi, a_lo = _split_bf16(ca)
    w_hi, w_lo = _split_bf16(w_ref[0])
    acc = jnp.dot(a_hi, w_hi, preferred_element_type=F32)
    acc += jnp.dot(a_lo, w_hi, preferred_element_type=F32)
    acc += jnp.dot(a_hi, w_lo, preferred_element_type=F32)
    o_ref[0] = acc + b_ref[0]


def _modulation(c, w_ada, b_ada):
    depth, d, d6 = w_ada.shape
    b = c.shape[0]
    nblk = d6 // d
    return pl.pallas_call(
        _mod_kernel,
        grid=(depth, nblk),
        in_specs=[
            pl.BlockSpec((b, d), lambda l, n: (0, 0)),
            pl.BlockSpec((1, d, d), lambda l, n: (l, 0, n)),
            pl.BlockSpec((1, 1, d), lambda l, n: (l, 0, n)),
        ],
        out_specs=pl.BlockSpec((1, b, d), lambda l, n: (l, 0, n)),
        out_shape=jax.ShapeDtypeStruct((depth, b, d6), F32),
        compiler_params=pltpu.CompilerParams(
            dimension_semantics=("arbitrary", "arbitrary"),
            vmem_limit_bytes=32 * 1024 * 1024),
        name="adaln_mod",
    )(c, w_ada, b_ada.reshape(depth, 1, d6))


def _cast_pad_kernel(w_ref, o_ref):
    rows, cols = w_ref.shape[2:]
    prow, pcol = o_ref.shape[2:]
    o_ref[0, 0, 0:rows, 0:cols] = w_ref[0, 0].astype(BF16)
    if pcol > cols:
        o_ref[0, 0, :, cols:] = jnp.zeros((prow, pcol - cols), BF16)
    if prow > rows:
        o_ref[0, 0, rows:, 0:cols] = jnp.zeros((prow - rows, cols), BF16)


def _cast_pad(w, prow, pcol):
    depth, ne, rows, cols = w.shape
    return pl.pallas_call(
        _cast_pad_kernel,
        grid=(depth, ne),
        in_specs=[pl.BlockSpec((1, 1, rows, cols), lambda l, e: (l, e, 0, 0))],
        out_specs=pl.BlockSpec((1, 1, prow, pcol), lambda l, e: (l, e, 0, 0)),
        out_shape=jax.ShapeDtypeStruct((depth, ne, prow, pcol), BF16),
        compiler_params=pltpu.CompilerParams(
            dimension_semantics=("arbitrary", "arbitrary"),
            vmem_limit_bytes=32 * 1024 * 1024),
        name="cast_pad",
    )(w)


def _route(logits_t, br):
    m = jnp.max(logits_t, axis=0, keepdims=True)
    e = jnp.exp(logits_t - m)
    probs = e / jnp.sum(e, axis=0, keepdims=True)
    sel = probs + br
    one = jnp.ones_like(m)
    zero = jnp.zeros_like(m)
    picked, gscore = [], []
    for g in range(N_GROUPS):
        rows = [sel[g * EXPERTS_PER_GROUP + k:g * EXPERTS_PER_GROUP + k + 1, :]
                for k in range(EXPERTS_PER_GROUP)]
        score = zero
        for k in range(EXPERTS_PER_GROUP):
            rank = zero
            for k2 in range(EXPERTS_PER_GROUP):
                if k2 == k:
                    continue
                ahead = (rows[k2] >= rows[k]) if k2 < k else (rows[k2] > rows[k])
                rank = rank + jnp.where(ahead, one, zero)
            pk = jnp.where(rank < 2.0, one, zero)
            picked.append(pk)
            score = score + pk * rows[k]
        gscore.append(score)
    best = gscore[0]
    for g in range(1, N_GROUPS):
        best = jnp.maximum(best, gscore[g])
    taken = zero
    flag = [zero] * EXPERTS_PER_GROUP
    prob = [zero] * EXPERTS_PER_GROUP
    gbase = zero
    for g in range(N_GROUPS):
        isb = jnp.where(gscore[g] == best, one, zero) * (one - taken)
        taken = taken + isb
        gbase = gbase + isb * float(g * PAIRS_PER_GROUP)
        for k in range(EXPERTS_PER_GROUP):
            ei = g * EXPERTS_PER_GROUP + k
            flag[k] = flag[k] + isb * picked[ei]
            prob[k] = prob[k] + isb * probs[ei:ei + 1, :]
    f0, f1, f2, f3 = flag
    pair = f0 * (f2 * 1.0 + f3 * 2.0) + (one - f0) * (f1 * (f2 * 3.0 + f3 * 4.0) + (one - f1) * 5.0)
    cls = gbase + pair
    wa_raw = f0 * prob[0] + (one - f0) * (f1 * prob[1] + (one - f1) * prob[2])
    tot = f0 * prob[0] + f1 * prob[1] + f2 * prob[2] + f3 * prob[3]
    wa = wa_raw / tot
    wb = (tot - wa_raw) / tot
    return cls, wa, wb


def _mix_kernel(x_ref, mod_ref, gpre_ref, gpost_ref, gffn_ref, wnat_ref, wt_ref, wdw_ref, bdw_ref,
                lng_ref, lnb_ref, wup_ref, bup_ref, gnorm_ref, wout_ref, wr_ref, br_ref,
                ubd_ref, ebd_ref, tri_ref,
                x1_ref, h2p_ref, route_ref,
                ubuf, ush, cbuf, st_ref, sbd_ref, ycat_ref, *, ts):
    j = pl.program_id(1)
    nch = ts // CHUNK

    @pl.when(j == 0)
    def _():
        ubuf[0:CONV_PAD, :] = jnp.zeros((CONV_PAD, D_CONV), F32)
        st_ref[...] = jnp.zeros_like(st_ref)
        sbd_ref[...] = jnp.zeros_like(sbd_ref)

    @pl.when(j > 0)
    def _():
        ubuf[0:CONV_PAD, :] = ubuf[ts:ts + CONV_PAD, :]

    x = x_ref[0]
    sh1 = mod_ref[0, 0:1, :]
    sc1 = mod_ref[0, 1:2, :]
    g1 = mod_ref[0, 2:3, :]
    sh2 = mod_ref[0, 3:4, :]
    sc2 = mod_ref[0, 4:5, :]

    h = _rms(x, gpre_ref[...] * (1.0 + sc1)) + sh1
    hb = h.astype(BF16)

    cvg = jnp.dot(hb, wnat_ref[:, 0:2 * D_CONV], preferred_element_type=F32)
    ubuf[CONV_PAD:CONV_PAD + ts, :] = cvg[:, 0:D_CONV] * _sigmoid(cvg[:, D_CONV:2 * D_CONV])

    for r in range(1, SUBLANES):
        ush[r - 1] = ubuf[r:r + ts + CONV_PAD - SUBLANES, :]

    groups = CONV_ROWS // SUBLANES
    first_off = CONV_PAD - (CONV_WIDTH - 1)

    def conv_block(rb, carry, c0):
        r0 = rb * CONV_ROWS
        cs = slice(c0, c0 + CONV_LANES)
        acc = [jnp.broadcast_to(bdw_ref[:, cs], (SUBLANES, CONV_LANES)) for _ in range(groups)]
        for r in range(SUBLANES):
            taps = [tap for tap in range(CONV_WIDTH) if (first_off + tap) % SUBLANES == r]
            shifts = [(first_off + tap) // SUBLANES for tap in taps]
            src = ubuf if r == 0 else ush.at[r - 1]
            blk = {m: src[pl.ds(r0 + m * SUBLANES, SUBLANES), cs]
                   for m in range(min(shifts), max(shifts) + groups)}
            for tap, a in zip(taps, shifts):
                w8 = wdw_ref[tap * SUBLANES:(tap + 1) * SUBLANES, cs]
                for g in range(groups):
                    acc[g] = acc[g] + w8 * blk[a + g]
        cbuf[pl.ds(r0, CONV_ROWS), cs] = jnp.concatenate(acc, axis=0)
        return carry

    for c0 in range(0, D_CONV, CONV_LANES):
        for rb in range(ts // CONV_ROWS):
            conv_block(rb, 0, c0)
    cv = cbuf[...]
    mu = jnp.mean(cv, axis=-1, keepdims=True)
    cen = cv - mu
    var = jnp.mean(cen * cen, axis=-1, keepdims=True)
    yn = cen * lax.rsqrt(var + EPS) * lng_ref[...] + lnb_ref[...]
    ycat_ref[:, 0:D_CONV] = (yn * _sigmoid(yn)).astype(BF16)

    qvo = jnp.dot(hb, wnat_ref[:, 2 * D_CONV:], preferred_element_type=F32)
    q = (qvo[:, 0:D_GLA_K] * (GLA_DK ** -0.5)).astype(BF16)
    v = qvo[:, D_GLA_K:D_GLA_K + D_GLA_V].astype(BF16)
    og = qvo[:, D_GLA_K + D_GLA_V:]
    tt = lax.dot_general(wt_ref[...], hb, NT_DIMS, preferred_element_type=F32)
    k_t = tt[0:D_GLA_K, :]
    lr_t = tt[D_GLA_K:, :].astype(BF16)
    z_t = jnp.dot(wup_ref[...], lr_t, preferred_element_type=F32) + bup_ref[...]
    la_t = (jnp.minimum(z_t, 0.0) - jnp.log(1.0 + jnp.exp(-jnp.abs(z_t)))) * (1.0 / GATE_TAU)
    la_b = la_t.astype(BF16)
    slab = ubd_ref.shape[0]
    bcum = jnp.concatenate(
        [jnp.dot(la_b[:, s0:s0 + slab], ubd_ref[...], preferred_element_type=F32)
         for s0 in range(0, ts, slab)], axis=1)
    bend = jnp.concatenate(
        [jnp.dot(la_b[:, s0:s0 + slab], ebd_ref[...], preferred_element_type=F32)
         for s0 in range(0, ts, slab)], axis=1)
    kd_t = (k_t * jnp.exp(bend - bcum)).astype(BF16)
    dec_t = jnp.exp(bend)

    st = [st_ref[hh] for hh in range(GLA_HEADS)]
    gn = gnorm_ref[...]
    for c in range(nch):
        lo, hi_ = c * CHUNK, (c + 1) * CHUNK
        kd_c = kd_t[:, lo:hi_]
        v_c = v[lo:hi_, :]
        for hh in range(GLA_HEADS):
            ks, ke = hh * GLA_DK, (hh + 1) * GLA_DK
            vs, ve = hh * GLA_DV, (hh + 1) * GLA_DV
            upd = jnp.dot(kd_c[ks:ke, :], v_c[:, vs:ve], preferred_element_type=F32)
            dec = jnp.broadcast_to(dec_t[ks:ke, lo:lo + 1], (GLA_DK, GLA_DV))
            st[hh] = dec * st[hh] + upd
            sbd_ref[ks:ke, vs:ve] = st[hh].astype(BF16)
        o_c = jnp.dot(q[lo:hi_, :], sbd_ref[...], preferred_element_type=F32)
        og_c = og[lo:hi_, :]
        for hh in range(GLA_HEADS):
            vs, ve = hh * GLA_DV, (hh + 1) * GLA_DV
            on = _rms(o_c[:, vs:ve], gn)
            gate = og_c[:, vs:ve]
            ycat_ref[lo:hi_, D_CONV + vs:D_CONV + ve] = (on * (gate * _sigmoid(gate))).astype(BF16)
    for hh in range(GLA_HEADS):
        st_ref[hh] = st[hh]

    y = jnp.dot(ycat_ref[...], wout_ref[...], preferred_element_type=F32)
    x1 = x + g1 * _rms(y, gpost_ref[...])
    x1_ref[0] = x1
    h2 = _rms(x1, gffn_ref[...] * (1.0 + sc2)) + sh2
    d = h2.shape[1]
    h2p_ref[0, :, 0:d] = h2
    h2_hi, h2_lo = _split_bf16(h2)
    lg2 = lax.dot_general(wr_ref[...], h2_hi, NT_DIMS, preferred_element_type=F32)
    lg1 = lax.dot_general(wr_ref[0:N_EXPERTS, :], h2_lo, NT_DIMS, preferred_element_type=F32)
    logits_t = lg2[0:N_EXPERTS, :] + lg2[N_EXPERTS:, :] + lg1
    cls, wa, wb = _route(logits_t, br_ref[...])
    class_ids = lax.broadcasted_iota(jnp.int32, (CLASS_ROWS, ts), 0).astype(F32)
    onehot_t = jnp.where(class_ids == cls, 1.0, 0.0)
    seen = jnp.dot(onehot_t.astype(BF16), tri_ref[...], preferred_element_type=F32)
    rank = jnp.sum(onehot_t * seen, axis=0, keepdims=True) - 1.0
    route_ref[0] = jnp.concatenate([cls, rank, jnp.zeros((SUBLANES - 2, ts), F32)], axis=0)
    h2p_ref[0, :, d:d + LANES] = jnp.transpose(jnp.broadcast_to(wa, (LANES, ts)))
    h2p_ref[0, :, d + LANES:] = jnp.transpose(jnp.broadcast_to(wb, (LANES, ts)))


def _mix(x, mod, lw, consts, ts):
    b, s, d = x.shape
    nst = s // ts
    full = lambda shape: pl.BlockSpec(shape, lambda bi, ji: (0,) * len(shape))
    in_specs = [
        pl.BlockSpec((1, ts, d), lambda bi, ji: (bi, ji, 0)),
        pl.BlockSpec((1, 6, d), lambda bi, ji: (bi, 0, 0)),
        full((1, d)), full((1, d)), full((1, d)),
        full(lw["wnat"].shape), full(lw["wt"].shape),
        full((CONV_WIDTH * SUBLANES, D_CONV)), full((1, D_CONV)), full((1, D_CONV)), full((1, D_CONV)),
        full((D_GLA_K, GATE_RANK)), full((D_GLA_K, 1)), full((1, GLA_DV)),
        full((d, d)), full((2 * N_EXPERTS, d)), full((N_EXPERTS, 1)),
        full(consts["ubd"].shape), full(consts["ebd"].shape), full((ts, ts)),
    ]
    out_specs = [
        pl.BlockSpec((1, ts, d), lambda bi, ji: (bi, ji, 0)),
        pl.BlockSpec((1, ts, d + 2 * LANES), lambda bi, ji: (bi, ji, 0)),
        pl.BlockSpec((1, 8, ts), lambda bi, ji: (bi * nst + ji, 0, 0)),
    ]
    out_shape = [
        jax.ShapeDtypeStruct((b, s, d), F32),
        jax.ShapeDtypeStruct((b, s, d + 2 * LANES), F32),
        jax.ShapeDtypeStruct((b * nst, 8, ts), F32),
    ]
    scratch = [
        pltpu.VMEM((CONV_PAD + ts, D_CONV), F32),
        pltpu.VMEM((SUBLANES - 1, CONV_PAD + ts - SUBLANES, D_CONV), F32),
        pltpu.VMEM((ts, D_CONV), F32),
        pltpu.VMEM((GLA_HEADS, GLA_DK, GLA_DV), F32),
        pltpu.VMEM((D_GLA_K, D_GLA_V), BF16),
        pltpu.VMEM((ts, d), BF16),
    ]
    return pl.pallas_call(
        functools.partial(_mix_kernel, ts=ts),
        grid=(b, nst),
        in_specs=in_specs,
        out_specs=out_specs,
        out_shape=out_shape,
        scratch_shapes=scratch,
        compiler_params=pltpu.CompilerParams(
            dimension_semantics=("arbitrary", "arbitrary"),
            vmem_limit_bytes=56 * 1024 * 1024),
        name="mix",
    )(x, mod, lw["gpre"], lw["gpost"], lw["gffn"], lw["wnat"], lw["wt"], lw["wdw"], lw["bdw"],
      lw["lng"], lw["lnb"], lw["wup"], lw["bup"], lw["gnorm"], lw["wout"], consts["wr"],
      consts["br"], consts["ubd"], consts["ebd"], consts["tri"])


def _row_copy(idx_ref, base, r, src_hbm, dst_vmem, sem):
    tok = idx_ref[base + r]
    return pltpu.make_async_copy(src_hbm.at[pl.ds(tok, 1), :], dst_vmem.at[pl.ds(r, 1), :], sem)


def _row_gather_start(idx_ref, base, src_hbm, dst_vmem, sem, n_rows, unrolled):
    if unrolled:
        for r in range(n_rows):
            _row_copy(idx_ref, base, r, src_hbm, dst_vmem, sem).start(priority=r % DMA_QUEUES)
    else:
        def body(r, carry):
            _row_copy(idx_ref, base, r, src_hbm, dst_vmem, sem).start()
            return carry
        lax.fori_loop(0, n_rows, body, 0)


def _row_gather_wait(src_hbm, dst_vmem, sem, n_rows):
    pltpu.make_async_copy(src_hbm.at[pl.ds(0, n_rows), :], dst_vmem, sem).wait()


def _gather_pipeline(idx_ref, src_hbm, buf, sems, n_rows):
    i = pl.program_id(0)
    n = pl.num_programs(0)

    @pl.when(i == 0)
    def _():
        for k in range(GATHER_SLOTS - 1):
            @pl.when(k < n)
            def _():
                _row_gather_start(idx_ref, k * n_rows, src_hbm, buf.at[k], sems.at[k], n_rows, unrolled=False)

    ahead = i + GATHER_SLOTS - 1

    @pl.when(ahead < n)
    def _():
        aslot = lax.rem(ahead, GATHER_SLOTS)
        _row_gather_start(idx_ref, ahead * n_rows, src_hbm, buf.at[aslot], sems.at[aslot], n_rows,
                          unrolled=True)

    slot = lax.rem(i, GATHER_SLOTS)
    _row_gather_wait(src_hbm, buf.at[slot], sems.at[slot], n_rows)
    return slot


def _moe_kernel(te1_ref, te2_ref, tvalid_ref,
                hs_ref, w1a_ref, w3a_ref, w2a_ref, w1b_ref, w3b_ref, w2b_ref, gpost_ref,
                z_ref):
    i = pl.program_id(0)
    d = z_ref.shape[1]

    @pl.when(tvalid_ref[i] == 0)
    def _():
        z_ref[...] = jnp.zeros_like(z_ref)

    @pl.when(tvalid_ref[i] != 0)
    def _():
        h = hs_ref[:, 0:d].astype(BF16)
        y = None
        for e, (w1, w3, w2) in enumerate(((w1a_ref, w3a_ref, w2a_ref), (w1b_ref, w3b_ref, w2b_ref))):
            g_t = lax.dot_general(w1[0, 0], h, NT_DIMS, preferred_element_type=F32)
            u_t = lax.dot_general(w3[0, 0], h, NT_DIMS, preferred_element_type=F32)
            he_t = ((g_t * _sigmoid(g_t)) * u_t).astype(BF16)
            ye = lax.dot_general(he_t, w2[0, 0], TN_DIMS, preferred_element_type=F32)
            wcol = hs_ref[:, d + e * LANES:d + (e + 1) * LANES]
            ye = ye * jnp.concatenate([wcol] * (d // LANES), axis=1)
            y = ye if y is None else y + ye
        z_ref[...] = _rms(y, gpost_ref[...])


def _moe(hs, te1, te2, tvalid, w1, w3, w2, gpost, layer, tm):
    dx = hs.shape[1]
    d = dx - 2 * LANES
    nt = te1.shape[0]
    wspec_a = lambda shape: pl.BlockSpec((1, 1) + shape, lambda i, e1, e2, tv: (layer, e1[i], 0, 0))
    wspec_b = lambda shape: pl.BlockSpec((1, 1) + shape, lambda i, e1, e2, tv: (layer, e2[i], 0, 0))
    grid_spec = pltpu.PrefetchScalarGridSpec(
        num_scalar_prefetch=3,
        grid=(nt,),
        in_specs=[
            pl.BlockSpec((tm, dx), lambda i, e1, e2, tv: (i, 0)),
            wspec_a((D_EXPERT_PAD, d)), wspec_a((D_EXPERT_PAD, d)), wspec_a((D_EXPERT_PAD, d)),
            wspec_b((D_EXPERT_PAD, d)), wspec_b((D_EXPERT_PAD, d)), wspec_b((D_EXPERT_PAD, d)),
            pl.BlockSpec((1, d), lambda i, e1, e2, tv: (0, 0)),
        ],
        out_specs=pl.BlockSpec((tm, d), lambda i, e1, e2, tv: (i, 0)),
    )
    return pl.pallas_call(
        _moe_kernel,
        grid_spec=grid_spec,
        out_shape=jax.ShapeDtypeStruct((nt * tm, d), F32),
        compiler_params=pltpu.CompilerParams(
            dimension_semantics=("arbitrary",),
            vmem_limit_bytes=48 * 1024 * 1024),
        name="moe",
    )(te1, te2, tvalid, hs, w1, w3, w2, w1, w3, w2, gpost)


def _combine_kernel(pos_ref, x1_ref, g2_ref, z_hbm, o_ref, zg, sems, *, tc):
    slot = _gather_pipeline(pos_ref, z_hbm, zg, sems, tc)
    o_ref[...] = x1_ref[...] + g2_ref[0] * zg[slot]


def _combine(x1, g2, z, pos, seq_len, tc):
    t, d = x1.shape
    per_seq = seq_len // tc
    grid_spec = pltpu.PrefetchScalarGridSpec(
        num_scalar_prefetch=1,
        grid=(t // tc,),
        in_specs=[
            pl.BlockSpec((tc, d), lambda i, pos: (i, 0)),
            pl.BlockSpec((1, 1, d), lambda i, pos: (i // per_seq, 0, 0)),
            pl.BlockSpec(memory_space=pl.ANY),
        ],
        out_specs=pl.BlockSpec((tc, d), lambda i, pos: (i, 0)),
        scratch_shapes=[
            pltpu.VMEM((GATHER_SLOTS, tc, d), F32),
            pltpu.SemaphoreType.DMA((GATHER_SLOTS,)),
        ],
    )
    return pl.pallas_call(
        functools.partial(_combine_kernel, tc=tc),
        grid_spec=grid_spec,
        out_shape=jax.ShapeDtypeStruct((t, d), F32),
        compiler_params=pltpu.CompilerParams(
            dimension_semantics=("arbitrary",),
            vmem_limit_bytes=32 * 1024 * 1024),
        name="combine",
    )(pos, x1, g2, z)


def _class_tables():
    e1, e2 = [], []
    for g in range(N_GROUPS):
        for a in range(EXPERTS_PER_GROUP):
            for b in range(a + 1, EXPERTS_PER_GROUP):
                e1.append(g * EXPERTS_PER_GROUP + a)
                e2.append(g * EXPERTS_PER_GROUP + b)
    return jnp.asarray(e1, jnp.int32), jnp.asarray(e2, jnp.int32)


def _dispatch_kernel(pos_ref, pad_start_ref, pad_len_ref, used_ref, h_hbm, zero_ref, hs_hbm,
                     buf, in_sems, row_sems, pad_sem, *, rows, tm):
    i = pl.program_id(0)
    n = pl.num_programs(0)
    n_tiles = hs_hbm.shape[0] // tm

    def for_each_pad_row(fn):
        for c in range(N_CLASSES):
            def body(j, carry, c=c):
                fn(pltpu.make_async_copy(zero_ref.at[pl.ds(0, 1), :],
                                         hs_hbm.at[pl.ds(pad_start_ref[c] + j, 1), :], pad_sem))
                return carry
            lax.fori_loop(0, pad_len_ref[c], body, 0)

        def tile_body(k, carry):
            fn(pltpu.make_async_copy(zero_ref, hs_hbm.at[pl.ds(pl.multiple_of(k * tm, tm), tm), :], pad_sem))
            return carry
        lax.fori_loop(used_ref[0], n_tiles, tile_body, 0)

    def block_in(blk, slot):
        start = pl.multiple_of(blk * rows, rows)
        return pltpu.make_async_copy(h_hbm.at[pl.ds(start, rows), :], buf.at[slot], in_sems.at[slot])

    def rows_wait(slot):
        pltpu.make_async_copy(buf.at[slot], hs_hbm.at[pl.ds(0, rows), :], row_sems.at[slot]).wait()

    @pl.when(i == 0)
    def _():
        for_each_pad_row(lambda cp: cp.start())
        for k in range(DISPATCH_AHEAD):
            @pl.when(k < n)
            def _():
                block_in(k, k).start()

    @pl.when(i >= DISPATCH_AHEAD)
    def _():
        rows_wait(lax.rem(i - DISPATCH_AHEAD, DISPATCH_SLOTS))

    @pl.when(i + DISPATCH_AHEAD < n)
    def _():
        block_in(i + DISPATCH_AHEAD, lax.rem(i + DISPATCH_AHEAD, DISPATCH_SLOTS)).start()

    slot = lax.rem(i, DISPATCH_SLOTS)
    block_in(i, slot).wait()
    base = i * rows
    for u in range(rows):
        pltpu.make_async_copy(buf.at[slot, pl.ds(u, 1), :], hs_hbm.at[pl.ds(pos_ref[base + u], 1), :],
                              row_sems.at[slot]).start(priority=u % DMA_QUEUES)

    @pl.when(i == n - 1)
    def _():
        for k in range(DISPATCH_AHEAD):
            @pl.when(i - k >= 0)
            def _():
                rows_wait(lax.rem(i - k, DISPATCH_SLOTS))
        for_each_pad_row(lambda cp: cp.wait())


def _dispatch(h2p, pos, pad_start, pad_len, used_tiles, n_rows, rows, tm):
    t, dx = h2p.shape
    grid_spec = pltpu.PrefetchScalarGridSpec(
        num_scalar_prefetch=4,
        grid=(t // rows,),
        in_specs=[pl.BlockSpec(memory_space=pl.ANY),
                  pl.BlockSpec((tm, dx), lambda i, *_: (0, 0))],
        out_specs=pl.BlockSpec(memory_space=pl.ANY),
        scratch_shapes=[pltpu.VMEM((DISPATCH_SLOTS, rows, dx), F32),
                        pltpu.SemaphoreType.DMA((DISPATCH_SLOTS,)),
                        pltpu.SemaphoreType.DMA((DISPATCH_SLOTS,)),
                        pltpu.SemaphoreType.DMA],
    )
    return pl.pallas_call(
        functools.partial(_dispatch_kernel, rows=rows, tm=tm),
        grid_spec=grid_spec,
        out_shape=jax.ShapeDtypeStruct((n_rows, dx), F32),
        compiler_params=pltpu.CompilerParams(dimension_semantics=("arbitrary",)),
        name="dispatch",
    )(pos, pad_start, pad_len, used_tiles, h2p, jnp.zeros((tm, dx), F32))


def _plan(cls, rank, tm):
    t = cls.size
    nt = t // tm + N_CLASSES
    onehot = (cls[:, :, None] == jnp.arange(N_CLASSES, dtype=jnp.int32)[None, None, :]).astype(jnp.int32)
    per_tile = jnp.sum(onehot, axis=1)
    before = jnp.cumsum(per_tile, axis=0) - per_tile
    counts = jnp.sum(per_tile, axis=0)
    tiles = (counts + tm - 1) // tm
    tile_end = jnp.cumsum(tiles)
    tile_start = tile_end - tiles
    base = tile_start[None, :] * tm + before
    pos = (jnp.sum(onehot * base[:, None, :], axis=2) + rank).reshape(t)
    tile_ids = jnp.arange(nt, dtype=jnp.int32)
    tile_cls = jnp.sum((tile_end[None, :] <= tile_ids[:, None]).astype(jnp.int32), axis=1)
    tvalid = (tile_ids < tile_end[-1]).astype(jnp.int32)
    last_cls = jnp.max(jnp.where(counts > 0, jnp.arange(N_CLASSES, dtype=jnp.int32), 0))
    tile_cls = jnp.where(tvalid == 1, jnp.minimum(tile_cls, N_CLASSES - 1), last_cls)
    ce1, ce2 = _class_tables()
    pad_start = (tile_start * tm + counts).astype(jnp.int32)
    pad_len = (tiles * tm - counts).astype(jnp.int32)
    used_tiles = tile_end[-1:].astype(jnp.int32)
    return pos.astype(jnp.int32), pad_start, pad_len, used_tiles, ce1[tile_cls], ce2[tile_cls], tvalid


def _layer_weights(l, g_pre_mix, g_post_mix, g_pre_ffn, g_post_ffn, w_in, w_dw, b_dw, conv_ln_g,
                   conv_ln_b, w_gate_up, b_gate, gla_norm_g, w_out):
    d = w_in.shape[1]
    wi = w_in[l]
    o_q = 2 * D_CONV
    o_k = o_q + D_GLA_K
    o_v = o_k + D_GLA_K
    o_g = o_v + D_GLA_V
    o_lr = o_g + D_GLA_V
    wnat = jnp.concatenate([wi[:, 0:o_q], wi[:, o_q:o_k], wi[:, o_v:o_g], wi[:, o_g:o_lr]], axis=1)
    wt = jnp.concatenate([wi[:, o_k:o_v], wi[:, o_lr:]], axis=1).T
    return dict(
        gpre=g_pre_mix[l].reshape(1, d), gpost=g_post_mix[l].reshape(1, d),
        gffn=g_pre_ffn[l].reshape(1, d), gpostffn=g_post_ffn[l].reshape(1, d),
        wnat=wnat.astype(BF16), wt=wt.astype(BF16),
        wdw=jnp.repeat(w_dw[l], SUBLANES, axis=0), bdw=b_dw[l].reshape(1, D_CONV),
        lng=conv_ln_g[l].reshape(1, D_CONV), lnb=conv_ln_b[l].reshape(1, D_CONV),
        wup=w_gate_up[l].T.astype(BF16), bup=b_gate[l].reshape(D_GLA_K, 1),
        gnorm=gla_norm_g[l].reshape(1, GLA_DV),
        wout=w_out[l].astype(BF16),
    )


def kernel(x, c, w_ada, b_ada, g_pre_mix, g_post_mix, g_pre_ffn, g_post_ffn, w_in, w_dw, b_dw,
           conv_ln_g, conv_ln_b, w_gate_up, b_gate, gla_norm_g, w_out, w_router, b_router, w1, w3, w2):
    b, s, d = x.shape
    depth = w_ada.shape[0]
    t = b * s
    ts = min(SEQ_TILE, s)
    tm = min(MOE_TILE, t)
    tc = min(COMBINE_TILE, s)
    assert s % ts == 0 and ts % CHUNK == 0 and t % tm == 0 and s % tc == 0

    mod = _modulation(c, w_ada, b_ada).reshape(depth, b, 6, d)

    wr_hi, wr_lo = _split_bf16(w_router.T)
    tok = jnp.arange(min(ts, MXU_TILE), dtype=jnp.int32)
    same = (tok[:, None] // CHUNK) == (tok[None, :] // CHUNK)
    consts = dict(
        wr=jnp.concatenate([wr_hi, wr_lo], axis=0),
        br=b_router.reshape(N_EXPERTS, 1),
        ubd=(same & (tok[:, None] <= tok[None, :])).astype(BF16),
        ebd=same.astype(BF16),
        tri=(jnp.arange(ts)[:, None] <= jnp.arange(ts)[None, :]).astype(BF16),
    )

    w1b = _cast_pad(jnp.swapaxes(w1, 2, 3), D_EXPERT_PAD, d)
    w3b = _cast_pad(jnp.swapaxes(w3, 2, 3), D_EXPERT_PAD, d)
    w2b = _cast_pad(w2, D_EXPERT_PAD, d)

    for l in range(depth):
        lw = _layer_weights(l, g_pre_mix, g_post_mix, g_pre_ffn, g_post_ffn, w_in, w_dw, b_dw,
                            conv_ln_g, conv_ln_b, w_gate_up, b_gate, gla_norm_g, w_out)
        x1, h2p, route = _mix(x, mod[l], lw, consts, ts)
        cls = route[:, 0, :].astype(jnp.int32)
        rank = route[:, 1, :].astype(jnp.int32)
        pos, pad_start, pad_len, used_tiles, te1, te2, tvalid = _plan(cls, rank, tm)
        hs = _dispatch(h2p.reshape(t, d + 2 * LANES), pos, pad_start, pad_len, used_tiles,
                       te1.shape[0] * tm, min(DISPATCH_ROWS, t), tm)
        z = _moe(hs, te1, te2, tvalid, w1b, w3b, w2b, lw["gpostffn"], l, tm)
        g2 = mod[l][:, 5:6, :]
        x = _combine(x1.reshape(t, d), g2, z, pos, s, tc).reshape(b, s, d)
    return x
```

```python
import functools

import jax
import jax.numpy as jnp
from jax import lax
from jax.experimental import pallas as pl
from jax.experimental.pallas import tpu as pltpu

CHUNK = 64
CONV_WIDTH = 31
D_CONV = 512
GLA_HEADS = 4
GLA_DV = 128
GLA_DK = 64
D_GLA_K = GLA_HEADS * GLA_DK
D_GLA_V = GLA_HEADS * GLA_DV
GATE_RANK = 16
GATE_TAU = 16.0
N_EXPERTS = 16
N_GROUPS = 4
EXPERTS_PER_GROUP = 4
PAIRS_PER_GROUP = 6
N_CLASSES = N_GROUPS * PAIRS_PER_GROUP
CLASS_ROWS = 32
D_EXPERT = 704
D_EXPERT_PAD = 768
EPS = 1e-6
LOG2_E = 1.4426950408889634

SUBLANES = 8
LANES = 128
MXU_TILE = 256
CONV_PAD = 32
CONV_ROWS = 32
CONV_LANES = 256
SEQ_TILE = 512
MOE_TILE = 256
COMBINE_TILE = 1024
DISPATCH_ROWS = 1024
DISPATCH_AHEAD = 2
DISPATCH_SLOTS = 2 * DISPATCH_AHEAD
VMEM_HEADROOM = 8 * 1024 * 1024
GATHER_SLOTS = 3

F32 = jnp.float32
BF16 = jnp.bfloat16
NT_DIMS = (((1,), (1,)), ((), ()))
TN_DIMS = (((0,), (0,)), ((), ()))


def _sigmoid(x):
    return 1.0 / (1.0 + jnp.exp2(x * -LOG2_E))


def _rms(x, g):
    ms = jnp.mean(x * x, axis=-1, keepdims=True)
    return x * lax.rsqrt(ms + EPS) * g


def _split_bf16(x):
    hi = x.astype(BF16)
    lo = (x - hi.astype(F32)).astype(BF16)
    return hi, lo


def _mod_kernel(c_ref, w_ref, b_ref, o_ref):
    c = c_ref[...]
    ca = c * _sigmoid(c)
    a_hi, a_lo = _split_bf16(ca)
    w_hi, w_lo = _split_bf16(w_ref[0])
    acc = jnp.dot(a_hi, w_hi, preferred_element_type=F32)
    acc += jnp.dot(a_lo, w_hi, preferred_element_type=F32)
    acc += jnp.dot(a_hi, w_lo, preferred_element_type=F32)
    o_ref[0] = acc + b_ref[0]


def _modulation(c, w_ada, b_ada):
    depth, d, d6 = w_ada.shape
    b = c.shape[0]
    nblk = d6 // d
    return pl.pallas_call(
        _mod_kernel,
        grid=(depth, nblk),
        in_specs=[
            pl.BlockSpec((b, d), lambda l, n: (0, 0)),
            pl.BlockSpec((1, d, d), lambda l, n: (l, 0, n)),
            pl.BlockSpec((1, 1, d), lambda l, n: (l, 0, n)),
        ],
        out_specs=pl.BlockSpec((1, b, d), lambda l, n: (l, 0, n)),
        out_shape=jax.ShapeDtypeStruct((depth, b, d6), F32),
        compiler_params=pltpu.CompilerParams(
            dimension_semantics=("arbitrary", "arbitrary"),
            vmem_limit_bytes=32 * 1024 * 1024),
        name="adaln_mod",
    )(c, w_ada, b_ada.reshape(depth, 1, d6))


def _cast_pad_kernel(w_ref, o_ref):
    rows, cols = w_ref.shape[2:]
    prow, pcol = o_ref.shape[2:]
    o_ref[0, 0, 0:rows, 0:cols] = w_ref[0, 0].astype(BF16)
    if pcol > cols:
        o_ref[0, 0, :, cols:] = jnp.zeros((prow, pcol - cols), BF16)
    if prow > rows:
        o_ref[0, 0, rows:, 0:cols] = jnp.zeros((prow - rows, cols), BF16)


def _cast_pad(w, prow, pcol):
    depth, ne, rows, cols = w.shape
    return pl.pallas_call(
        _cast_pad_kernel,
        grid=(depth, ne),
        in_specs=[pl.BlockSpec((1, 1, rows, cols), lambda l, e: (l, e, 0, 0))],
        out_specs=pl.BlockSpec((1, 1, prow, pcol), lambda l, e: (l, e, 0, 0)),
        out_shape=jax.ShapeDtypeStruct((depth, ne, prow, pcol), BF16),
        compiler_params=pltpu.CompilerParams(
            dimension_semantics=("arbitrary", "arbitrary"),
            vmem_limit_bytes=32 * 1024 * 1024),
        name="cast_pad",
    )(w)


def _route(logits_t, br):
    m = jnp.max(logits_t, axis=0, keepdims=True)
    e = jnp.exp(logits_t - m)
    probs = e / jnp.sum(e, axis=0, keepdims=True)
    sel = probs + br
    one = jnp.ones_like(m)
    zero = jnp.zeros_like(m)
    picked, gscore = [], []
    for g in range(N_GROUPS):
        rows = [sel[g * EXPERTS_PER_GROUP + k:g * EXPERTS_PER_GROUP + k + 1, :]
                for k in range(EXPERTS_PER_GROUP)]
        score = zero
        for k in range(EXPERTS_PER_GROUP):
            rank = zero
            for k2 in range(EXPERTS_PER_GROUP):
                if k2 == k:
                    continue
                ahead = (rows[k2] >= rows[k]) if k2 < k else (rows[k2] > rows[k])
                rank = rank + jnp.where(ahead, one, zero)
            pk = jnp.where(rank < 2.0, one, zero)
            picked.append(pk)
            score = score + pk * rows[k]
        gscore.append(score)
    best = gscore[0]
    for g in range(1, N_GROUPS):
        best = jnp.maximum(best, gscore[g])
    taken = zero
    flag = [zero] * EXPERTS_PER_GROUP
    prob = [zero] * EXPERTS_PER_GROUP
    gbase = zero
    for g in range(N_GROUPS):
        isb = jnp.where(gscore[g] == best, one, zero) * (one - taken)
        taken = taken + isb
        gbase = gbase + isb * float(g * PAIRS_PER_GROUP)
        for k in range(EXPERTS_PER_GROUP):
            ei = g * EXPERTS_PER_GROUP + k
            flag[k] = flag[k] + isb * picked[ei]
            prob[k] = prob[k] + isb * probs[ei:ei + 1, :]
    f0, f1, f2, f3 = flag
    pair = f0 * (f2 * 1.0 + f3 * 2.0) + (one - f0) * (f1 * (f2 * 3.0 + f3 * 4.0) + (one - f1) * 5.0)
    cls = gbase + pair
    wa_raw = f0 * prob[0] + (one - f0) * (f1 * prob[1] + (one - f1) * prob[2])
    tot = f0 * prob[0] + f1 * prob[1] + f2 * prob[2] + f3 * prob[3]
    wa = wa_raw / tot
    wb = (tot - wa_raw) / tot
    return cls, wa, wb


def _mix_kernel(x_ref, mod_ref, gpre_ref, gpost_ref, gffn_ref, wnat_ref, wt_ref, wdw_ref, bdw_ref,
                lng_ref, lnb_ref, wup_ref, bup_ref, gnorm_ref, wout_ref, wr_ref, br_ref,
                ubd_ref, ebd_ref, tri_ref,
                x1_ref, h2p_ref, route_ref,
                ubuf, ush, cbuf, st_ref, sbd_ref, ycat_ref, *, ts):
    j = pl.program_id(1)
    nch = ts // CHUNK

    @pl.when(j == 0)
    def _():
        ubuf[0:CONV_PAD, :] = jnp.zeros((CONV_PAD, D_CONV), F32)
        st_ref[...] = jnp.zeros_like(st_ref)
        sbd_ref[...] = jnp.zeros_like(sbd_ref)

    @pl.when(j > 0)
    def _():
        ubuf[0:CONV_PAD, :] = ubuf[ts:ts + CONV_PAD, :]

    x = x_ref[0]
    sh1 = mod_ref[0, 0:1, :]
    sc1 = mod_ref[0, 1:2, :]
    g1 = mod_ref[0, 2:3, :]
    sh2 = mod_ref[0, 3:4, :]
    sc2 = mod_ref[0, 4:5, :]

    h = _rms(x, gpre_ref[...] * (1.0 + sc1)) + sh1
    hb = h.astype(BF16)

    cvg = jnp.dot(hb, wnat_ref[:, 0:2 * D_CONV], preferred_element_type=F32)
    ubuf[CONV_PAD:CONV_PAD + ts, :] = cvg[:, 0:D_CONV] * _sigmoid(cvg[:, D_CONV:2 * D_CONV])

    for r in range(1, SUBLANES):
        ush[r - 1] = ubuf[r:r + ts + CONV_PAD - SUBLANES, :]

    groups = CONV_ROWS // SUBLANES
    first_off = CONV_PAD - (CONV_WIDTH - 1)

    def conv_block(rb, carry, c0):
        r0 = rb * CONV_ROWS
        cs = slice(c0, c0 + CONV_LANES)
        acc = [jnp.broadcast_to(bdw_ref[:, cs], (SUBLANES, CONV_LANES)) for _ in range(groups)]
        for r in range(SUBLANES):
            taps = [tap for tap in range(CONV_WIDTH) if (first_off + tap) % SUBLANES == r]
            shifts = [(first_off + tap) // SUBLANES for tap in taps]
            src = ubuf if r == 0 else ush.at[r - 1]
            blk = {m: src[pl.ds(r0 + m * SUBLANES, SUBLANES), cs]
                   for m in range(min(shifts), max(shifts) + groups)}
            for tap, a in zip(taps, shifts):
                w8 = wdw_ref[tap * SUBLANES:(tap + 1) * SUBLANES, cs]
                for g in range(groups):
                    acc[g] = acc[g] + w8 * blk[a + g]
        cbuf[pl.ds(r0, CONV_ROWS), cs] = jnp.concatenate(acc, axis=0)
        return carry

    for c0 in range(0, D_CONV, CONV_LANES):
        for rb in range(ts // CONV_ROWS):
            conv_block(rb, 0, c0)
    cv = cbuf[...]
    mu = jnp.mean(cv, axis=-1, keepdims=True)
    cen = cv - mu
    var = jnp.mean(cen * cen, axis=-1, keepdims=True)
    yn = cen * lax.rsqrt(var + EPS) * lng_ref[...] + lnb_ref[...]
    ycat_ref[:, 0:D_CONV] = (yn * _sigmoid(yn)).astype(BF16)

    qvo = jnp.dot(hb, wnat_ref[:, 2 * D_CONV:], preferred_element_type=F32)
    q = (qvo[:, 0:D_GLA_K] * (GLA_DK ** -0.5)).astype(BF16)
    v = qvo[:, D_GLA_K:D_GLA_K + D_GLA_V].astype(BF16)
    og = qvo[:, D_GLA_K + D_GLA_V:]
    tt = lax.dot_general(wt_ref[...], hb, NT_DIMS, preferred_element_type=F32)
    k_t = tt[0:D_GLA_K, :]
    lr_t = tt[D_GLA_K:, :].astype(BF16)
    z_t = jnp.dot(wup_ref[...], lr_t, preferred_element_type=F32) + bup_ref[...]
    la_t = (jnp.minimum(z_t, 0.0) - jnp.log(1.0 + jnp.exp(-jnp.abs(z_t)))) * (1.0 / GATE_TAU)
    la_b = la_t.astype(BF16)
    slab = ubd_ref.shape[0]
    bcum = jnp.concatenate(
        [jnp.dot(la_b[:, s0:s0 + slab], ubd_ref[...], preferred_element_type=F32)
         for s0 in range(0, ts, slab)], axis=1)
    bend = jnp.concatenate(
        [jnp.dot(la_b[:, s0:s0 + slab], ebd_ref[...], preferred_element_type=F32)
         for s0 in range(0, ts, slab)], axis=1)
    kd_t = (k_t * jnp.exp(bend - bcum)).astype(BF16)
    dec_t = jnp.exp(bend)

    st = [st_ref[hh] for hh in range(GLA_HEADS)]
    gn = gnorm_ref[...]
    for c in range(nch):
        lo, hi_ = c * CHUNK, (c + 1) * CHUNK
        kd_c = kd_t[:, lo:hi_]
        v_c = v[lo:hi_, :]
        for hh in range(GLA_HEADS):
            ks, ke = hh * GLA_DK, (hh + 1) * GLA_DK
            vs, ve = hh * GLA_DV, (hh + 1) * GLA_DV
            upd = jnp.dot(kd_c[ks:ke, :], v_c[:, vs:ve], preferred_element_type=F32)
            dec = jnp.broadcast_to(dec_t[ks:ke, lo:lo + 1], (GLA_DK, GLA_DV))
            st[hh] = dec * st[hh] + upd
            sbd_ref[ks:ke, vs:ve] = st[hh].astype(BF16)
        o_c = jnp.dot(q[lo:hi_, :], sbd_ref[...], preferred_element_type=F32)
        og_c = og[lo:hi_, :]
        for hh in range(GLA_HEADS):
            vs, ve = hh * GLA_DV, (hh + 1) * GLA_DV
            on = _rms(o_c[:, vs:ve], gn)
            gate = og_c[:, vs:ve]
            ycat_ref[lo:hi_, D_CONV + vs:D_CONV + ve] = (on * (gate * _sigmoid(gate))).astype(BF16)
    for hh in range(GLA_HEADS):
        st_ref[hh] = st[hh]

    y = jnp.dot(ycat_ref[...], wout_ref[...], preferred_element_type=F32)
    x1 = x + g1 * _rms(y, gpost_ref[...])
    x1_ref[0] = x1
    h2 = _rms(x1, gffn_ref[...] * (1.0 + sc2)) + sh2
    d = h2.shape[1]
    h2p_ref[0, :, 0:d] = h2
    h2_hi, h2_lo = _split_bf16(h2)
    lg2 = lax.dot_general(wr_ref[...], h2_hi, NT_DIMS, preferred_element_type=F32)
    lg1 = lax.dot_general(wr_ref[0:N_EXPERTS, :], h2_lo, NT_DIMS, preferred_element_type=F32)
    logits_t = lg2[0:N_EXPERTS, :] + lg2[N_EXPERTS:, :] + lg1
    cls, wa, wb = _route(logits_t, br_ref[...])
    class_ids = lax.broadcasted_iota(jnp.int32, (CLASS_ROWS, ts), 0).astype(F32)
    onehot_t = jnp.where(class_ids == cls, 1.0, 0.0)
    seen = jnp.dot(onehot_t.astype(BF16), tri_ref[...], preferred_element_type=F32)
    rank = jnp.sum(onehot_t * seen, axis=0, keepdims=True) - 1.0
    route_ref[0] = jnp.concatenate([cls, rank, jnp.zeros((SUBLANES - 2, ts), F32)], axis=0)
    h2p_ref[0, :, d:d + LANES] = jnp.transpose(jnp.broadcast_to(wa, (LANES, ts)))
    h2p_ref[0, :, d + LANES:] = jnp.transpose(jnp.broadcast_to(wb, (LANES, ts)))


def _mix(x, mod, lw, consts, ts):
    b, s, d = x.shape
    nst = s // ts
    full = lambda shape: pl.BlockSpec(shape, lambda bi, ji: (0,) * len(shape))
    in_specs = [
        pl.BlockSpec((1, ts, d), lambda bi, ji: (bi, ji, 0)),
        pl.BlockSpec((1, 6, d), lambda bi, ji: (bi, 0, 0)),
        full((1, d)), full((1, d)), full((1, d)),
        full(lw["wnat"].shape), full(lw["wt"].shape),
        full((CONV_WIDTH * SUBLANES, D_CONV)), full((1, D_CONV)), full((1, D_CONV)), full((1, D_CONV)),
        full((D_GLA_K, GATE_RANK)), full((D_GLA_K, 1)), full((1, GLA_DV)),
        full((d, d)), full((2 * N_EXPERTS, d)), full((N_EXPERTS, 1)),
        full(consts["ubd"].shape), full(consts["ebd"].shape), full((ts, ts)),
    ]
    out_specs = [
        pl.BlockSpec((1, ts, d), lambda bi, ji: (bi, ji, 0)),
        pl.BlockSpec((1, ts, d + 2 * LANES), lambda bi, ji: (bi, ji, 0)),
        pl.BlockSpec((1, 8, ts), lambda bi, ji: (bi * nst + ji, 0, 0)),
    ]
    out_shape = [
        jax.ShapeDtypeStruct((b, s, d), F32),
        jax.ShapeDtypeStruct((b, s, d + 2 * LANES), F32),
        jax.ShapeDtypeStruct((b * nst, 8, ts), F32),
    ]
    scratch = [
        pltpu.VMEM((CONV_PAD + ts, D_CONV), F32),
        pltpu.VMEM((SUBLANES - 1, CONV_PAD + ts - SUBLANES, D_CONV), F32),
        pltpu.VMEM((ts, D_CONV), F32),
        pltpu.VMEM((GLA_HEADS, GLA_DK, GLA_DV), F32),
        pltpu.VMEM((D_GLA_K, D_GLA_V), BF16),
        pltpu.VMEM((ts, d), BF16),
    ]
    return pl.pallas_call(
        functools.partial(_mix_kernel, ts=ts),
        grid=(b, nst),
        in_specs=in_specs,
        out_specs=out_specs,
        out_shape=out_shape,
        scratch_shapes=scratch,
        compiler_params=pltpu.CompilerParams(
            dimension_semantics=("arbitrary", "arbitrary"),
            vmem_limit_bytes=56 * 1024 * 1024),
        name="mix",
    )(x, mod, lw["gpre"], lw["gpost"], lw["gffn"], lw["wnat"], lw["wt"], lw["wdw"], lw["bdw"],
      lw["lng"], lw["lnb"], lw["wup"], lw["bup"], lw["gnorm"], lw["wout"], consts["wr"],
      consts["br"], consts["ubd"], consts["ebd"], consts["tri"])


def _row_copy(idx_ref, base, r, src_hbm, dst_vmem, sem):
    tok = idx_ref[base + r]
    return pltpu.make_async_copy(src_hbm.at[pl.ds(tok, 1), :], dst_vmem.at[pl.ds(r, 1), :], sem)


def _row_gather_start(idx_ref, base, src_hbm, dst_vmem, sem, n_rows, unrolled):
    if unrolled:
        for r in range(n_rows):
            _row_copy(idx_ref, base, r, src_hbm, dst_vmem, sem).start()
    else:
        def body(r, carry):
            _row_copy(idx_ref, base, r, src_hbm, dst_vmem, sem).start()
            return carry
        lax.fori_loop(0, n_rows, body, 0)


def _row_gather_wait(src_hbm, dst_vmem, sem, n_rows):
    pltpu.make_async_copy(src_hbm.at[pl.ds(0, n_rows), :], dst_vmem, sem).wait()


def _gather_pipeline(idx_ref, src_hbm, buf, sems, n_rows):
    i = pl.program_id(0)
    n = pl.num_programs(0)

    @pl.when(i == 0)
    def _():
        for k in range(GATHER_SLOTS - 1):
            @pl.when(k < n)
            def _():
                _row_gather_start(idx_ref, k * n_rows, src_hbm, buf.at[k], sems.at[k], n_rows, unrolled=False)

    ahead = i + GATHER_SLOTS - 1

    @pl.when(ahead < n)
    def _():
        aslot = lax.rem(ahead, GATHER_SLOTS)
        _row_gather_start(idx_ref, ahead * n_rows, src_hbm, buf.at[aslot], sems.at[aslot], n_rows,
                          unrolled=True)

    slot = lax.rem(i, GATHER_SLOTS)
    _row_gather_wait(src_hbm, buf.at[slot], sems.at[slot], n_rows)
    return slot


def _moe_kernel(te1_ref, te2_ref, tvalid_ref,
                hs_ref, w1a_ref, w3a_ref, w2a_ref, w1b_ref, w3b_ref, w2b_ref, gpost_ref,
                z_ref):
    i = pl.program_id(0)
    d = z_ref.shape[1]

    @pl.when(tvalid_ref[i] == 0)
    def _():
        z_ref[...] = jnp.zeros_like(z_ref)

    @pl.when(tvalid_ref[i] != 0)
    def _():
        h = hs_ref[:, 0:d].astype(BF16)
        y = None
        for e, (w1, w3, w2) in enumerate(((w1a_ref, w3a_ref, w2a_ref), (w1b_ref, w3b_ref, w2b_ref))):
            g_t = lax.dot_general(w1[0, 0], h, NT_DIMS, preferred_element_type=F32)
            u_t = lax.dot_general(w3[0, 0], h, NT_DIMS, preferred_element_type=F32)
            he_t = ((g_t * _sigmoid(g_t)) * u_t).astype(BF16)
            ye = lax.dot_general(he_t, w2[0, 0], TN_DIMS, preferred_element_type=F32)
            wcol = hs_ref[:, d + e * LANES:d + (e + 1) * LANES]
            ye = ye * jnp.concatenate([wcol] * (d // LANES), axis=1)
            y = ye if y is None else y + ye
        z_ref[...] = _rms(y, gpost_ref[...])


def _moe(hs, te1, te2, tvalid, w1, w3, w2, gpost, layer, tm):
    dx = hs.shape[1]
    d = dx - 2 * LANES
    nt = te1.shape[0]
    wspec_a = lambda shape: pl.BlockSpec((1, 1) + shape, lambda i, e1, e2, tv: (layer, e1[i], 0, 0))
    wspec_b = lambda shape: pl.BlockSpec((1, 1) + shape, lambda i, e1, e2, tv: (layer, e2[i], 0, 0))
    grid_spec = pltpu.PrefetchScalarGridSpec(
        num_scalar_prefetch=3,
        grid=(nt,),
        in_specs=[
            pl.BlockSpec((tm, dx), lambda i, e1, e2, tv: (i, 0)),
            wspec_a((D_EXPERT_PAD, d)), wspec_a((D_EXPERT_PAD, d)), wspec_a((D_EXPERT_PAD, d)),
            wspec_b((D_EXPERT_PAD, d)), wspec_b((D_EXPERT_PAD, d)), wspec_b((D_EXPERT_PAD, d)),
            pl.BlockSpec((1, d), lambda i, e1, e2, tv: (0, 0)),
        ],
        out_specs=pl.BlockSpec((tm, d), lambda i, e1, e2, tv: (i, 0)),
    )
    return pl.pallas_call(
        _moe_kernel,
        grid_spec=grid_spec,
        out_shape=jax.ShapeDtypeStruct((nt * tm, d), F32),
        compiler_params=pltpu.CompilerParams(
            dimension_semantics=("arbitrary",),
            vmem_limit_bytes=48 * 1024 * 1024),
        name="moe",
    )(te1, te2, tvalid, hs, w1, w3, w2, w1, w3, w2, gpost)


def _combine_kernel(pos_ref, x1_ref, g2_ref, z_hbm, o_ref, zg, sems, *, tc):
    slot = _gather_pipeline(pos_ref, z_hbm, zg, sems, tc)
    o_ref[...] = x1_ref[...] + g2_ref[0] * zg[slot]


def _combine(x1, g2, z, pos, seq_len, tc):
    t, d = x1.shape
    per_seq = seq_len // tc
    grid_spec = pltpu.PrefetchScalarGridSpec(
        num_scalar_prefetch=1,
        grid=(t // tc,),
        in_specs=[
            pl.BlockSpec((tc, d), lambda i, pos: (i, 0)),
            pl.BlockSpec((1, 1, d), lambda i, pos: (i // per_seq, 0, 0)),
            pl.BlockSpec(memory_space=pl.ANY),
        ],
        out_specs=pl.BlockSpec((tc, d), lambda i, pos: (i, 0)),
        scratch_shapes=[
            pltpu.VMEM((GATHER_SLOTS, tc, d), F32),
            pltpu.SemaphoreType.DMA((GATHER_SLOTS,)),
        ],
    )
    return pl.pallas_call(
        functools.partial(_combine_kernel, tc=tc),
        grid_spec=grid_spec,
        out_shape=jax.ShapeDtypeStruct((t, d), F32),
        compiler_params=pltpu.CompilerParams(
            dimension_semantics=("arbitrary",),
            vmem_limit_bytes=(GATHER_SLOTS + 4) * tc * d * 4 + VMEM_HEADROOM),
        name="combine",
    )(pos, x1, g2, z)


def _class_tables():
    e1, e2 = [], []
    for g in range(N_GROUPS):
        for a in range(EXPERTS_PER_GROUP):
            for b in range(a + 1, EXPERTS_PER_GROUP):
                e1.append(g * EXPERTS_PER_GROUP + a)
                e2.append(g * EXPERTS_PER_GROUP + b)
    return jnp.asarray(e1, jnp.int32), jnp.asarray(e2, jnp.int32)


def _dispatch_kernel(pos_ref, pad_start_ref, pad_len_ref, used_ref, h_hbm, zero_ref, hs_hbm,
                     buf, in_sems, row_sems, pad_sem, *, rows, tm):
    i = pl.program_id(0)
    n = pl.num_programs(0)
    n_tiles = hs_hbm.shape[0] // tm

    def for_each_pad_row(fn):
        for c in range(N_CLASSES):
            def body(j, carry, c=c):
                fn(pltpu.make_async_copy(zero_ref.at[pl.ds(0, 1), :],
                                         hs_hbm.at[pl.ds(pad_start_ref[c] + j, 1), :], pad_sem))
                return carry
            lax.fori_loop(0, pad_len_ref[c], body, 0)

        def tile_body(k, carry):
            fn(pltpu.make_async_copy(zero_ref, hs_hbm.at[pl.ds(pl.multiple_of(k * tm, tm), tm), :], pad_sem))
            return carry
        lax.fori_loop(used_ref[0], n_tiles, tile_body, 0)

    def block_in(blk, slot):
        start = pl.multiple_of(blk * rows, rows)
        return pltpu.make_async_copy(h_hbm.at[pl.ds(start, rows), :], buf.at[slot], in_sems.at[slot])

    def rows_wait(slot):
        pltpu.make_async_copy(buf.at[slot], hs_hbm.at[pl.ds(0, rows), :], row_sems.at[slot]).wait()

    @pl.when(i == 0)
    def _():
        for_each_pad_row(lambda cp: cp.start())
        for k in range(DISPATCH_AHEAD):
            @pl.when(k < n)
            def _():
                block_in(k, k).start()

    @pl.when(i >= DISPATCH_AHEAD)
    def _():
        rows_wait(lax.rem(i - DISPATCH_AHEAD, DISPATCH_SLOTS))

    @pl.when(i + DISPATCH_AHEAD < n)
    def _():
        block_in(i + DISPATCH_AHEAD, lax.rem(i + DISPATCH_AHEAD, DISPATCH_SLOTS)).start()

    slot = lax.rem(i, DISPATCH_SLOTS)
    block_in(i, slot).wait()
    base = i * rows
    for u in range(rows):
        pltpu.make_async_copy(buf.at[slot, pl.ds(u, 1), :], hs_hbm.at[pl.ds(pos_ref[base + u], 1), :],
                              row_sems.at[slot]).start()

    @pl.when(i == n - 1)
    def _():
        for k in range(DISPATCH_AHEAD):
            @pl.when(i - k >= 0)
            def _():
                rows_wait(lax.rem(i - k, DISPATCH_SLOTS))
        for_each_pad_row(lambda cp: cp.wait())


def _dispatch(h2p, pos, pad_start, pad_len, used_tiles, n_rows, rows, tm):
    t, dx = h2p.shape
    grid_spec = pltpu.PrefetchScalarGridSpec(
        num_scalar_prefetch=4,
        grid=(t // rows,),
        in_specs=[pl.BlockSpec(memory_space=pl.ANY),
                  pl.BlockSpec((tm, dx), lambda i, *_: (0, 0))],
        out_specs=pl.BlockSpec(memory_space=pl.ANY),
        scratch_shapes=[pltpu.VMEM((DISPATCH_SLOTS, rows, dx), F32),
                        pltpu.SemaphoreType.DMA((DISPATCH_SLOTS,)),
                        pltpu.SemaphoreType.DMA((DISPATCH_SLOTS,)),
                        pltpu.SemaphoreType.DMA],
    )
    return pl.pallas_call(
        functools.partial(_dispatch_kernel, rows=rows, tm=tm),
        grid_spec=grid_spec,
        out_shape=jax.ShapeDtypeStruct((n_rows, dx), F32),
        compiler_params=pltpu.CompilerParams(
            dimension_semantics=("arbitrary",),
            vmem_limit_bytes=(DISPATCH_SLOTS * rows + 2 * tm) * dx * 4 + VMEM_HEADROOM),
        name="dispatch",
    )(pos, pad_start, pad_len, used_tiles, h2p, jnp.zeros((tm, dx), F32))


def _plan(cls, rank, tm):
    t = cls.size
    nt = t // tm + N_CLASSES
    onehot = (cls[:, :, None] == jnp.arange(N_CLASSES, dtype=jnp.int32)[None, None, :]).astype(jnp.int32)
    per_tile = jnp.sum(onehot, axis=1)
    before = jnp.cumsum(per_tile, axis=0) - per_tile
    counts = jnp.sum(per_tile, axis=0)
    tiles = (counts + tm - 1) // tm
    tile_end = jnp.cumsum(tiles)
    tile_start = tile_end - tiles
    base = tile_start[None, :] * tm + before
    pos = (jnp.sum(onehot * base[:, None, :], axis=2) + rank).reshape(t)
    tile_ids = jnp.arange(nt, dtype=jnp.int32)
    tile_cls = jnp.sum((tile_end[None, :] <= tile_ids[:, None]).astype(jnp.int32), axis=1)
    tvalid = (tile_ids < tile_end[-1]).astype(jnp.int32)
    last_cls = jnp.max(jnp.where(counts > 0, jnp.arange(N_CLASSES, dtype=jnp.int32), 0))
    tile_cls = jnp.where(tvalid == 1, jnp.minimum(tile_cls, N_CLASSES - 1), last_cls)
    ce1, ce2 = _class_tables()
    pad_start = (tile_start * tm + counts).astype(jnp.int32)
    pad_len = (tiles * tm - counts).astype(jnp.int32)
    used_tiles = tile_end[-1:].astype(jnp.int32)
    return pos.astype(jnp.int32), pad_start, pad_len, used_tiles, ce1[tile_cls], ce2[tile_cls], tvalid


def _layer_weights(l, g_pre_mix, g_post_mix, g_pre_ffn, g_post_ffn, w_in, w_dw, b_dw, conv_ln_g,
                   conv_ln_b, w_gate_up, b_gate, gla_norm_g, w_out):
    d = w_in.shape[1]
    wi = w_in[l]
    o_q = 2 * D_CONV
    o_k = o_q + D_GLA_K
    o_v = o_k + D_GLA_K
    o_g = o_v + D_GLA_V
    o_lr = o_g + D_GLA_V
    wnat = jnp.concatenate([wi[:, 0:o_q], wi[:, o_q:o_k], wi[:, o_v:o_g], wi[:, o_g:o_lr]], axis=1)
    wt = jnp.concatenate([wi[:, o_k:o_v], wi[:, o_lr:]], axis=1).T
    return dict(
        gpre=g_pre_mix[l].reshape(1, d), gpost=g_post_mix[l].reshape(1, d),
        gffn=g_pre_ffn[l].reshape(1, d), gpostffn=g_post_ffn[l].reshape(1, d),
        wnat=wnat.astype(BF16), wt=wt.astype(BF16),
        wdw=jnp.repeat(w_dw[l], SUBLANES, axis=0), bdw=b_dw[l].reshape(1, D_CONV),
        lng=conv_ln_g[l].reshape(1, D_CONV), lnb=conv_ln_b[l].reshape(1, D_CONV),
        wup=w_gate_up[l].T.astype(BF16), bup=b_gate[l].reshape(D_GLA_K, 1),
        gnorm=gla_norm_g[l].reshape(1, GLA_DV),
        wout=w_out[l].astype(BF16),
    )


def kernel(x, c, w_ada, b_ada, g_pre_mix, g_post_mix, g_pre_ffn, g_post_ffn, w_in, w_dw, b_dw,
           conv_ln_g, conv_ln_b, w_gate_up, b_gate, gla_norm_g, w_out, w_router, b_router, w1, w3, w2):
    b, s, d = x.shape
    depth = w_ada.shape[0]
    t = b * s
    ts = min(SEQ_TILE, s)
    tm = min(MOE_TILE, t)
    tc = min(COMBINE_TILE, s)
    assert s % ts == 0 and ts % CHUNK == 0 and t % tm == 0 and s % tc == 0

    mod = _modulation(c, w_ada, b_ada).reshape(depth, b, 6, d)

    wr_hi, wr_lo = _split_bf16(w_router.T)
    tok = jnp.arange(min(ts, MXU_TILE), dtype=jnp.int32)
    same = (tok[:, None] // CHUNK) == (tok[None, :] // CHUNK)
    consts = dict(
        wr=jnp.concatenate([wr_hi, wr_lo], axis=0),
        br=b_router.reshape(N_EXPERTS, 1),
        ubd=(same & (tok[:, None] <= tok[None, :])).astype(BF16),
        ebd=same.astype(BF16),
        tri=(jnp.arange(ts)[:, None] <= jnp.arange(ts)[None, :]).astype(BF16),
    )

    w1b = _cast_pad(jnp.swapaxes(w1, 2, 3), D_EXPERT_PAD, d)
    w3b = _cast_pad(jnp.swapaxes(w3, 2, 3), D_EXPERT_PAD, d)
    w2b = _cast_pad(w2, D_EXPERT_PAD, d)

    for l in range(depth):
        lw = _layer_weights(l, g_pre_mix, g_post_mix, g_pre_ffn, g_post_ffn, w_in, w_dw, b_dw,
                            conv_ln_g, conv_ln_b, w_gate_up, b_gate, gla_norm_g, w_out)
        x1, h2p, route = _mix(x, mod[l], lw, consts, ts)
        cls = route[:, 0, :].astype(jnp.int32)
        rank = route[:, 1, :].astype(jnp.int32)
        pos, pad_start, pad_len, used_tiles, te1, te2, tvalid = _plan(cls, rank, tm)
        hs = _dispatch(h2p.reshape(t, d + 2 * LANES), pos, pad_start, pad_len, used_tiles,
                       te1.shape[0] * tm, min(DISPATCH_ROWS, t), tm)
        z = _moe(hs, te1, te2, tvalid, w1b, w3b, w2b, lw["gpostffn"], l, tm)
        g2 = mod[l][:, 5:6, :]
        x = _combine(x1.reshape(t, d), g2, z, pos, s, tc).reshape(b, s, d)
    return x
```

```python
import functools

import jax
import jax.numpy as jnp
from jax import lax
from jax.experimental import pallas as pl
from jax.experimental.pallas import tpu as pltpu

CHUNK = 64
CONV_WIDTH = 31
D_CONV = 512
GLA_HEADS = 4
GLA_DV = 128
GLA_DK = 64
D_GLA_K = GLA_HEADS * GLA_DK
D_GLA_V = GLA_HEADS * GLA_DV
GATE_RANK = 16
GATE_TAU = 16.0
N_EXPERTS = 16
N_GROUPS = 4
EXPERTS_PER_GROUP = 4
PAIRS_PER_GROUP = 6
N_CLASSES = N_GROUPS * PAIRS_PER_GROUP
CLASS_ROWS = 32
D_EXPERT = 704
D_EXPERT_PAD = 768
EPS = 1e-6
LOG2_E = 1.4426950408889634

SUBLANES = 8
LANES = 128
MXU_TILE = 256
CONV_PAD = 32
CONV_ROWS = 32
CONV_LANES = 256
SEQ_TILE = 512
MOE_TILE = 256
COMBINE_TILE = 512
DISPATCH_ROWS = 512
DISPATCH_AHEAD = 2
DISPATCH_SLOTS = 2 * DISPATCH_AHEAD
VMEM_HEADROOM = 8 * 1024 * 1024
GATHER_SLOTS = 3

F32 = jnp.float32
BF16 = jnp.bfloat16
NT_DIMS = (((1,), (1,)), ((), ()))
TN_DIMS = (((0,), (0,)), ((), ()))


def _sigmoid(x):
    return 1.0 / (1.0 + jnp.exp2(x * -LOG2_E))


def _rms(x, g):
    ms = jnp.mean(x * x, axis=-1, keepdims=True)
    return x * lax.rsqrt(ms + EPS) * g


def _split_bf16(x):
    hi = x.astype(BF16)
    lo = (x - hi.astype(F32)).astype(BF16)
    return hi, lo


def _mod_kernel(c_ref, w_ref, b_ref, o_ref):
    c = c_ref[...]
    ca = c * _sigmoid(c)
    a_hi, a_lo = _split_bf16(ca)
    w_hi, w_lo = _split_bf16(w_ref[0])
    acc = jnp.dot(a_hi, w_hi, preferred_element_type=F32)
    acc += jnp.dot(a_lo, w_hi, preferred_element_type=F32)
    acc += jnp.dot(a_hi, w_lo, preferred_element_type=F32)
    o_ref[0] = acc + b_ref[0]


def _modulation(c, w_ada, b_ada):
    depth, d, d6 = w_ada.shape
    b = c.shape[0]
    nblk = d6 // d
    return pl.pallas_call(
        _mod_kernel,
        grid=(depth, nblk),
        in_specs=[
            pl.BlockSpec((b, d), lambda l, n: (0, 0)),
            pl.BlockSpec((1, d, d), lambda l, n: (l, 0, n)),
            pl.BlockSpec((1, 1, d), lambda l, n: (l, 0, n)),
        ],
        out_specs=pl.BlockSpec((1, b, d), lambda l, n: (l, 0, n)),
        out_shape=jax.ShapeDtypeStruct((depth, b, d6), F32),
        compiler_params=pltpu.CompilerParams(
            dimension_semantics=("arbitrary", "arbitrary"),
            vmem_limit_bytes=32 * 1024 * 1024),
        name="adaln_mod",
    )(c, w_ada, b_ada.reshape(depth, 1, d6))


def _cast_pad_kernel(w_ref, o_ref):
    rows, cols = w_ref.shape[2:]
    prow, pcol = o_ref.shape[2:]
    o_ref[0, 0, 0:rows, 0:cols] = w_ref[0, 0].astype(BF16)
    if pcol > cols:
        o_ref[0, 0, :, cols:] = jnp.zeros((prow, pcol - cols), BF16)
    if prow > rows:
        o_ref[0, 0, rows:, 0:cols] = jnp.zeros((prow - rows, cols), BF16)


def _cast_pad(w, prow, pcol):
    depth, ne, rows, cols = w.shape
    return pl.pallas_call(
        _cast_pad_kernel,
        grid=(depth, ne),
        in_specs=[pl.BlockSpec((1, 1, rows, cols), lambda l, e: (l, e, 0, 0))],
        out_specs=pl.BlockSpec((1, 1, prow, pcol), lambda l, e: (l, e, 0, 0)),
        out_shape=jax.ShapeDtypeStruct((depth, ne, prow, pcol), BF16),
        compiler_params=pltpu.CompilerParams(
            dimension_semantics=("arbitrary", "arbitrary"),
            vmem_limit_bytes=32 * 1024 * 1024),
        name="cast_pad",
    )(w)


def _route(logits_t, br):
    m = jnp.max(logits_t, axis=0, keepdims=True)
    e = jnp.exp(logits_t - m)
    probs = e / jnp.sum(e, axis=0, keepdims=True)
    sel = probs + br
    one = jnp.ones_like(m)
    zero = jnp.zeros_like(m)
    picked, gscore = [], []
    for g in range(N_GROUPS):
        rows = [sel[g * EXPERTS_PER_GROUP + k:g * EXPERTS_PER_GROUP + k + 1, :]
                for k in range(EXPERTS_PER_GROUP)]
        score = zero
        for k in range(EXPERTS_PER_GROUP):
            rank = zero
            for k2 in range(EXPERTS_PER_GROUP):
                if k2 == k:
                    continue
                ahead = (rows[k2] >= rows[k]) if k2 < k else (rows[k2] > rows[k])
                rank = rank + jnp.where(ahead, one, zero)
            pk = jnp.where(rank < 2.0, one, zero)
            picked.append(pk)
            score = score + pk * rows[k]
        gscore.append(score)
    best = gscore[0]
    for g in range(1, N_GROUPS):
        best = jnp.maximum(best, gscore[g])
    taken = zero
    flag = [zero] * EXPERTS_PER_GROUP
    prob = [zero] * EXPERTS_PER_GROUP
    gbase = zero
    for g in range(N_GROUPS):
        isb = jnp.where(gscore[g] == best, one, zero) * (one - taken)
        taken = taken + isb
        gbase = gbase + isb * float(g * PAIRS_PER_GROUP)
        for k in range(EXPERTS_PER_GROUP):
            ei = g * EXPERTS_PER_GROUP + k
            flag[k] = flag[k] + isb * picked[ei]
            prob[k] = prob[k] + isb * probs[ei:ei + 1, :]
    f0, f1, f2, f3 = flag
    pair = f0 * (f2 * 1.0 + f3 * 2.0) + (one - f0) * (f1 * (f2 * 3.0 + f3 * 4.0) + (one - f1) * 5.0)
    cls = gbase + pair
    wa_raw = f0 * prob[0] + (one - f0) * (f1 * prob[1] + (one - f1) * prob[2])
    tot = f0 * prob[0] + f1 * prob[1] + f2 * prob[2] + f3 * prob[3]
    wa = wa_raw / tot
    wb = (tot - wa_raw) / tot
    return cls, wa, wb


def _mix_kernel(x_ref, mod_ref, gpre_ref, gpost_ref, gffn_ref, wnat_ref, wt_ref, wdw_ref, bdw_ref,
                lng_ref, lnb_ref, wup_ref, bup_ref, gnorm_ref, wout_ref, wr_ref, br_ref,
                ubd_ref, ebd_ref, tri_ref,
                x1_ref, h2p_ref, route_ref,
                ubuf, ush, cbuf, st_ref, sbd_ref, ycat_ref, *, ts):
    j = pl.program_id(1)
    nch = ts // CHUNK

    @pl.when(j == 0)
    def _():
        ubuf[0:CONV_PAD, :] = jnp.zeros((CONV_PAD, D_CONV), F32)
        st_ref[...] = jnp.zeros_like(st_ref)
        sbd_ref[...] = jnp.zeros_like(sbd_ref)

    @pl.when(j > 0)
    def _():
        ubuf[0:CONV_PAD, :] = ubuf[ts:ts + CONV_PAD, :]

    x = x_ref[0]
    sh1 = mod_ref[0, 0:1, :]
    sc1 = mod_ref[0, 1:2, :]
    g1 = mod_ref[0, 2:3, :]
    sh2 = mod_ref[0, 3:4, :]
    sc2 = mod_ref[0, 4:5, :]

    h = _rms(x, gpre_ref[...] * (1.0 + sc1)) + sh1
    hb = h.astype(BF16)

    cvg = jnp.dot(hb, wnat_ref[:, 0:2 * D_CONV], preferred_element_type=F32)
    ubuf[CONV_PAD:CONV_PAD + ts, :] = cvg[:, 0:D_CONV] * _sigmoid(cvg[:, D_CONV:2 * D_CONV])

    for r in range(1, SUBLANES):
        ush[r - 1] = ubuf[r:r + ts + CONV_PAD - SUBLANES, :]

    groups = CONV_ROWS // SUBLANES
    first_off = CONV_PAD - (CONV_WIDTH - 1)

    def conv_block(rb, carry, c0):
        r0 = rb * CONV_ROWS
        cs = slice(c0, c0 + CONV_LANES)
        acc = [jnp.broadcast_to(bdw_ref[:, cs], (SUBLANES, CONV_LANES)) for _ in range(groups)]
        for r in range(SUBLANES):
            taps = [tap for tap in range(CONV_WIDTH) if (first_off + tap) % SUBLANES == r]
            shifts = [(first_off + tap) // SUBLANES for tap in taps]
            src = ubuf if r == 0 else ush.at[r - 1]
            blk = {m: src[pl.ds(r0 + m * SUBLANES, SUBLANES), cs]
                   for m in range(min(shifts), max(shifts) + groups)}
            for tap, a in zip(taps, shifts):
                w8 = wdw_ref[tap * SUBLANES:(tap + 1) * SUBLANES, cs]
                for g in range(groups):
                    acc[g] = acc[g] + w8 * blk[a + g]
        cbuf[pl.ds(r0, CONV_ROWS), cs] = jnp.concatenate(acc, axis=0)
        return carry

    for c0 in range(0, D_CONV, CONV_LANES):
        for rb in range(ts // CONV_ROWS):
            conv_block(rb, 0, c0)
    cv = cbuf[...]
    mu = jnp.mean(cv, axis=-1, keepdims=True)
    cen = cv - mu
    var = jnp.mean(cen * cen, axis=-1, keepdims=True)
    yn = cen * lax.rsqrt(var + EPS) * lng_ref[...] + lnb_ref[...]
    ycat_ref[:, 0:D_CONV] = (yn * _sigmoid(yn)).astype(BF16)

    qvo = jnp.dot(hb, wnat_ref[:, 2 * D_CONV:], preferred_element_type=F32)
    q = (qvo[:, 0:D_GLA_K] * (GLA_DK ** -0.5)).astype(BF16)
    v = qvo[:, D_GLA_K:D_GLA_K + D_GLA_V].astype(BF16)
    og = qvo[:, D_GLA_K + D_GLA_V:]
    tt = lax.dot_general(wt_ref[...], hb, NT_DIMS, preferred_element_type=F32)
    k_t = tt[0:D_GLA_K, :]
    lr_t = tt[D_GLA_K:, :].astype(BF16)
    z_t = jnp.dot(wup_ref[...], lr_t, preferred_element_type=F32) + bup_ref[...]
    la_t = (jnp.minimum(z_t, 0.0) - jnp.log(1.0 + jnp.exp(-jnp.abs(z_t)))) * (1.0 / GATE_TAU)
    la_b = la_t.astype(BF16)
    slab = ubd_ref.shape[0]
    bcum = jnp.concatenate(
        [jnp.dot(la_b[:, s0:s0 + slab], ubd_ref[...], preferred_element_type=F32)
         for s0 in range(0, ts, slab)], axis=1)
    bend = jnp.concatenate(
        [jnp.dot(la_b[:, s0:s0 + slab], ebd_ref[...], preferred_element_type=F32)
         for s0 in range(0, ts, slab)], axis=1)
    kd_t = (k_t * jnp.exp(bend - bcum)).astype(BF16)
    dec_t = jnp.exp(bend)

    st = [st_ref[hh] for hh in range(GLA_HEADS)]
    gn = gnorm_ref[...]
    for c in range(nch):
        lo, hi_ = c * CHUNK, (c + 1) * CHUNK
        kd_c = kd_t[:, lo:hi_]
        v_c = v[lo:hi_, :]
        for hh in range(GLA_HEADS):
            ks, ke = hh * GLA_DK, (hh + 1) * GLA_DK
            vs, ve = hh * GLA_DV, (hh + 1) * GLA_DV
            upd = jnp.dot(kd_c[ks:ke, :], v_c[:, vs:ve], preferred_element_type=F32)
            dec = jnp.broadcast_to(dec_t[ks:ke, lo:lo + 1], (GLA_DK, GLA_DV))
            st[hh] = dec * st[hh] + upd
            sbd_ref[ks:ke, vs:ve] = st[hh].astype(BF16)
        o_c = jnp.dot(q[lo:hi_, :], sbd_ref[...], preferred_element_type=F32)
        og_c = og[lo:hi_, :]
        for hh in range(GLA_HEADS):
            vs, ve = hh * GLA_DV, (hh + 1) * GLA_DV
            on = _rms(o_c[:, vs:ve], gn)
            gate = og_c[:, vs:ve]
            ycat_ref[lo:hi_, D_CONV + vs:D_CONV + ve] = (on * (gate * _sigmoid(gate))).astype(BF16)
    for hh in range(GLA_HEADS):
        st_ref[hh] = st[hh]

    y = jnp.dot(ycat_ref[...], wout_ref[...], preferred_element_type=F32)
    x1 = x + g1 * _rms(y, gpost_ref[...])
    x1_ref[0] = x1
    h2 = _rms(x1, gffn_ref[...] * (1.0 + sc2)) + sh2
    d = h2.shape[1]
    h2p_ref[0, :, 0:d] = h2
    h2_hi, h2_lo = _split_bf16(h2)
    lg2 = lax.dot_general(wr_ref[...], h2_hi, NT_DIMS, preferred_element_type=F32)
    lg1 = lax.dot_general(wr_ref[0:N_EXPERTS, :], h2_lo, NT_DIMS, preferred_element_type=F32)
    logits_t = lg2[0:N_EXPERTS, :] + lg2[N_EXPERTS:, :] + lg1
    cls, wa, wb = _route(logits_t, br_ref[...])
    class_ids = lax.broadcasted_iota(jnp.int32, (CLASS_ROWS, ts), 0).astype(F32)
    onehot_t = jnp.where(class_ids == cls, 1.0, 0.0)
    seen = jnp.dot(onehot_t.astype(BF16), tri_ref[...], preferred_element_type=F32)
    rank = jnp.sum(onehot_t * seen, axis=0, keepdims=True) - 1.0
    route_ref[0] = jnp.concatenate([cls, rank, jnp.zeros((SUBLANES - 2, ts), F32)], axis=0)
    h2p_ref[0, :, d:d + LANES] = jnp.transpose(jnp.broadcast_to(wa, (LANES, ts)))
    h2p_ref[0, :, d + LANES:] = jnp.transpose(jnp.broadcast_to(wb, (LANES, ts)))


def _mix(x, mod, lw, consts, ts):
    b, s, d = x.shape
    nst = s // ts
    full = lambda shape: pl.BlockSpec(shape, lambda bi, ji: (0,) * len(shape))
    in_specs = [
        pl.BlockSpec((1, ts, d), lambda bi, ji: (bi, ji, 0)),
        pl.BlockSpec((1, 6, d), lambda bi, ji: (bi, 0, 0)),
        full((1, d)), full((1, d)), full((1, d)),
        full(lw["wnat"].shape), full(lw["wt"].shape),
        full((CONV_WIDTH * SUBLANES, D_CONV)), full((1, D_CONV)), full((1, D_CONV)), full((1, D_CONV)),
        full((D_GLA_K, GATE_RANK)), full((D_GLA_K, 1)), full((1, GLA_DV)),
        full((d, d)), full((2 * N_EXPERTS, d)), full((N_EXPERTS, 1)),
        full(consts["ubd"].shape), full(consts["ebd"].shape), full((ts, ts)),
    ]
    out_specs = [
        pl.BlockSpec((1, ts, d), lambda bi, ji: (bi, ji, 0)),
        pl.BlockSpec((1, ts, d + 2 * LANES), lambda bi, ji: (bi, ji, 0)),
        pl.BlockSpec((1, 8, ts), lambda bi, ji: (bi * nst + ji, 0, 0)),
    ]
    out_shape = [
        jax.ShapeDtypeStruct((b, s, d), F32),
        jax.ShapeDtypeStruct((b, s, d + 2 * LANES), F32),
        jax.ShapeDtypeStruct((b * nst, 8, ts), F32),
    ]
    scratch = [
        pltpu.VMEM((CONV_PAD + ts, D_CONV), F32),
        pltpu.VMEM((SUBLANES - 1, CONV_PAD + ts - SUBLANES, D_CONV), F32),
        pltpu.VMEM((ts, D_CONV), F32),
        pltpu.VMEM((GLA_HEADS, GLA_DK, GLA_DV), F32),
        pltpu.VMEM((D_GLA_K, D_GLA_V), BF16),
        pltpu.VMEM((ts, d), BF16),
    ]
    return pl.pallas_call(
        functools.partial(_mix_kernel, ts=ts),
        grid=(b, nst),
        in_specs=in_specs,
        out_specs=out_specs,
        out_shape=out_shape,
        scratch_shapes=scratch,
        compiler_params=pltpu.CompilerParams(
            dimension_semantics=("arbitrary", "arbitrary"),
            vmem_limit_bytes=56 * 1024 * 1024),
        name="mix",
    )(x, mod, lw["gpre"], lw["gpost"], lw["gffn"], lw["wnat"], lw["wt"], lw["wdw"], lw["bdw"],
      lw["lng"], lw["lnb"], lw["wup"], lw["bup"], lw["gnorm"], lw["wout"], consts["wr"],
      consts["br"], consts["ubd"], consts["ebd"], consts["tri"])


def _row_copy(idx_ref, base, r, src_hbm, dst_vmem, sem):
    tok = idx_ref[base + r]
    return pltpu.make_async_copy(src_hbm.at[pl.ds(tok, 1), :], dst_vmem.at[pl.ds(r, 1), :], sem)


def _row_gather_start(idx_ref, base, src_hbm, dst_vmem, sem, n_rows, unrolled):
    if unrolled:
        for r in range(n_rows):
            _row_copy(idx_ref, base, r, src_hbm, dst_vmem, sem).start()
    else:
        def body(r, carry):
            _row_copy(idx_ref, base, r, src_hbm, dst_vmem, sem).start()
            return carry
        lax.fori_loop(0, n_rows, body, 0)


def _row_gather_wait(src_hbm, dst_vmem, sem, n_rows):
    pltpu.make_async_copy(src_hbm.at[pl.ds(0, n_rows), :], dst_vmem, sem).wait()


def _gather_pipeline(idx_ref, src_hbm, buf, sems, n_rows):
    i = pl.program_id(0)
    n = pl.num_programs(0)

    @pl.when(i == 0)
    def _():
        for k in range(GATHER_SLOTS - 1):
            @pl.when(k < n)
            def _():
                _row_gather_start(idx_ref, k * n_rows, src_hbm, buf.at[k], sems.at[k], n_rows, unrolled=False)

    ahead = i + GATHER_SLOTS - 1

    @pl.when(ahead < n)
    def _():
        aslot = lax.rem(ahead, GATHER_SLOTS)
        _row_gather_start(idx_ref, ahead * n_rows, src_hbm, buf.at[aslot], sems.at[aslot], n_rows,
                          unrolled=True)

    slot = lax.rem(i, GATHER_SLOTS)
    _row_gather_wait(src_hbm, buf.at[slot], sems.at[slot], n_rows)
    return slot


def _moe_kernel(te1_ref, te2_ref, tvalid_ref,
                hs_ref, w1a_ref, w3a_ref, w2a_ref, w1b_ref, w3b_ref, w2b_ref, gpost_ref,
                z_ref):
    i = pl.program_id(0)
    d = z_ref.shape[1]

    @pl.when(tvalid_ref[i] == 0)
    def _():
        z_ref[...] = jnp.zeros_like(z_ref)

    @pl.when(tvalid_ref[i] != 0)
    def _():
        h = hs_ref[:, 0:d].astype(BF16)
        y = None
        for e, (w1, w3, w2) in enumerate(((w1a_ref, w3a_ref, w2a_ref), (w1b_ref, w3b_ref, w2b_ref))):
            g_t = lax.dot_general(w1[0, 0], h, NT_DIMS, preferred_element_type=F32)
            u_t = lax.dot_general(w3[0, 0], h, NT_DIMS, preferred_element_type=F32)
            he_t = ((g_t * _sigmoid(g_t)) * u_t).astype(BF16)
            ye = lax.dot_general(he_t, w2[0, 0], TN_DIMS, preferred_element_type=F32)
            wcol = hs_ref[:, d + e * LANES:d + (e + 1) * LANES]
            ye = ye * jnp.concatenate([wcol] * (d // LANES), axis=1)
            y = ye if y is None else y + ye
        z_ref[...] = _rms(y, gpost_ref[...])


def _moe(hs, te1, te2, tvalid, w1, w3, w2, gpost, layer, tm):
    dx = hs.shape[1]
    d = dx - 2 * LANES
    nt = te1.shape[0]
    wspec_a = lambda shape: pl.BlockSpec((1, 1) + shape, lambda i, e1, e2, tv: (layer, e1[i], 0, 0))
    wspec_b = lambda shape: pl.BlockSpec((1, 1) + shape, lambda i, e1, e2, tv: (layer, e2[i], 0, 0))
    grid_spec = pltpu.PrefetchScalarGridSpec(
        num_scalar_prefetch=3,
        grid=(nt,),
        in_specs=[
            pl.BlockSpec((tm, dx), lambda i, e1, e2, tv: (i, 0)),
            wspec_a((D_EXPERT_PAD, d)), wspec_a((D_EXPERT_PAD, d)), wspec_a((D_EXPERT_PAD, d)),
            wspec_b((D_EXPERT_PAD, d)), wspec_b((D_EXPERT_PAD, d)), wspec_b((D_EXPERT_PAD, d)),
            pl.BlockSpec((1, d), lambda i, e1, e2, tv: (0, 0)),
        ],
        out_specs=pl.BlockSpec((tm, d), lambda i, e1, e2, tv: (i, 0)),
    )
    return pl.pallas_call(
        _moe_kernel,
        grid_spec=grid_spec,
        out_shape=jax.ShapeDtypeStruct((nt * tm, d), F32),
        compiler_params=pltpu.CompilerParams(
            dimension_semantics=("arbitrary",),
            vmem_limit_bytes=48 * 1024 * 1024),
        name="moe",
    )(te1, te2, tvalid, hs, w1, w3, w2, w1, w3, w2, gpost)


def _combine_kernel(pos_ref, x1_ref, g2_ref, z_hbm, o_ref, zg, sems, *, tc):
    slot = _gather_pipeline(pos_ref, z_hbm, zg, sems, tc)
    o_ref[...] = x1_ref[...] + g2_ref[0] * zg[slot]


def _combine(x1, g2, z, pos, seq_len, tc):
    t, d = x1.shape
    per_seq = seq_len // tc
    grid_spec = pltpu.PrefetchScalarGridSpec(
        num_scalar_prefetch=1,
        grid=(t // tc,),
        in_specs=[
            pl.BlockSpec((tc, d), lambda i, pos: (i, 0)),
            pl.BlockSpec((1, 1, d), lambda i, pos: (i // per_seq, 0, 0)),
            pl.BlockSpec(memory_space=pl.ANY),
        ],
        out_specs=pl.BlockSpec((tc, d), lambda i, pos: (i, 0)),
        scratch_shapes=[
            pltpu.VMEM((GATHER_SLOTS, tc, d), F32),
            pltpu.SemaphoreType.DMA((GATHER_SLOTS,)),
        ],
    )
    return pl.pallas_call(
        functools.partial(_combine_kernel, tc=tc),
        grid_spec=grid_spec,
        out_shape=jax.ShapeDtypeStruct((t, d), F32),
        compiler_params=pltpu.CompilerParams(
            dimension_semantics=("arbitrary",),
            vmem_limit_bytes=(GATHER_SLOTS + 4) * tc * d * 4 + VMEM_HEADROOM),
        name="combine",
    )(pos, x1, g2, z)


def _class_tables():
    e1, e2 = [], []
    for g in range(N_GROUPS):
        for a in range(EXPERTS_PER_GROUP):
            for b in range(a + 1, EXPERTS_PER_GROUP):
                e1.append(g * EXPERTS_PER_GROUP + a)
                e2.append(g * EXPERTS_PER_GROUP + b)
    return jnp.asarray(e1, jnp.int32), jnp.asarray(e2, jnp.int32)


def _dispatch_kernel(pos_ref, pad_start_ref, pad_len_ref, used_ref, h_hbm, zero_ref, hs_hbm,
                     buf, in_sems, row_sems, pad_sem, *, rows, tm):
    i = pl.program_id(0)
    n = pl.num_programs(0)
    n_tiles = hs_hbm.shape[0] // tm

    def for_each_pad_row(fn):
        for c in range(N_CLASSES):
            def body(j, carry, c=c):
                fn(pltpu.make_async_copy(zero_ref.at[pl.ds(0, 1), :],
                                         hs_hbm.at[pl.ds(pad_start_ref[c] + j, 1), :], pad_sem))
                return carry
            lax.fori_loop(0, pad_len_ref[c], body, 0)

        def tile_body(k, carry):
            fn(pltpu.make_async_copy(zero_ref, hs_hbm.at[pl.ds(pl.multiple_of(k * tm, tm), tm), :], pad_sem))
            return carry
        lax.fori_loop(used_ref[0], n_tiles, tile_body, 0)

    def block_in(blk, slot):
        start = pl.multiple_of(blk * rows, rows)
        return pltpu.make_async_copy(h_hbm.at[pl.ds(start, rows), :], buf.at[slot], in_sems.at[slot])

    def rows_wait(slot):
        pltpu.make_async_copy(buf.at[slot], hs_hbm.at[pl.ds(0, rows), :], row_sems.at[slot]).wait()

    @pl.when(i == 0)
    def _():
        for_each_pad_row(lambda cp: cp.start())
        for k in range(DISPATCH_AHEAD):
            @pl.when(k < n)
            def _():
                block_in(k, k).start()

    @pl.when(i >= DISPATCH_AHEAD)
    def _():
        rows_wait(lax.rem(i - DISPATCH_AHEAD, DISPATCH_SLOTS))

    @pl.when(i + DISPATCH_AHEAD < n)
    def _():
        block_in(i + DISPATCH_AHEAD, lax.rem(i + DISPATCH_AHEAD, DISPATCH_SLOTS)).start()

    slot = lax.rem(i, DISPATCH_SLOTS)
    block_in(i, slot).wait()
    base = i * rows
    for u in range(rows):
        pltpu.make_async_copy(buf.at[slot, pl.ds(u, 1), :], hs_hbm.at[pl.ds(pos_ref[base + u], 1), :],
                              row_sems.at[slot]).start()

    @pl.when(i == n - 1)
    def _():
        for k in range(DISPATCH_AHEAD):
            @pl.when(i - k >= 0)
            def _():
                rows_wait(lax.rem(i - k, DISPATCH_SLOTS))
        for_each_pad_row(lambda cp: cp.wait())


def _dispatch(h2p, pos, pad_start, pad_len, used_tiles, n_rows, rows, tm):
    t, dx = h2p.shape
    grid_spec = pltpu.PrefetchScalarGridSpec(
        num_scalar_prefetch=4,
        grid=(t // rows,),
        in_specs=[pl.BlockSpec(memory_space=pl.ANY),
                  pl.BlockSpec((tm, dx), lambda i, *_: (0, 0))],
        out_specs=pl.BlockSpec(memory_space=pl.ANY),
        scratch_shapes=[pltpu.VMEM((DISPATCH_SLOTS, rows, dx), F32),
                        pltpu.SemaphoreType.DMA((DISPATCH_SLOTS,)),
                        pltpu.SemaphoreType.DMA((DISPATCH_SLOTS,)),
                        pltpu.SemaphoreType.DMA],
    )
    return pl.pallas_call(
        functools.partial(_dispatch_kernel, rows=rows, tm=tm),
        grid_spec=grid_spec,
        out_shape=jax.ShapeDtypeStruct((n_rows, dx), F32),
        compiler_params=pltpu.CompilerParams(
            dimension_semantics=("arbitrary",),
            vmem_limit_bytes=(DISPATCH_SLOTS * rows + 2 * tm) * dx * 4 + VMEM_HEADROOM),
        name="dispatch",
    )(pos, pad_start, pad_len, used_tiles, h2p, jnp.zeros((tm, dx), F32))


def _plan(cls, rank, tm):
    t = cls.size
    nt = t // tm + N_CLASSES
    onehot = (cls[:, :, None] == jnp.arange(N_CLASSES, dtype=jnp.int32)[None, None, :]).astype(jnp.int32)
    per_tile = jnp.sum(onehot, axis=1)
    before = jnp.cumsum(per_tile, axis=0) - per_tile
    counts = jnp.sum(per_tile, axis=0)
    tiles = (counts + tm - 1) // tm
    tile_end = jnp.cumsum(tiles)
    tile_start = tile_end - tiles
    base = tile_start[None, :] * tm + before
    pos = (jnp.sum(onehot * base[:, None, :], axis=2) + rank).reshape(t)
    tile_ids = jnp.arange(nt, dtype=jnp.int32)
    tile_cls = jnp.sum((tile_end[None, :] <= tile_ids[:, None]).astype(jnp.int32), axis=1)
    tvalid = (tile_ids < tile_end[-1]).astype(jnp.int32)
    last_cls = jnp.max(jnp.where(counts > 0, jnp.arange(N_CLASSES, dtype=jnp.int32), 0))
    tile_cls = jnp.where(tvalid == 1, jnp.minimum(tile_cls, N_CLASSES - 1), last_cls)
    ce1, ce2 = _class_tables()
    pad_start = (tile_start * tm + counts).astype(jnp.int32)
    pad_len = (tiles * tm - counts).astype(jnp.int32)
    used_tiles = tile_end[-1:].astype(jnp.int32)
    return pos.astype(jnp.int32), pad_start, pad_len, used_tiles, ce1[tile_cls], ce2[tile_cls], tvalid


def _layer_weights(l, g_pre_mix, g_post_mix, g_pre_ffn, g_post_ffn, w_in, w_dw, b_dw, conv_ln_g,
                   conv_ln_b, w_gate_up, b_gate, gla_norm_g, w_out):
    d = w_in.shape[1]
    wi = w_in[l]
    o_q = 2 * D_CONV
    o_k = o_q + D_GLA_K
    o_v = o_k + D_GLA_K
    o_g = o_v + D_GLA_V
    o_lr = o_g + D_GLA_V
    wnat = jnp.concatenate([wi[:, 0:o_q], wi[:, o_q:o_k], wi[:, o_v:o_g], wi[:, o_g:o_lr]], axis=1)
    wt = jnp.concatenate([wi[:, o_k:o_v], wi[:, o_lr:]], axis=1).T
    return dict(
        gpre=g_pre_mix[l].reshape(1, d), gpost=g_post_mix[l].reshape(1, d),
        gffn=g_pre_ffn[l].reshape(1, d), gpostffn=g_post_ffn[l].reshape(1, d),
        wnat=wnat.astype(BF16), wt=wt.astype(BF16),
        wdw=jnp.repeat(w_dw[l], SUBLANES, axis=0), bdw=b_dw[l].reshape(1, D_CONV),
        lng=conv_ln_g[l].reshape(1, D_CONV), lnb=conv_ln_b[l].reshape(1, D_CONV),
        wup=w_gate_up[l].T.astype(BF16), bup=b_gate[l].reshape(D_GLA_K, 1),
        gnorm=gla_norm_g[l].reshape(1, GLA_DV),
        wout=w_out[l].astype(BF16),
    )


def kernel(x, c, w_ada, b_ada, g_pre_mix, g_post_mix, g_pre_ffn, g_post_ffn, w_in, w_dw, b_dw,
           conv_ln_g, conv_ln_b, w_gate_up, b_gate, gla_norm_g, w_out, w_router, b_router, w1, w3, w2):
    b, s, d = x.shape
    depth = w_ada.shape[0]
    t = b * s
    ts = min(SEQ_TILE, s)
    tm = min(MOE_TILE, t)
    tc = min(COMBINE_TILE, s)
    assert s % ts == 0 and ts % CHUNK == 0 and t % tm == 0 and s % tc == 0

    mod = _modulation(c, w_ada, b_ada).reshape(depth, b, 6, d)

    wr_hi, wr_lo = _split_bf16(w_router.T)
    tok = jnp.arange(min(ts, MXU_TILE), dtype=jnp.int32)
    same = (tok[:, None] // CHUNK) == (tok[None, :] // CHUNK)
    consts = dict(
        wr=jnp.concatenate([wr_hi, wr_lo], axis=0),
        br=b_router.reshape(N_EXPERTS, 1),
        ubd=(same & (tok[:, None] <= tok[None, :])).astype(BF16),
        ebd=same.astype(BF16),
        tri=(jnp.arange(ts)[:, None] <= jnp.arange(ts)[None, :]).astype(BF16),
    )

    w1b = _cast_pad(jnp.swapaxes(w1, 2, 3), D_EXPERT_PAD, d)
    w3b = _cast_pad(jnp.swapaxes(w3, 2, 3), D_EXPERT_PAD, d)
    w2b = _cast_pad(w2, D_EXPERT_PAD, d)

    for l in range(depth):
        lw = _layer_weights(l, g_pre_mix, g_post_mix, g_pre_ffn, g_post_ffn, w_in, w_dw, b_dw,
                            conv_ln_g, conv_ln_b, w_gate_up, b_gate, gla_norm_g, w_out)
        x1, h2p, route = _mix(x, mod[l], lw, consts, ts)
        cls = route[:, 0, :].astype(jnp.int32)
        rank = route[:, 1, :].astype(jnp.int32)
        pos, pad_start, pad_len, used_tiles, te1, te2, tvalid = _plan(cls, rank, tm)
        hs = _dispatch(h2p.reshape(t, d + 2 * LANES), pos, pad_start, pad_len, used_tiles,
                       te1.shape[0] * tm, min(DISPATCH_ROWS, t), tm)
        z = _moe(hs, te1, te2, tvalid, w1b, w3b, w2b, lw["gpostffn"], l, tm)
        g2 = mod[l][:, 5:6, :]
        x = _combine(x1.reshape(t, d), g2, z, pos, s, tc).reshape(b, s, d)
    return x
```

```python
import functools

import jax
import jax.numpy as jnp
from jax import lax
from jax.experimental import pallas as pl
from jax.experimental.pallas import tpu as pltpu

CHUNK = 64
CONV_WIDTH = 31
D_CONV = 512
GLA_HEADS = 4
GLA_DV = 128
GLA_DK = 64
D_GLA_K = GLA_HEADS * GLA_DK
D_GLA_V = GLA_HEADS * GLA_DV
GATE_RANK = 16
GATE_TAU = 16.0
N_EXPERTS = 16
N_GROUPS = 4
EXPERTS_PER_GROUP = 4
PAIRS_PER_GROUP = 6
N_CLASSES = N_GROUPS * PAIRS_PER_GROUP
CLASS_ROWS = 32
D_EXPERT = 704
D_EXPERT_PAD = 768
EPS = 1e-6
LOG2_E = 1.4426950408889634

SUBLANES = 8
LANES = 128
MXU_TILE = 256
CONV_PAD = 32
CONV_ROWS = 32
CONV_LANES = 256
SEQ_TILE = 512
MOE_TILE = 256
COMBINE_TILE = 512
DISPATCH_ROWS = 512
DISPATCH_AHEAD = 2
DISPATCH_SLOTS = 2 * DISPATCH_AHEAD
VMEM_HEADROOM = 8 * 1024 * 1024
DMA_QUEUES = 2
GATHER_SLOTS = 3

F32 = jnp.float32
BF16 = jnp.bfloat16
NT_DIMS = (((1,), (1,)), ((), ()))
TN_DIMS = (((0,), (0,)), ((), ()))


def _sigmoid(x):
    return 1.0 / (1.0 + jnp.exp2(x * -LOG2_E))


def _rms(x, g):
    ms = jnp.mean(x * x, axis=-1, keepdims=True)
    return x * lax.rsqrt(ms + EPS) * g


def _split_bf16(x):
    hi = x.astype(BF16)
    lo = (x - hi.astype(F32)).astype(BF16)
    return hi, lo


def _mod_kernel(c_ref, w_ref, b_ref, o_ref):
    c = c_ref[...]
    ca = c * _sigmoid(c)
    a_hi, a_lo = _split_bf16(ca)
    w_hi, w_lo = _split_bf16(w_ref[0])
    acc = jnp.dot(a_hi, w_hi, preferred_element_type=F32)
    acc += jnp.dot(a_lo, w_hi, preferred_element_type=F32)
    acc += jnp.dot(a_hi, w_lo, preferred_element_type=F32)
    o_ref[0] = acc + b_ref[0]


def _modulation(c, w_ada, b_ada):
    depth, d, d6 = w_ada.shape
    b = c.shape[0]
    nblk = d6 // d
    return pl.pallas_call(
        _mod_kernel,
        grid=(depth, nblk),
        in_specs=[
            pl.BlockSpec((b, d), lambda l, n: (0, 0)),
            pl.BlockSpec((1, d, d), lambda l, n: (l, 0, n)),
            pl.BlockSpec((1, 1, d), lambda l, n: (l, 0, n)),
        ],
        out_specs=pl.BlockSpec((1, b, d), lambda l, n: (l, 0, n)),
        out_shape=jax.ShapeDtypeStruct((depth, b, d6), F32),
        compiler_params=pltpu.CompilerParams(
            dimension_semantics=("arbitrary", "arbitrary"),
            vmem_limit_bytes=32 * 1024 * 1024),
        name="adaln_mod",
    )(c, w_ada, b_ada.reshape(depth, 1, d6))


def _cast_pad_kernel(w_ref, o_ref):
    rows, cols = w_ref.shape[2:]
    prow, pcol = o_ref.shape[2:]
    o_ref[0, 0, 0:rows, 0:cols] = w_ref[0, 0].astype(BF16)
    if pcol > cols:
        o_ref[0, 0, :, cols:] = jnp.zeros((prow, pcol - cols), BF16)
    if prow > rows:
        o_ref[0, 0, rows:, 0:cols] = jnp.zeros((prow - rows, cols), BF16)


def _cast_pad(w, prow, pcol):
    depth, ne, rows, cols = w.shape
    return pl.pallas_call(
        _cast_pad_kernel,
        grid=(depth, ne),
        in_specs=[pl.BlockSpec((1, 1, rows, cols), lambda l, e: (l, e, 0, 0))],
        out_specs=pl.BlockSpec((1, 1, prow, pcol), lambda l, e: (l, e, 0, 0)),
        out_shape=jax.ShapeDtypeStruct((depth, ne, prow, pcol), BF16),
        compiler_params=pltpu.CompilerParams(
            dimension_semantics=("arbitrary", "arbitrary"),
            vmem_limit_bytes=32 * 1024 * 1024),
        name="cast_pad",
    )(w)


def _route(logits_t, br):
    m = jnp.max(logits_t, axis=0, keepdims=True)
    e = jnp.exp(logits_t - m)
    probs = e / jnp.sum(e, axis=0, keepdims=True)
    sel = probs + br
    one = jnp.ones_like(m)
    zero = jnp.zeros_like(m)
    picked, gscore = [], []
    for g in range(N_GROUPS):
        rows = [sel[g * EXPERTS_PER_GROUP + k:g * EXPERTS_PER_GROUP + k + 1, :]
                for k in range(EXPERTS_PER_GROUP)]
        score = zero
        for k in range(EXPERTS_PER_GROUP):
            rank = zero
            for k2 in range(EXPERTS_PER_GROUP):
                if k2 == k:
                    continue
                ahead = (rows[k2] >= rows[k]) if k2 < k else (rows[k2] > rows[k])
                rank = rank + jnp.where(ahead, one, zero)
            pk = jnp.where(rank < 2.0, one, zero)
            picked.append(pk)
            score = score + pk * rows[k]
        gscore.append(score)
    best = gscore[0]
    for g in range(1, N_GROUPS):
        best = jnp.maximum(best, gscore[g])
    taken = zero
    flag = [zero] * EXPERTS_PER_GROUP
    prob = [zero] * EXPERTS_PER_GROUP
    gbase = zero
    for g in range(N_GROUPS):
        isb = jnp.where(gscore[g] == best, one, zero) * (one - taken)
        taken = taken + isb
        gbase = gbase + isb * float(g * PAIRS_PER_GROUP)
        for k in range(EXPERTS_PER_GROUP):
            ei = g * EXPERTS_PER_GROUP + k
            flag[k] = flag[k] + isb * picked[ei]
            prob[k] = prob[k] + isb * probs[ei:ei + 1, :]
    f0, f1, f2, f3 = flag
    pair = f0 * (f2 * 1.0 + f3 * 2.0) + (one - f0) * (f1 * (f2 * 3.0 + f3 * 4.0) + (one - f1) * 5.0)
    cls = gbase + pair
    wa_raw = f0 * prob[0] + (one - f0) * (f1 * prob[1] + (one - f1) * prob[2])
    tot = f0 * prob[0] + f1 * prob[1] + f2 * prob[2] + f3 * prob[3]
    wa = wa_raw / tot
    wb = (tot - wa_raw) / tot
    return cls, wa, wb


def _mix_kernel(x_ref, mod_ref, gpre_ref, gpost_ref, gffn_ref, wnat_ref, wt_ref, wdw_ref, bdw_ref,
                lng_ref, lnb_ref, wup_ref, bup_ref, gnorm_ref, wout_ref, wr_ref, br_ref,
                ubd_ref, ebd_ref, tri_ref,
                x1_ref, h2p_ref, route_ref,
                ubuf, ush, cbuf, st_ref, sbd_ref, ycat_ref, *, ts):
    j = pl.program_id(1)
    nch = ts // CHUNK

    @pl.when(j == 0)
    def _():
        ubuf[0:CONV_PAD, :] = jnp.zeros((CONV_PAD, D_CONV), F32)
        st_ref[...] = jnp.zeros_like(st_ref)
        sbd_ref[...] = jnp.zeros_like(sbd_ref)

    @pl.when(j > 0)
    def _():
        ubuf[0:CONV_PAD, :] = ubuf[ts:ts + CONV_PAD, :]

    x = x_ref[0]
    sh1 = mod_ref[0, 0:1, :]
    sc1 = mod_ref[0, 1:2, :]
    g1 = mod_ref[0, 2:3, :]
    sh2 = mod_ref[0, 3:4, :]
    sc2 = mod_ref[0, 4:5, :]

    h = _rms(x, gpre_ref[...] * (1.0 + sc1)) + sh1
    hb = h.astype(BF16)

    cvg = jnp.dot(hb, wnat_ref[:, 0:2 * D_CONV], preferred_element_type=F32)
    ubuf[CONV_PAD:CONV_PAD + ts, :] = cvg[:, 0:D_CONV] * _sigmoid(cvg[:, D_CONV:2 * D_CONV])

    for r in range(1, SUBLANES):
        ush[r - 1] = ubuf[r:r + ts + CONV_PAD - SUBLANES, :]

    groups = CONV_ROWS // SUBLANES
    first_off = CONV_PAD - (CONV_WIDTH - 1)

    def conv_block(rb, carry, c0):
        r0 = rb * CONV_ROWS
        cs = slice(c0, c0 + CONV_LANES)
        acc = [jnp.broadcast_to(bdw_ref[:, cs], (SUBLANES, CONV_LANES)) for _ in range(groups)]
        for r in range(SUBLANES):
            taps = [tap for tap in range(CONV_WIDTH) if (first_off + tap) % SUBLANES == r]
            shifts = [(first_off + tap) // SUBLANES for tap in taps]
            src = ubuf if r == 0 else ush.at[r - 1]
            blk = {m: src[pl.ds(r0 + m * SUBLANES, SUBLANES), cs]
                   for m in range(min(shifts), max(shifts) + groups)}
            for tap, a in zip(taps, shifts):
                w8 = wdw_ref[tap * SUBLANES:(tap + 1) * SUBLANES, cs]
                for g in range(groups):
                    acc[g] = acc[g] + w8 * blk[a + g]
        cbuf[pl.ds(r0, CONV_ROWS), cs] = jnp.concatenate(acc, axis=0)
        return carry

    for c0 in range(0, D_CONV, CONV_LANES):
        for rb in range(ts // CONV_ROWS):
            conv_block(rb, 0, c0)
    cv = cbuf[...]
    mu = jnp.mean(cv, axis=-1, keepdims=True)
    cen = cv - mu
    var = jnp.mean(cen * cen, axis=-1, keepdims=True)
    yn = cen * lax.rsqrt(var + EPS) * lng_ref[...] + lnb_ref[...]
    ycat_ref[:, 0:D_CONV] = (yn * _sigmoid(yn)).astype(BF16)

    qvo = jnp.dot(hb, wnat_ref[:, 2 * D_CONV:], preferred_element_type=F32)
    q = (qvo[:, 0:D_GLA_K] * (GLA_DK ** -0.5)).astype(BF16)
    v = qvo[:, D_GLA_K:D_GLA_K + D_GLA_V].astype(BF16)
    og = qvo[:, D_GLA_K + D_GLA_V:]
    tt = lax.dot_general(wt_ref[...], hb, NT_DIMS, preferred_element_type=F32)
    k_t = tt[0:D_GLA_K, :]
    lr_t = tt[D_GLA_K:, :].astype(BF16)
    z_t = jnp.dot(wup_ref[...], lr_t, preferred_element_type=F32) + bup_ref[...]
    la_t = (jnp.minimum(z_t, 0.0) - jnp.log(1.0 + jnp.exp(-jnp.abs(z_t)))) * (1.0 / GATE_TAU)
    la_b = la_t.astype(BF16)
    slab = ubd_ref.shape[0]
    bcum = jnp.concatenate(
        [jnp.dot(la_b[:, s0:s0 + slab], ubd_ref[...], preferred_element_type=F32)
         for s0 in range(0, ts, slab)], axis=1)
    bend = jnp.concatenate(
        [jnp.dot(la_b[:, s0:s0 + slab], ebd_ref[...], preferred_element_type=F32)
         for s0 in range(0, ts, slab)], axis=1)
    kd_t = (k_t * jnp.exp(bend - bcum)).astype(BF16)
    dec_t = jnp.exp(bend)

    st = [st_ref[hh] for hh in range(GLA_HEADS)]
    gn = gnorm_ref[...]
    for c in range(nch):
        lo, hi_ = c * CHUNK, (c + 1) * CHUNK
        kd_c = kd_t[:, lo:hi_]
        v_c = v[lo:hi_, :]
        for hh in range(GLA_HEADS):
            ks, ke = hh * GLA_DK, (hh + 1) * GLA_DK
            vs, ve = hh * GLA_DV, (hh + 1) * GLA_DV
            upd = jnp.dot(kd_c[ks:ke, :], v_c[:, vs:ve], preferred_element_type=F32)
            dec = jnp.broadcast_to(dec_t[ks:ke, lo:lo + 1], (GLA_DK, GLA_DV))
            st[hh] = dec * st[hh] + upd
            sbd_ref[ks:ke, vs:ve] = st[hh].astype(BF16)
        o_c = jnp.dot(q[lo:hi_, :], sbd_ref[...], preferred_element_type=F32)
        og_c = og[lo:hi_, :]
        for hh in range(GLA_HEADS):
            vs, ve = hh * GLA_DV, (hh + 1) * GLA_DV
            on = _rms(o_c[:, vs:ve], gn)
            gate = og_c[:, vs:ve]
            ycat_ref[lo:hi_, D_CONV + vs:D_CONV + ve] = (on * (gate * _sigmoid(gate))).astype(BF16)
    for hh in range(GLA_HEADS):
        st_ref[hh] = st[hh]

    y = jnp.dot(ycat_ref[...], wout_ref[...], preferred_element_type=F32)
    x1 = x + g1 * _rms(y, gpost_ref[...])
    x1_ref[0] = x1
    h2 = _rms(x1, gffn_ref[...] * (1.0 + sc2)) + sh2
    d = h2.shape[1]
    h2p_ref[0, :, 0:d] = h2
    h2_hi, h2_lo = _split_bf16(h2)
    lg2 = lax.dot_general(wr_ref[...], h2_hi, NT_DIMS, preferred_element_type=F32)
    lg1 = lax.dot_general(wr_ref[0:N_EXPERTS, :], h2_lo, NT_DIMS, preferred_element_type=F32)
    logits_t = lg2[0:N_EXPERTS, :] + lg2[N_EXPERTS:, :] + lg1
    cls, wa, wb = _route(logits_t, br_ref[...])
    class_ids = lax.broadcasted_iota(jnp.int32, (CLASS_ROWS, ts), 0).astype(F32)
    onehot_t = jnp.where(class_ids == cls, 1.0, 0.0)
    seen = jnp.dot(onehot_t.astype(BF16), tri_ref[...], preferred_element_type=F32)
    rank = jnp.sum(onehot_t * seen, axis=0, keepdims=True) - 1.0
    route_ref[0] = jnp.concatenate([cls, rank, jnp.zeros((SUBLANES - 2, ts), F32)], axis=0)
    h2p_ref[0, :, d:d + LANES] = jnp.transpose(jnp.broadcast_to(wa, (LANES, ts)))
    h2p_ref[0, :, d + LANES:] = jnp.transpose(jnp.broadcast_to(wb, (LANES, ts)))


def _mix(x, mod, lw, consts, ts):
    b, s, d = x.shape
    nst = s // ts
    full = lambda shape: pl.BlockSpec(shape, lambda bi, ji: (0,) * len(shape))
    in_specs = [
        pl.BlockSpec((1, ts, d), lambda bi, ji: (bi, ji, 0)),
        pl.BlockSpec((1, 6, d), lambda bi, ji: (bi, 0, 0)),
        full((1, d)), full((1, d)), full((1, d)),
        full(lw["wnat"].shape), full(lw["wt"].shape),
        full((CONV_WIDTH * SUBLANES, D_CONV)), full((1, D_CONV)), full((1, D_CONV)), full((1, D_CONV)),
        full((D_GLA_K, GATE_RANK)), full((D_GLA_K, 1)), full((1, GLA_DV)),
        full((d, d)), full((2 * N_EXPERTS, d)), full((N_EXPERTS, 1)),
        full(consts["ubd"].shape), full(consts["ebd"].shape), full((ts, ts)),
    ]
    out_specs = [
        pl.BlockSpec((1, ts, d), lambda bi, ji: (bi, ji, 0)),
        pl.BlockSpec((1, ts, d + 2 * LANES), lambda bi, ji: (bi, ji, 0)),
        pl.BlockSpec((1, 8, ts), lambda bi, ji: (bi * nst + ji, 0, 0)),
    ]
    out_shape = [
        jax.ShapeDtypeStruct((b, s, d), F32),
        jax.ShapeDtypeStruct((b, s, d + 2 * LANES), F32),
        jax.ShapeDtypeStruct((b * nst, 8, ts), F32),
    ]
    scratch = [
        pltpu.VMEM((CONV_PAD + ts, D_CONV), F32),
        pltpu.VMEM((SUBLANES - 1, CONV_PAD + ts - SUBLANES, D_CONV), F32),
        pltpu.VMEM((ts, D_CONV), F32),
        pltpu.VMEM((GLA_HEADS, GLA_DK, GLA_DV), F32),
        pltpu.VMEM((D_GLA_K, D_GLA_V), BF16),
        pltpu.VMEM((ts, d), BF16),
    ]
    return pl.pallas_call(
        functools.partial(_mix_kernel, ts=ts),
        grid=(b, nst),
        in_specs=in_specs,
        out_specs=out_specs,
        out_shape=out_shape,
        scratch_shapes=scratch,
        compiler_params=pltpu.CompilerParams(
            dimension_semantics=("arbitrary", "arbitrary"),
            vmem_limit_bytes=56 * 1024 * 1024),
        name="mix",
    )(x, mod, lw["gpre"], lw["gpost"], lw["gffn"], lw["wnat"], lw["wt"], lw["wdw"], lw["bdw"],
      lw["lng"], lw["lnb"], lw["wup"], lw["bup"], lw["gnorm"], lw["wout"], consts["wr"],
      consts["br"], consts["ubd"], consts["ebd"], consts["tri"])


def _row_copy(idx_ref, base, r, src_hbm, dst_vmem, sem):
    tok = idx_ref[base + r]
    return pltpu.make_async_copy(src_hbm.at[pl.ds(tok, 1), :], dst_vmem.at[pl.ds(r, 1), :], sem)


def _row_gather_start(idx_ref, base, src_hbm, dst_vmem, sem, n_rows, unrolled):
    if unrolled:
        for r in range(n_rows):
            _row_copy(idx_ref, base, r, src_hbm, dst_vmem, sem).start(priority=r % DMA_QUEUES)
    else:
        def body(r, carry):
            _row_copy(idx_ref, base, r, src_hbm, dst_vmem, sem).start()
            return carry
        lax.fori_loop(0, n_rows, body, 0)


def _row_gather_wait(src_hbm, dst_vmem, sem, n_rows):
    pltpu.make_async_copy(src_hbm.at[pl.ds(0, n_rows), :], dst_vmem, sem).wait()


def _gather_pipeline(idx_ref, src_hbm, buf, sems, n_rows):
    i = pl.program_id(0)
    n = pl.num_programs(0)

    @pl.when(i == 0)
    def _():
        for k in range(GATHER_SLOTS - 1):
            @pl.when(k < n)
            def _():
                _row_gather_start(idx_ref, k * n_rows, src_hbm, buf.at[k], sems.at[k], n_rows, unrolled=False)

    ahead = i + GATHER_SLOTS - 1

    @pl.when(ahead < n)
    def _():
        aslot = lax.rem(ahead, GATHER_SLOTS)
        _row_gather_start(idx_ref, ahead * n_rows, src_hbm, buf.at[aslot], sems.at[aslot], n_rows,
                          unrolled=True)

    slot = lax.rem(i, GATHER_SLOTS)
    _row_gather_wait(src_hbm, buf.at[slot], sems.at[slot], n_rows)
    return slot


def _moe_kernel(te1_ref, te2_ref, tvalid_ref,
                hs_ref, w1a_ref, w3a_ref, w2a_ref, w1b_ref, w3b_ref, w2b_ref, gpost_ref,
                z_ref):
    i = pl.program_id(0)
    d = z_ref.shape[1]

    @pl.when(tvalid_ref[i] == 0)
    def _():
        z_ref[...] = jnp.zeros_like(z_ref)

    @pl.when(tvalid_ref[i] != 0)
    def _():
        h = hs_ref[:, 0:d].astype(BF16)
        y = None
        for e, (w1, w3, w2) in enumerate(((w1a_ref, w3a_ref, w2a_ref), (w1b_ref, w3b_ref, w2b_ref))):
            g_t = lax.dot_general(w1[0, 0], h, NT_DIMS, preferred_element_type=F32)
            u_t = lax.dot_general(w3[0, 0], h, NT_DIMS, preferred_element_type=F32)
            he_t = ((g_t * _sigmoid(g_t)) * u_t).astype(BF16)
            ye = lax.dot_general(he_t, w2[0, 0], TN_DIMS, preferred_element_type=F32)
            wcol = hs_ref[:, d + e * LANES:d + (e + 1) * LANES]
            ye = ye * jnp.concatenate([wcol] * (d // LANES), axis=1)
            y = ye if y is None else y + ye
        z_ref[...] = _rms(y, gpost_ref[...])


def _moe(hs, te1, te2, tvalid, w1, w3, w2, gpost, layer, tm):
    dx = hs.shape[1]
    d = dx - 2 * LANES
    nt = te1.shape[0]
    wspec_a = lambda shape: pl.BlockSpec((1, 1) + shape, lambda i, e1, e2, tv: (layer, e1[i], 0, 0))
    wspec_b = lambda shape: pl.BlockSpec((1, 1) + shape, lambda i, e1, e2, tv: (layer, e2[i], 0, 0))
    grid_spec = pltpu.PrefetchScalarGridSpec(
        num_scalar_prefetch=3,
        grid=(nt,),
        in_specs=[
            pl.BlockSpec((tm, dx), lambda i, e1, e2, tv: (i, 0)),
            wspec_a((D_EXPERT_PAD, d)), wspec_a((D_EXPERT_PAD, d)), wspec_a((D_EXPERT_PAD, d)),
            wspec_b((D_EXPERT_PAD, d)), wspec_b((D_EXPERT_PAD, d)), wspec_b((D_EXPERT_PAD, d)),
            pl.BlockSpec((1, d), lambda i, e1, e2, tv: (0, 0)),
        ],
        out_specs=pl.BlockSpec((tm, d), lambda i, e1, e2, tv: (i, 0)),
    )
    return pl.pallas_call(
        _moe_kernel,
        grid_spec=grid_spec,
        out_shape=jax.ShapeDtypeStruct((nt * tm, d), F32),
        compiler_params=pltpu.CompilerParams(
            dimension_semantics=("arbitrary",),
            vmem_limit_bytes=48 * 1024 * 1024),
        name="moe",
    )(te1, te2, tvalid, hs, w1, w3, w2, w1, w3, w2, gpost)


def _combine_kernel(pos_ref, x1_ref, g2_ref, z_hbm, o_ref, zg, sems, *, tc):
    slot = _gather_pipeline(pos_ref, z_hbm, zg, sems, tc)
    o_ref[...] = x1_ref[...] + g2_ref[0] * zg[slot]


def _combine(x1, g2, z, pos, seq_len, tc):
    t, d = x1.shape
    per_seq = seq_len // tc
    grid_spec = pltpu.PrefetchScalarGridSpec(
        num_scalar_prefetch=1,
        grid=(t // tc,),
        in_specs=[
            pl.BlockSpec((tc, d), lambda i, pos: (i, 0)),
            pl.BlockSpec((1, 1, d), lambda i, pos: (i // per_seq, 0, 0)),
            pl.BlockSpec(memory_space=pl.ANY),
        ],
        out_specs=pl.BlockSpec((tc, d), lambda i, pos: (i, 0)),
        scratch_shapes=[
            pltpu.VMEM((GATHER_SLOTS, tc, d), F32),
            pltpu.SemaphoreType.DMA((GATHER_SLOTS,)),
        ],
    )
    return pl.pallas_call(
        functools.partial(_combine_kernel, tc=tc),
        grid_spec=grid_spec,
        out_shape=jax.ShapeDtypeStruct((t, d), F32),
        compiler_params=pltpu.CompilerParams(
            dimension_semantics=("arbitrary",),
            vmem_limit_bytes=(GATHER_SLOTS + 4) * tc * d * 4 + VMEM_HEADROOM),
        name="combine",
    )(pos, x1, g2, z)


def _class_tables():
    e1, e2 = [], []
    for g in range(N_GROUPS):
        for a in range(EXPERTS_PER_GROUP):
            for b in range(a + 1, EXPERTS_PER_GROUP):
                e1.append(g * EXPERTS_PER_GROUP + a)
                e2.append(g * EXPERTS_PER_GROUP + b)
    return jnp.asarray(e1, jnp.int32), jnp.asarray(e2, jnp.int32)


def _dispatch_kernel(pos_ref, pad_start_ref, pad_len_ref, used_ref, h_hbm, zero_ref, hs_hbm,
                     buf, in_sems, row_sems, pad_sem, *, rows, tm):
    i = pl.program_id(0)
    n = pl.num_programs(0)
    n_tiles = hs_hbm.shape[0] // tm

    def for_each_pad_row(fn):
        for c in range(N_CLASSES):
            def body(j, carry, c=c):
                fn(pltpu.make_async_copy(zero_ref.at[pl.ds(0, 1), :],
                                         hs_hbm.at[pl.ds(pad_start_ref[c] + j, 1), :], pad_sem))
                return carry
            lax.fori_loop(0, pad_len_ref[c], body, 0)

        def tile_body(k, carry):
            fn(pltpu.make_async_copy(zero_ref, hs_hbm.at[pl.ds(pl.multiple_of(k * tm, tm), tm), :], pad_sem))
            return carry
        lax.fori_loop(used_ref[0], n_tiles, tile_body, 0)

    def block_in(blk, slot):
        start = pl.multiple_of(blk * rows, rows)
        return pltpu.make_async_copy(h_hbm.at[pl.ds(start, rows), :], buf.at[slot], in_sems.at[slot])

    def rows_wait(slot):
        pltpu.make_async_copy(buf.at[slot], hs_hbm.at[pl.ds(0, rows), :], row_sems.at[slot]).wait()

    @pl.when(i == 0)
    def _():
        for_each_pad_row(lambda cp: cp.start())
        for k in range(DISPATCH_AHEAD):
            @pl.when(k < n)
            def _():
                block_in(k, k).start()

    @pl.when(i >= DISPATCH_AHEAD)
    def _():
        rows_wait(lax.rem(i - DISPATCH_AHEAD, DISPATCH_SLOTS))

    @pl.when(i + DISPATCH_AHEAD < n)
    def _():
        block_in(i + DISPATCH_AHEAD, lax.rem(i + DISPATCH_AHEAD, DISPATCH_SLOTS)).start()

    slot = lax.rem(i, DISPATCH_SLOTS)
    block_in(i, slot).wait()
    base = i * rows
    for u in range(rows):
        pltpu.make_async_copy(buf.at[slot, pl.ds(u, 1), :], hs_hbm.at[pl.ds(pos_ref[base + u], 1), :],
                              row_sems.at[slot]).start()

    @pl.when(i == n - 1)
    def _():
        for k in range(DISPATCH_AHEAD):
            @pl.when(i - k >= 0)
            def _():
                rows_wait(lax.rem(i - k, DISPATCH_SLOTS))
        for_each_pad_row(lambda cp: cp.wait())


def _dispatch(h2p, pos, pad_start, pad_len, used_tiles, n_rows, rows, tm):
    t, dx = h2p.shape
    grid_spec = pltpu.PrefetchScalarGridSpec(
        num_scalar_prefetch=4,
        grid=(t // rows,),
        in_specs=[pl.BlockSpec(memory_space=pl.ANY),
                  pl.BlockSpec((tm, dx), lambda i, *_: (0, 0))],
        out_specs=pl.BlockSpec(memory_space=pl.ANY),
        scratch_shapes=[pltpu.VMEM((DISPATCH_SLOTS, rows, dx), F32),
                        pltpu.SemaphoreType.DMA((DISPATCH_SLOTS,)),
                        pltpu.SemaphoreType.DMA((DISPATCH_SLOTS,)),
                        pltpu.SemaphoreType.DMA],
    )
    return pl.pallas_call(
        functools.partial(_dispatch_kernel, rows=rows, tm=tm),
        grid_spec=grid_spec,
        out_shape=jax.ShapeDtypeStruct((n_rows, dx), F32),
        compiler_params=pltpu.CompilerParams(
            dimension_semantics=("arbitrary",),
            vmem_limit_bytes=(DISPATCH_SLOTS * rows + 2 * tm) * dx * 4 + VMEM_HEADROOM),
        name="dispatch",
    )(pos, pad_start, pad_len, used_tiles, h2p, jnp.zeros((tm, dx), F32))


def _plan(cls, rank, tm):
    t = cls.size
    nt = t // tm + N_CLASSES
    onehot = (cls[:, :, None] == jnp.arange(N_CLASSES, dtype=jnp.int32)[None, None, :]).astype(jnp.int32)
    per_tile = jnp.sum(onehot, axis=1)
    before = jnp.cumsum(per_tile, axis=0) - per_tile
    counts = jnp.sum(per_tile, axis=0)
    tiles = (counts + tm - 1) // tm
    tile_end = jnp.cumsum(tiles)
    tile_start = tile_end - tiles
    base = tile_start[None, :] * tm + before
    pos = (jnp.sum(onehot * base[:, None, :], axis=2) + rank).reshape(t)
    tile_ids = jnp.arange(nt, dtype=jnp.int32)
    tile_cls = jnp.sum((tile_end[None, :] <= tile_ids[:, None]).astype(jnp.int32), axis=1)
    tvalid = (tile_ids < tile_end[-1]).astype(jnp.int32)
    last_cls = jnp.max(jnp.where(counts > 0, jnp.arange(N_CLASSES, dtype=jnp.int32), 0))
    tile_cls = jnp.where(tvalid == 1, jnp.minimum(tile_cls, N_CLASSES - 1), last_cls)
    ce1, ce2 = _class_tables()
    pad_start = (tile_start * tm + counts).astype(jnp.int32)
    pad_len = (tiles * tm - counts).astype(jnp.int32)
    used_tiles = tile_end[-1:].astype(jnp.int32)
    return pos.astype(jnp.int32), pad_start, pad_len, used_tiles, ce1[tile_cls], ce2[tile_cls], tvalid


def _layer_weights(l, g_pre_mix, g_post_mix, g_pre_ffn, g_post_ffn, w_in, w_dw, b_dw, conv_ln_g,
                   conv_ln_b, w_gate_up, b_gate, gla_norm_g, w_out):
    d = w_in.shape[1]
    wi = w_in[l]
    o_q = 2 * D_CONV
    o_k = o_q + D_GLA_K
    o_v = o_k + D_GLA_K
    o_g = o_v + D_GLA_V
    o_lr = o_g + D_GLA_V
    wnat = jnp.concatenate([wi[:, 0:o_q], wi[:, o_q:o_k], wi[:, o_v:o_g], wi[:, o_g:o_lr]], axis=1)
    wt = jnp.concatenate([wi[:, o_k:o_v], wi[:, o_lr:]], axis=1).T
    return dict(
        gpre=g_pre_mix[l].reshape(1, d), gpost=g_post_mix[l].reshape(1, d),
        gffn=g_pre_ffn[l].reshape(1, d), gpostffn=g_post_ffn[l].reshape(1, d),
        wnat=wnat.astype(BF16), wt=wt.astype(BF16),
        wdw=jnp.repeat(w_dw[l], SUBLANES, axis=0), bdw=b_dw[l].reshape(1, D_CONV),
        lng=conv_ln_g[l].reshape(1, D_CONV), lnb=conv_ln_b[l].reshape(1, D_CONV),
        wup=w_gate_up[l].T.astype(BF16), bup=b_gate[l].reshape(D_GLA_K, 1),
        gnorm=gla_norm_g[l].reshape(1, GLA_DV),
        wout=w_out[l].astype(BF16),
    )


def kernel(x, c, w_ada, b_ada, g_pre_mix, g_post_mix, g_pre_ffn, g_post_ffn, w_in, w_dw, b_dw,
           conv_ln_g, conv_ln_b, w_gate_up, b_gate, gla_norm_g, w_out, w_router, b_router, w1, w3, w2):
    b, s, d = x.shape
    depth = w_ada.shape[0]
    t = b * s
    ts = min(SEQ_TILE, s)
    tm = min(MOE_TILE, t)
    tc = min(COMBINE_TILE, s)
    assert s % ts == 0 and ts % CHUNK == 0 and t % tm == 0 and s % tc == 0

    mod = _modulation(c, w_ada, b_ada).reshape(depth, b, 6, d)

    wr_hi, wr_lo = _split_bf16(w_router.T)
    tok = jnp.arange(min(ts, MXU_TILE), dtype=jnp.int32)
    same = (tok[:, None] // CHUNK) == (tok[None, :] // CHUNK)
    consts = dict(
        wr=jnp.concatenate([wr_hi, wr_lo], axis=0),
        br=b_router.reshape(N_EXPERTS, 1),
        ubd=(same & (tok[:, None] <= tok[None, :])).astype(BF16),
        ebd=same.astype(BF16),
        tri=(jnp.arange(ts)[:, None] <= jnp.arange(ts)[None, :]).astype(BF16),
    )

    w1b = _cast_pad(jnp.swapaxes(w1, 2, 3), D_EXPERT_PAD, d)
    w3b = _cast_pad(jnp.swapaxes(w3, 2, 3), D_EXPERT_PAD, d)
    w2b = _cast_pad(w2, D_EXPERT_PAD, d)

    for l in range(depth):
        lw = _layer_weights(l, g_pre_mix, g_post_mix, g_pre_ffn, g_post_ffn, w_in, w_dw, b_dw,
                            conv_ln_g, conv_ln_b, w_gate_up, b_gate, gla_norm_g, w_out)
        x1, h2p, route = _mix(x, mod[l], lw, consts, ts)
        cls = route[:, 0, :].astype(jnp.int32)
        rank = route[:, 1, :].astype(jnp.int32)
        pos, pad_start, pad_len, used_tiles, te1, te2, tvalid = _plan(cls, rank, tm)
        hs = _dispatch(h2p.reshape(t, d + 2 * LANES), pos, pad_start, pad_len, used_tiles,
                       te1.shape[0] * tm, min(DISPATCH_ROWS, t), tm)
        z = _moe(hs, te1, te2, tvalid, w1b, w3b, w2b, lw["gpostffn"], l, tm)
        g2 = mod[l][:, 5:6, :]
        x = _combine(x1.reshape(t, d), g2, z, pos, s, tc).reshape(b, s, d)
    return x
```
